```python
import math
import jax, jax.numpy as jnp
from jax import lax
import numpy as np

D_MODEL = 2048
BATCH = 4
SEQ = 2048
DEPTH = 1
DEC_BATCH = 8
DEC_SEQ = 8
PAST_LEN = 16384
PAGE_SIZE = 128

N_HEADS_ATTN = 8
HEAD_DIM_QK = 64
HEAD_DIM_V = 2 * HEAD_DIM_QK
ATTN_WIDTH = N_HEADS_ATTN * HEAD_DIM_V
CONV_WIDTH = D_MODEL - ATTN_WIDTH
CONV_GROUPS = 8
CONV_KERNEL = 31
D_FF = 5632
D_PLE = 256
Q_BLOCK = 128
RMS_EPS = 1e-6
LN_EPS = 1e-5
NEG_INF = -1e30
QK_COLS = N_HEADS_ATTN * 2 * HEAD_DIM_QK
IN_COLS = 2 * QK_COLS + ATTN_WIDTH + 2 * CONV_WIDTH
POOL_NUM, POOL_DEN = 5, 4

kernel_name = "hymba_diffattn_conformer_step"


def rms_norm(x, g):
    xf = x.astype(jnp.float32)
    y = xf * lax.rsqrt(jnp.mean(xf * xf, axis=-1, keepdims=True) + RMS_EPS)
    return (y * g).astype(x.dtype)


def group_layer_norm(x, g, b):
    xf = x.astype(jnp.float32).reshape(*x.shape[:-1], CONV_GROUPS, -1)
    mu = jnp.mean(xf, axis=-1, keepdims=True)
    var = jnp.mean(jnp.square(xf - mu), axis=-1, keepdims=True)
    y = ((xf - mu) * lax.rsqrt(var + LN_EPS)).reshape(x.shape)
    return (y * g + b).astype(x.dtype)


def swiglu(x, w_gate, w_up, w_down):
    return (jax.nn.silu(x @ w_gate) * (x @ w_up)) @ w_down


def alibi_slopes():
    return jnp.asarray(np.array([2.0 ** (-8.0 * (h + 1) / N_HEADS_ATTN) for h in range(N_HEADS_ATTN)], dtype=np.float32))


def alibi_bias(qpos, kpos):
    dist = (qpos[:, None] - kpos[None, :]).astype(jnp.float32)
    return jnp.where(dist[None] >= 0, -alibi_slopes()[:, None, None] * dist[None], NEG_INF)


def lambda_init(layer_idx):
    return 0.8 - 0.6 * math.exp(-0.3 * layer_idx)


def diff_weights(s, lam):
    p = jax.nn.softmax(s, axis=-1)
    return p[:, :, 0] - lam * p[:, :, 1]


def diff_attn_prompt(q, k, v, lam):
    b, s = q.shape[:2]
    nblk = s // Q_BLOCK
    scale = HEAD_DIM_QK ** -0.5
    kpos = jnp.arange(s)
    qb = q.reshape(b, nblk, Q_BLOCK, *q.shape[2:]).swapaxes(0, 1)

    def one_block(args):
        q_blk, i = args
        qpos = i * Q_BLOCK + jnp.arange(Q_BLOCK)
        sc = jnp.einsum('bqhmd,bkhmd->bhmqk', q_blk, k, preferred_element_type=jnp.float32) * scale
        w = diff_weights(sc + alibi_bias(qpos, kpos)[None, :, None], lam)
        return jnp.einsum('bhqk,bkhd->bqhd', w.astype(v.dtype), v, preferred_element_type=jnp.float32).astype(v.dtype)

    out = lax.map(one_block, (qb, jnp.arange(nblk)))
    return out.swapaxes(0, 1).reshape(b, s, N_HEADS_ATTN, HEAD_DIM_V)


def diff_attn_sample(q, k_new, v_new, k_past, v_past, lam):
    t = q.shape[1]
    p = k_past.shape[1]
    scale = HEAD_DIM_QK ** -0.5
    s_past = jnp.einsum('bqhmd,bkhmd->bhmqk', q, k_past, preferred_element_type=jnp.float32)
    s_new = jnp.einsum('bqhmd,bkhmd->bhmqk', q, k_new, preferred_element_type=jnp.float32)
    sc = jnp.concatenate([s_past, s_new], axis=-1) * scale
    qpos = p + jnp.arange(t)
    kpos = jnp.arange(p + t)
    w = diff_weights(sc + alibi_bias(qpos, kpos)[None, :, None], lam).astype(v_new.dtype)
    out = (jnp.einsum('bhqk,bkhd->bqhd', w[..., :p], v_past, preferred_element_type=jnp.float32)
           + jnp.einsum('bhqk,bkhd->bqhd', w[..., p:], v_new, preferred_element_type=jnp.float32))
    return out.astype(v_new.dtype)


def conformer_conv(c_in, conv_prev, w_dw, b_dw, g_cn, b_cn):
    a, g = jnp.split(c_in, 2, axis=-1)
    u = a * jax.nn.sigmoid(g)
    u_ext = jnp.concatenate([conv_prev.astype(u.dtype), u], axis=1)
    y = lax.conv_general_dilated(u_ext, w_dw[:, None, :].astype(u.dtype), window_strides=(1,), padding='VALID',
                                 dimension_numbers=('NWC', 'WIO', 'NWC'), feature_group_count=CONV_WIDTH) + b_dw
    y = jax.nn.silu(group_layer_norm(y, g_cn, b_cn))
    return y, u_ext[:, -(CONV_KERNEL - 1):]


def layer_forward(h, p_l, conv_prev, attend, layer_idx, W):
    b, s, _ = h.shape
    h = h + 0.5 * swiglu(rms_norm(h, W['g_ffn1']), W['w_ffn1_gate'], W['w_ffn1_up'], W['w_ffn1_down'])
    u = rms_norm(h, W['g_mix']) @ W['w_in']
    q = u[..., :QK_COLS].reshape(b, s, N_HEADS_ATTN, 2, HEAD_DIM_QK)
    k = u[..., QK_COLS:2 * QK_COLS].reshape(b, s, N_HEADS_ATTN, 2, HEAD_DIM_QK)
    v = u[..., 2 * QK_COLS:2 * QK_COLS + ATTN_WIDTH].reshape(b, s, N_HEADS_ATTN, HEAD_DIM_V)
    c_in = u[..., 2 * QK_COLS + ATTN_WIDTH:]
    lam_init = lambda_init(layer_idx)
    lam = (jnp.exp(jnp.sum(W['lambda_q1'].astype(jnp.float32) * W['lambda_k1'].astype(jnp.float32)))
           - jnp.exp(jnp.sum(W['lambda_q2'].astype(jnp.float32) * W['lambda_k2'].astype(jnp.float32))) + lam_init)
    o_attn = attend(q, k, v, lam)
    o_attn = (rms_norm(o_attn, W['g_subln']) * (1.0 - lam_init)).reshape(b, s, ATTN_WIDTH)
    o_conv, conv_new = conformer_conv(c_in, conv_prev, W['w_dw'], W['b_dw'], W['g_conv_norm'], W['b_conv_norm'])
    h = h + jnp.concatenate([o_attn, o_conv.astype(o_attn.dtype)], axis=-1) @ W['w_out']
    h = h + 0.5 * swiglu(rms_norm(h, W['g_ffn2']), W['w_ffn2_gate'], W['w_ffn2_up'], W['w_ffn2_down'])
    gate = jax.nn.sigmoid(rms_norm(h, W['g_ple']) @ W['w_ple_gate'])
    h = h + (p_l @ W['w_ple_proj']) * gate
    k_rows = k.reshape(b, s, N_HEADS_ATTN, 2 * HEAD_DIM_QK)
    return h, k_rows, v, conv_new


def setup_inputs(seed: int = 0) -> dict:
    key = jax.random.key(seed)
    ks = iter(jax.random.split(key, 40))
    f32 = jnp.float32
    n_pages = PAST_LEN // PAGE_SIZE
    n_pool = (DEC_BATCH * n_pages * POOL_NUM) // POOL_DEN

    def nrm(shape, scale=1.0):
        return jax.random.normal(next(ks), shape, f32) * scale

    def gain(shape):
        return 1.0 + nrm(shape, 0.05)

    page_table = jax.random.permutation(next(ks), n_pool)[:DEC_BATCH * n_pages].reshape(DEC_BATCH, n_pages).astype(jnp.int32)
    return {
        'x_prompt': nrm((BATCH, SEQ, D_MODEL)),
        'x_sample': nrm((DEC_BATCH, DEC_SEQ, D_MODEL)),
        'cache_k': nrm((DEPTH, n_pool, PAGE_SIZE, N_HEADS_ATTN, 2 * HEAD_DIM_QK)),
        'cache_v': nrm((DEPTH, n_pool, PAGE_SIZE, N_HEADS_ATTN, HEAD_DIM_V)),
        'state_conv': nrm((DEPTH, DEC_BATCH, CONV_KERNEL - 1, CONV_WIDTH)),
        'page_table': page_table,
        'p_prompt': nrm((DEPTH, BATCH, SEQ, D_PLE)),
        'p_sample': nrm((DEPTH, DEC_BATCH, DEC_SEQ, D_PLE)),
        'g_ffn1': gain((DEPTH, D_MODEL)),
        'w_ffn1_gate': nrm((DEPTH, D_MODEL, D_FF), D_MODEL ** -0.5),
        'w_ffn1_up': nrm((DEPTH, D_MODEL, D_FF), D_MODEL ** -0.5),
        'w_ffn1_down': nrm((DEPTH, D_FF, D_MODEL), D_FF ** -0.5),
        'g_mix': gain((DEPTH, D_MODEL)),
        'w_in': nrm((DEPTH, D_MODEL, IN_COLS), D_MODEL ** -0.5),
        'lambda_q1': nrm((DEPTH, HEAD_DIM_QK), 0.1),
        'lambda_k1': nrm((DEPTH, HEAD_DIM_QK), 0.1),
        'lambda_q2': nrm((DEPTH, HEAD_DIM_QK), 0.1),
        'lambda_k2': nrm((DEPTH, HEAD_DIM_QK), 0.1),
        'g_subln': gain((DEPTH, HEAD_DIM_V)),
        'w_dw': nrm((DEPTH, CONV_KERNEL, CONV_WIDTH), CONV_KERNEL ** -0.5),
        'b_dw': nrm((DEPTH, CONV_WIDTH), 0.01),
        'g_conv_norm': gain((DEPTH, CONV_WIDTH)),
        'b_conv_norm': nrm((DEPTH, CONV_WIDTH), 0.01),
        'w_out': nrm((DEPTH, D_MODEL, D_MODEL), D_MODEL ** -0.5),
        'g_ffn2': gain((DEPTH, D_MODEL)),
        'w_ffn2_gate': nrm((DEPTH, D_MODEL, D_FF), D_MODEL ** -0.5),
        'w_ffn2_up': nrm((DEPTH, D_MODEL, D_FF), D_MODEL ** -0.5),
        'w_ffn2_down': nrm((DEPTH, D_FF, D_MODEL), D_FF ** -0.5),
        'g_ple': gain((DEPTH, D_MODEL)),
        'w_ple_gate': nrm((DEPTH, D_MODEL, D_MODEL), D_MODEL ** -0.5),
        'w_ple_proj': nrm((DEPTH, D_PLE, D_MODEL), D_PLE ** -0.5),
        'g_final': gain((D_MODEL,)),
    }


def reference(x_prompt, x_sample, cache_k, cache_v, state_conv, page_table, p_prompt, p_sample,
              g_ffn1, w_ffn1_gate, w_ffn1_up, w_ffn1_down, g_mix, w_in,
              lambda_q1, lambda_k1, lambda_q2, lambda_k2, g_subln,
              w_dw, b_dw, g_conv_norm, b_conv_norm, w_out,
              g_ffn2, w_ffn2_gate, w_ffn2_up, w_ffn2_down,
              g_ple, w_ple_gate, w_ple_proj, g_final):
    n_pages = page_table.shape[1]
    past = n_pages * PAGE_SIZE
    h_p = x_prompt
    h_s = x_sample
    kp_l, vp_l, cp_l, ks_l, vs_l, cs_l = [], [], [], [], [], []
    for l in range(DEPTH):
        W = dict(g_ffn1=g_ffn1[l], w_ffn1_gate=w_ffn1_gate[l], w_ffn1_up=w_ffn1_up[l], w_ffn1_down=w_ffn1_down[l],
                 g_mix=g_mix[l], w_in=w_in[l],
                 lambda_q1=lambda_q1[l], lambda_k1=lambda_k1[l], lambda_q2=lambda_q2[l], lambda_k2=lambda_k2[l],
                 g_subln=g_subln[l], w_dw=w_dw[l], b_dw=b_dw[l], g_conv_norm=g_conv_norm[l], b_conv_norm=b_conv_norm[l],
                 w_out=w_out[l], g_ffn2=g_ffn2[l], w_ffn2_gate=w_ffn2_gate[l], w_ffn2_up=w_ffn2_up[l],
                 w_ffn2_down=w_ffn2_down[l], g_ple=g_ple[l], w_ple_gate=w_ple_gate[l], w_ple_proj=w_ple_proj[l])
        zero_conv = jnp.zeros((h_p.shape[0], CONV_KERNEL - 1, CONV_WIDTH), h_p.dtype)
        h_p, kp, vp, cp = layer_forward(h_p, p_prompt[l], zero_conv, diff_attn_prompt, l, W)
        k_past = cache_k[l][page_table].reshape(page_table.shape[0], past, N_HEADS_ATTN, 2, HEAD_DIM_QK)
        v_past = cache_v[l][page_table].reshape(page_table.shape[0], past, N_HEADS_ATTN, HEAD_DIM_V)
        attend_s = lambda q, k, v, lam, kpst=k_past, vpst=v_past: diff_attn_sample(q, k, v, kpst, vpst, lam)
        h_s, ksn, vsn, csn = layer_forward(h_s, p_sample[l], state_conv[l], attend_s, l, W)
        kp_l.append(kp); vp_l.append(vp); cp_l.append(cp)
        ks_l.append(ksn); vs_l.append(vsn); cs_l.append(csn)
    y_prompt = rms_norm(h_p, g_final)
    y_sample = rms_norm(h_s, g_final)
    k_prompt = jnp.stack(kp_l)
    v_prompt = jnp.stack(vp_l)
    conv_prompt = jnp.stack(cp_l)
    k_sample = jnp.stack(ks_l)
    v_sample = jnp.stack(vs_l)
    conv_sample = jnp.stack(cs_l)
    return (y_prompt, y_sample, k_prompt, v_prompt, conv_prompt, k_sample, v_sample, conv_sample)
```

```python
import functools
import math

import jax
import jax.numpy as jnp
from jax import lax
from jax.experimental import pallas as pl
from jax.experimental.pallas import tpu as pltpu

F32 = jnp.float32
BF16 = jnp.bfloat16

HEAD_DIM_QK = 64
HEAD_DIM_V = 2 * HEAD_DIM_QK
CONV_GROUPS = 8
CONV_KERNEL = 31
PAGE_SIZE = 128
RMS_EPS = 1e-6
LN_EPS = 1e-5
NEG_INF = -1e30

V7X_VMEM_BYTES = 64 * 1024 * 1024
LANES = 128
SUBLANES = 8
CONV_PAD = 32


def _vmem_limit(nbytes):
    return int(min(V7X_VMEM_BYTES - 4 * 1024 * 1024, nbytes + 16 * 1024 * 1024))


def _params(sem, nbytes):
    return pltpu.CompilerParams(dimension_semantics=sem, vmem_limit_bytes=_vmem_limit(nbytes))


def _rms(x, g):
    ms = jnp.mean(x * x, axis=-1, keepdims=True)
    return x * lax.rsqrt(ms + RMS_EPS) * g


def _sigmoid(x):
    return 1.0 / (1.0 + jnp.exp(-x))


def _pick(n, pref):
    t = min(n, pref)
    assert n % t == 0, (n, t)
    return t


def _ffn_up_kernel(x_ref, g_ref, wg_ref, wu_ref, o_ref, n_ref):
    @pl.when(pl.program_id(1) == 0)
    def _():
        n_ref[...] = _rms(x_ref[...], g_ref[...]).astype(BF16)

    n = n_ref[...]
    a = jnp.dot(n, wg_ref[...], preferred_element_type=F32)
    u = jnp.dot(n, wu_ref[...], preferred_element_type=F32)
    o_ref[...] = (a * _sigmoid(a) * u).astype(BF16)


def _ffn_up(x, g, wg, wu, tm, tf):
    m, d = x.shape
    f = wg.shape[1]
    nbytes = 2 * tm * d * 4 + tm * d * 2 + 2 * 2 * d * tf * 2 + 2 * tm * tf * 2 + 3 * tm * tf * 4
    return pl.pallas_call(
        _ffn_up_kernel,
        grid=(m // tm, f // tf),
        in_specs=[
            pl.BlockSpec((tm, d), lambda i, j: (i, 0)),
            pl.BlockSpec((1, d), lambda i, j: (0, 0)),
            pl.BlockSpec((d, tf), lambda i, j: (0, j)),
            pl.BlockSpec((d, tf), lambda i, j: (0, j)),
        ],
        out_specs=pl.BlockSpec((tm, tf), lambda i, j: (i, j)),
        out_shape=jax.ShapeDtypeStruct((m, f), BF16),
        scratch_shapes=[pltpu.VMEM((tm, d), BF16)],
        compiler_params=_params(("parallel", "arbitrary"), nbytes),
        name="ffn_up",
    )(x, g, wg, wu)


def _ffn_down_kernel(h_ref, w_ref, r_ref, o_ref):
    o_ref[...] = r_ref[...] + 0.5 * jnp.dot(h_ref[...], w_ref[...], preferred_element_type=F32)


def _ffn_down(hid, wd, res, tm, tn):
    m, f = hid.shape
    d = wd.shape[1]
    nbytes = 2 * tm * f * 2 + 2 * f * tn * 2 + 4 * tm * tn * 4 + tm * tn * 4
    return pl.pallas_call(
        _ffn_down_kernel,
        grid=(m // tm, d // tn),
        in_specs=[
            pl.BlockSpec((tm, f), lambda i, j: (i, 0)),
            pl.BlockSpec((f, tn), lambda i, j: (0, j)),
            pl.BlockSpec((tm, tn), lambda i, j: (i, j)),
        ],
        out_specs=pl.BlockSpec((tm, tn), lambda i, j: (i, j)),
        out_shape=jax.ShapeDtypeStruct((m, d), F32),
        compiler_params=_params(("parallel", "arbitrary"), nbytes),
        name="ffn_down",
    )(hid, wd, res)


def _q_proj_kernel(scale, x_ref, g_ref, w_ref, o_ref, n_ref):
    @pl.when(pl.program_id(1) == 0)
    def _():
        n_ref[...] = _rms(x_ref[...], g_ref[...]).astype(BF16)

    u = jnp.dot(n_ref[...], w_ref[...], preferred_element_type=F32)
    o_ref[...] = (u * scale).astype(BF16)


def _kv_proj_kernel(x_ref, g_ref, w_ref, o32_ref, o16_ref, n_ref):
    @pl.when(pl.program_id(1) == 0)
    def _():
        n_ref[...] = _rms(x_ref[...], g_ref[...]).astype(BF16)

    u = jnp.dot(n_ref[...], w_ref[...], preferred_element_type=F32)
    o32_ref[...] = u
    o16_ref[...] = u.astype(BF16)


def _glu_proj_kernel(x_ref, g_ref, wa_ref, wg_ref, o_ref, n_ref):
    @pl.when(pl.program_id(1) == 0)
    def _():
        n_ref[...] = _rms(x_ref[...], g_ref[...]).astype(BF16)

    n = n_ref[...]
    a = jnp.dot(n, wa_ref[...], preferred_element_type=F32)
    gt = jnp.dot(n, wg_ref[...], preferred_element_type=F32)
    o_ref[...] = a * _sigmoid(gt)


def _in_proj(x, g, w_in, qk_cols, attn_width, conv_width, tm, tn):
    m, d = x.shape
    scale = HEAD_DIM_QK ** -0.5
    x_spec = pl.BlockSpec((tm, d), lambda i, j: (i, 0))
    g_spec = pl.BlockSpec((1, d), lambda i, j: (0, 0))

    def w_spec(col0):
        off = col0 // tn
        return pl.BlockSpec((d, tn), lambda i, j: (0, j + off))

    o_spec = pl.BlockSpec((tm, tn), lambda i, j: (i, j))
    scratch = [pltpu.VMEM((tm, d), BF16)]
    nbytes = 2 * tm * d * 4 + tm * d * 2 + 4 * d * tn * 2 + 8 * tm * tn * 4

    q16 = pl.pallas_call(
        functools.partial(_q_proj_kernel, scale),
        grid=(m // tm, qk_cols // tn),
        in_specs=[x_spec, g_spec, w_spec(0)],
        out_specs=o_spec,
        out_shape=jax.ShapeDtypeStruct((m, qk_cols), BF16),
        scratch_shapes=scratch,
        compiler_params=_params(("parallel", "arbitrary"), nbytes),
        name="q_proj",
    )(x, g, w_in)

    def proj32_16(col0, cols, name):
        return pl.pallas_call(
            _kv_proj_kernel,
            grid=(m // tm, cols // tn),
            in_specs=[x_spec, g_spec, w_spec(col0)],
            out_specs=[o_spec, o_spec],
            out_shape=[jax.ShapeDtypeStruct((m, cols), F32), jax.ShapeDtypeStruct((m, cols), BF16)],
            scratch_shapes=scratch,
            compiler_params=_params(("parallel", "arbitrary"), nbytes),
            name=name,
        )(x, g, w_in)

    k32, k16 = proj32_16(qk_cols, qk_cols, "k_proj")
    v32, v16 = proj32_16(2 * qk_cols, attn_width, "v_proj")

    c0 = 2 * qk_cols + attn_width
    glu = pl.pallas_call(
        _glu_proj_kernel,
        grid=(m // tm, conv_width // tn),
        in_specs=[x_spec, g_spec, w_spec(c0), w_spec(c0 + conv_width)],
        out_specs=o_spec,
        out_shape=jax.ShapeDtypeStruct((m, conv_width), F32),
        scratch_shapes=scratch,
        compiler_params=_params(("parallel", "arbitrary"), nbytes),
        name="glu_proj",
    )(x, g, w_in, w_in)
    return q16, k32, k16, v32, v16, glu


def _split_maps(q):
    lane = lax.broadcasted_iota(jnp.int32, q.shape, 1)
    zero = jnp.zeros_like(q)
    return jnp.concatenate([jnp.where(lane < HEAD_DIM_QK, q, zero), jnp.where(lane >= HEAD_DIM_QK, q, zero)], axis=0)


def _subln(o, g, post_scale):
    ms = jnp.mean(o * o, axis=-1, keepdims=True)
    return o * lax.rsqrt(ms + RMS_EPS) * g * post_scale


def _prompt_attn_kernel(post_scale, tq, n_heads, sc_ref, q_ref, k_ref, v_ref, g_ref, o_ref):
    h = pl.program_id(1)
    qi = pl.program_id(2)
    slope = sc_ref[h]
    lam = sc_ref[n_heads]
    qq = _split_maps(q_ref[...])

    def block(ki, carry, masked):
        m_prev, l_prev, acc = carry
        start = pl.multiple_of(ki * tq, tq)
        k = k_ref[pl.ds(start, tq), :]
        v = v_ref[pl.ds(start, tq), :]
        s = lax.dot_general(qq, k, (((1,), (1,)), ((), ())), preferred_element_type=F32)
        col = lax.broadcasted_iota(jnp.int32, (1, tq), 1)
        rel = (ki - qi) * tq + col
        s = s + slope * rel.astype(F32)
        if masked:
            row = lax.broadcasted_iota(jnp.int32, (2 * tq, tq), 0)
            row = jnp.where(row >= tq, row - tq, row)
            colf = lax.broadcasted_iota(jnp.int32, (2 * tq, tq), 1)
            s = jnp.where(colf <= row, s, NEG_INF)
        m_new = jnp.maximum(m_prev, jnp.max(s, axis=-1, keepdims=True))
        alpha = jnp.exp(m_prev - m_new)
        p = jnp.exp(s - m_new)
        l_new = alpha * l_prev + jnp.sum(p, axis=-1, keepdims=True)
        acc = alpha * acc + jnp.dot(p.astype(BF16), v, preferred_element_type=F32)
        return m_new, l_new, acc

    init = (jnp.full((2 * tq, 1), NEG_INF, F32), jnp.zeros((2 * tq, 1), F32), jnp.zeros((2 * tq, HEAD_DIM_V), F32))
    carry = lax.fori_loop(0, qi, lambda ki, c: block(ki, c, False), init)
    _, l, acc = block(qi, carry, True)
    o = acc / l
    o = o[:tq] - lam * o[tq:]
    o_ref[...] = _subln(o, g_ref[...], post_scale).astype(o_ref.dtype)


def _prompt_attn(q16, k16, v16, g_subln, scal, batch, seq, n_heads, post_scale, tq):
    nq = seq // tq
    grid_spec = pltpu.PrefetchScalarGridSpec(
        num_scalar_prefetch=1,
        grid=(batch, n_heads, nq),
        in_specs=[
            pl.BlockSpec((tq, HEAD_DIM_V), lambda b, h, i, sc: (b * nq + i, h)),
            pl.BlockSpec((seq, HEAD_DIM_V), lambda b, h, i, sc: (b, h)),
            pl.BlockSpec((seq, HEAD_DIM_V), lambda b, h, i, sc: (b, h)),
            pl.BlockSpec((1, HEAD_DIM_V), lambda b, h, i, sc: (0, 0)),
        ],
        out_specs=pl.BlockSpec((tq, HEAD_DIM_V), lambda b, h, i, sc: (b * nq + i, h)),
    )
    nbytes = 4 * seq * HEAD_DIM_V * 2 + 4 * tq * HEAD_DIM_V * 2 + 8 * 2 * tq * tq * 4
    return pl.pallas_call(
        functools.partial(_prompt_attn_kernel, post_scale, tq, n_heads),
        grid_spec=grid_spec,
        out_shape=jax.ShapeDtypeStruct((batch * seq, n_heads * HEAD_DIM_V), BF16),
        compiler_params=_params(("parallel", "parallel", "arbitrary"), nbytes),
        name="prompt_attn",
    )(scal, q16, k16, v16, g_subln)


def _decode_attn_kernel(post_scale, n_heads, t_new, past_len, pt_ref, sc_ref, qb_ref, k_ref, v_ref, kn_ref, vn_ref,
                        g_ref, o_ref, m_ref, l_ref, acc_ref):
    p = pl.program_id(1)
    rows = 2 * n_heads * t_new
    rows_per_head = 2 * t_new

    @pl.when(p == 0)
    def _():
        m_ref[...] = jnp.full(m_ref.shape, NEG_INF, F32)
        l_ref[...] = jnp.zeros(l_ref.shape, F32)
        acc_ref[...] = jnp.zeros(acc_ref.shape, F32)

    row = lax.broadcasted_iota(jnp.int32, (rows, 1), 0)
    head = row // rows_per_head
    slope = jnp.zeros((rows, 1), F32)
    for hh in range(n_heads):
        slope = jnp.where(head == hh, sc_ref[hh], slope)
    qb = qb_ref[0]

    def update(k16, v16, rel, mask):
        s = lax.dot_general(qb, k16, (((1,), (1,)), ((), ())), preferred_element_type=F32)
        s = s + slope * rel
        if mask is not None:
            s = jnp.where(mask, s, NEG_INF)
        m_prev = m_ref[...]
        m_new = jnp.maximum(m_prev, jnp.max(s, axis=-1, keepdims=True))
        alpha = jnp.exp(m_prev - m_new)
        pr = jnp.exp(s - m_new)
        l_ref[...] = alpha * l_ref[...] + jnp.sum(pr, axis=-1, keepdims=True)
        m_ref[...] = m_new
        full = jnp.dot(pr.astype(BF16), v16, preferred_element_type=F32)
        pv = jnp.concatenate(
            [full[hh * rows_per_head:(hh + 1) * rows_per_head, hh * HEAD_DIM_V:(hh + 1) * HEAD_DIM_V]
             for hh in range(n_heads)], axis=0)
        acc_ref[...] = alpha * acc_ref[...] + pv

    col = lax.broadcasted_iota(jnp.int32, (1, PAGE_SIZE), 1)
    rel_past = (p * PAGE_SIZE - past_len + col).astype(F32)
    update(k_ref[0].astype(BF16), v_ref[0].astype(BF16), rel_past, None)

    @pl.when(p == pl.num_programs(1) - 1)
    def _():
        pad = jnp.zeros((PAGE_SIZE - t_new, kn_ref.shape[-1]), F32)
        kn = jnp.concatenate([kn_ref[0], pad], axis=0).astype(BF16)
        vn = jnp.concatenate([vn_ref[0], pad], axis=0).astype(BF16)
        tok = row % t_new
        colf = lax.broadcasted_iota(jnp.int32, (rows, PAGE_SIZE), 1)
        update(kn, vn, col.astype(F32), colf <= tok)
        o = acc_ref[...] / l_ref[...]
        lam = sc_ref[n_heads]
        g = g_ref[...]
        outs = []
        for hh in range(n_heads):
            r0 = hh * rows_per_head
            oh = o[r0:r0 + t_new] - lam * o[r0 + t_new:r0 + 2 * t_new]
            outs.append(_subln(oh, g, post_scale))
        o_ref[0] = jnp.concatenate(outs, axis=1).astype(o_ref.dtype)


def _decode_attn(qblk, cache_k, cache_v, k_new, v_new, g_subln, page_table, scal, n_heads, t_new, post_scale):
    batch, n_pages = page_table.shape
    rows = 2 * n_heads * t_new
    width = n_heads * HEAD_DIM_V
    past_len = n_pages * PAGE_SIZE
    grid_spec = pltpu.PrefetchScalarGridSpec(
        num_scalar_prefetch=2,
        grid=(batch, n_pages),
        in_specs=[
            pl.BlockSpec((1, rows, width), lambda b, p, pt, sc: (b, 0, 0)),
            pl.BlockSpec((1, PAGE_SIZE, width), lambda b, p, pt, sc: (pt[b, p], 0, 0)),
            pl.BlockSpec((1, PAGE_SIZE, width), lambda b, p, pt, sc: (pt[b, p], 0, 0)),
            pl.BlockSpec((1, t_new, width), lambda b, p, pt, sc: (b, 0, 0)),
            pl.BlockSpec((1, t_new, width), lambda b, p, pt, sc: (b, 0, 0)),
            pl.BlockSpec((1, HEAD_DIM_V), lambda b, p, pt, sc: (0, 0)),
        ],
        out_specs=pl.BlockSpec((1, t_new, width), lambda b, p, pt, sc: (b, 0, 0)),
        scratch_shapes=[pltpu.VMEM((rows, 1), F32), pltpu.VMEM((rows, 1), F32), pltpu.VMEM((rows, HEAD_DIM_V), F32)],
    )
    nbytes = 4 * PAGE_SIZE * width * 4 + 2 * rows * width * 2 + 4 * rows * width * 4
    return pl.pallas_call(
        functools.partial(_decode_attn_kernel, post_scale, n_heads, t_new, past_len),
        grid_spec=grid_spec,
        out_shape=jax.ShapeDtypeStruct((batch, t_new, width), BF16),
        compiler_params=_params(("parallel", "arbitrary"), nbytes),
        name="decode_attn",
    )(page_table, scal, qblk, cache_k, cache_v, k_new, v_new, g_subln)


def _conv_kernel(seq, chunk, u_ref, prev_ref, w_ref, bdw_ref, gcn_ref, bcn_ref, o_ref, new_ref, ext_ref):
    hist = CONV_KERNEL - 1
    lead = CONV_PAD - hist
    ext_ref[0:lead, :] = jnp.zeros((lead, LANES), F32)
    ext_ref[lead:CONV_PAD, :] = prev_ref[0]
    ext_ref[CONV_PAD:CONV_PAD + seq, :] = u_ref[0]
    new_ref[0] = ext_ref[lead + seq:CONV_PAD + seq, :]
    bdw = bdw_ref[...]
    gcn = gcn_ref[...]
    bcn = bcn_ref[...]

    def body(c, _):
        r0 = pl.multiple_of(c * chunk, chunk)
        acc = jnp.zeros((chunk, LANES), F32)
        for j in range(CONV_KERNEL):
            acc = acc + w_ref[j:j + 1, :] * ext_ref[pl.ds(r0 + (lead + j), chunk), :]
        y = acc + bdw
        mu = jnp.mean(y, axis=-1, keepdims=True)
        yc = y - mu
        var = jnp.mean(yc * yc, axis=-1, keepdims=True)
        yn = yc * lax.rsqrt(var + LN_EPS) * gcn + bcn
        o_ref[0, pl.ds(r0, chunk), :] = (yn * _sigmoid(yn)).astype(o_ref.dtype)
        return 0

    lax.fori_loop(0, seq // chunk, body, 0)


def _conv_module(u, prev, w_dw, b_dw, g_cn, b_cn):
    batch, seq, width = u.shape
    assert width // CONV_GROUPS == LANES
    chunk = _pick(seq, 64)
    hist = CONV_KERNEL - 1
    vec = pl.BlockSpec((1, LANES), lambda b, c: (0, c))
    return pl.pallas_call(
        functools.partial(_conv_kernel, seq, chunk),
        grid=(batch, width // LANES),
        in_specs=[
            pl.BlockSpec((1, seq, LANES), lambda b, c: (b, 0, c)),
            pl.BlockSpec((1, hist, LANES), lambda b, c: (b, 0, c)),
            pl.BlockSpec((CONV_KERNEL, LANES), lambda b, c: (0, c)),
            vec, vec, vec,
        ],
        out_specs=[
            pl.BlockSpec((1, seq, LANES), lambda b, c: (b, 0, c)),
            pl.BlockSpec((1, hist, LANES), lambda b, c: (b, 0, c)),
        ],
        out_shape=[jax.ShapeDtypeStruct((batch, seq, width), BF16), jax.ShapeDtypeStruct((batch, hist, width), F32)],
        scratch_shapes=[pltpu.VMEM((CONV_PAD + seq, LANES), F32)],
        compiler_params=_params(("parallel", "parallel"), 8 * (seq + CONV_PAD) * LANES * 4),
        name="conv_module",
    )(u, prev, w_dw, b_dw, g_cn, b_cn)


def _out_proj_kernel(ka, a_ref, b_ref, w_ref, r_ref, o_ref):
    acc = jnp.dot(a_ref[...], w_ref[0:ka, :], preferred_element_type=F32)
    acc = acc + jnp.dot(b_ref[...], w_ref[ka:, :], preferred_element_type=F32)
    o_ref[...] = r_ref[...] + acc


def _out_proj(a, b, w, res, tm, tn):
    m, ka = a.shape
    kb = b.shape[1]
    d = w.shape[1]
    nbytes = 2 * tm * (ka + kb) * 2 + 2 * (ka + kb) * tn * 2 + 5 * tm * tn * 4
    return pl.pallas_call(
        functools.partial(_out_proj_kernel, ka),
        grid=(m // tm, d // tn),
        in_specs=[
            pl.BlockSpec((tm, ka), lambda i, j: (i, 0)),
            pl.BlockSpec((tm, kb), lambda i, j: (i, 0)),
            pl.BlockSpec((ka + kb, tn), lambda i, j: (0, j)),
            pl.BlockSpec((tm, tn), lambda i, j: (i, j)),
        ],
        out_specs=pl.BlockSpec((tm, tn), lambda i, j: (i, j)),
        out_shape=jax.ShapeDtypeStruct((m, d), F32),
        compiler_params=_params(("parallel", "arbitrary"), nbytes),
        name="out_proj",
    )(a, b, w, res)


def _ple_kernel(h_ref, g_ref, wg_ref, p_ref, wp_ref, o_ref):
    h = h_ref[...]
    n = _rms(h, g_ref[...]).astype(BF16)
    gate = _sigmoid(jnp.dot(n, wg_ref[...], preferred_element_type=F32))
    proj = jnp.dot(p_ref[...].astype(BF16), wp_ref[...], preferred_element_type=F32)
    o_ref[...] = h + proj * gate


def _ple(h, g, wg, p, wp, tm):
    m, d = h.shape
    dp = p.shape[1]
    nbytes = 4 * tm * d * 4 + 2 * d * d * 2 + 2 * dp * d * 2 + 2 * tm * dp * 4 + 4 * tm * d * 4
    return pl.pallas_call(
        _ple_kernel,
        grid=(m // tm,),
        in_specs=[
            pl.BlockSpec((tm, d), lambda i: (i, 0)),
            pl.BlockSpec((1, d), lambda i: (0, 0)),
            pl.BlockSpec((d, d), lambda i: (0, 0)),
            pl.BlockSpec((tm, dp), lambda i: (i, 0)),
            pl.BlockSpec((dp, d), lambda i: (0, 0)),
        ],
        out_specs=pl.BlockSpec((tm, d), lambda i: (i, 0)),
        out_shape=jax.ShapeDtypeStruct((m, d), F32),
        compiler_params=_params(("parallel",), nbytes),
        name="ple",
    )(h, g, wg, p, wp)


def _final_norm_kernel(h_ref, g_ref, o_ref):
    o_ref[...] = _rms(h_ref[...], g_ref[...])


def _final_norm(h, g, tm):
    m, d = h.shape
    return pl.pallas_call(
        _final_norm_kernel,
        grid=(m // tm,),
        in_specs=[pl.BlockSpec((tm, d), lambda i: (i, 0)), pl.BlockSpec((1, d), lambda i: (0, 0))],
        out_specs=pl.BlockSpec((tm, d), lambda i: (i, 0)),
        out_shape=jax.ShapeDtypeStruct((m, d), F32),
        compiler_params=_params(("parallel",), 4 * tm * d * 4),
        name="final_norm",
    )(h, g)


def _lambda_init(layer_idx):
    return 0.8 - 0.6 * math.exp(-0.3 * layer_idx)


def kernel(x_prompt, x_sample, cache_k, cache_v, state_conv, page_table, p_prompt, p_sample, g_ffn1, w_ffn1_gate, w_ffn1_up, w_ffn1_down, g_mix, w_in, lambda_q1, lambda_k1, lambda_q2, lambda_k2, g_subln, w_dw, b_dw, g_conv_norm, b_conv_norm, w_out, g_ffn2, w_ffn2_gate, w_ffn2_up, w_ffn2_down, g_ple, w_ple_gate, w_ple_proj, g_final):
    depth = w_in.shape[0]
    batch, seq, d_model = x_prompt.shape
    dec_batch, dec_seq, _ = x_sample.shape
    n_heads = cache_k.shape[3]
    assert cache_k.shape[4] == 2 * HEAD_DIM_QK and cache_v.shape[4] == HEAD_DIM_V and cache_k.shape[2] == PAGE_SIZE
    qk_cols = n_heads * 2 * HEAD_DIM_QK
    attn_width = n_heads * HEAD_DIM_V
    conv_width = d_model - attn_width
    n_pool = cache_k.shape[1]

    h_p = x_prompt.reshape(batch * seq, d_model)
    h_s = x_sample.reshape(dec_batch * dec_seq, d_model)
    tm_p = _pick(batch * seq, 512)
    tm_s = dec_batch * dec_seq
    tq = _pick(seq, 256)
    slopes = jnp.asarray([2.0 ** (-8.0 * (h + 1) / n_heads) for h in range(n_heads)], F32)
    row2 = lambda a: a.reshape(1, -1)

    outs = {name: [] for name in ("kp", "vp", "cp", "ks", "vs", "cs")}
    for l in range(depth):
        lam_init = _lambda_init(l)
        lam = (jnp.exp(jnp.sum(lambda_q1[l] * lambda_k1[l])) - jnp.exp(jnp.sum(lambda_q2[l] * lambda_k2[l])) + lam_init)
        scal = jnp.concatenate([slopes, lam.reshape(1).astype(F32)])
        post_scale = 1.0 - lam_init
        w1g, w1u, w1d = w_ffn1_gate[l].astype(BF16), w_ffn1_up[l].astype(BF16), w_ffn1_down[l].astype(BF16)
        w2g, w2u, w2d = w_ffn2_gate[l].astype(BF16), w_ffn2_up[l].astype(BF16), w_ffn2_down[l].astype(BF16)
        w_in16 = w_in[l].astype(BF16)
        w_out16 = w_out[l].astype(BF16)
        w_pg16 = w_ple_gate[l].astype(BF16)
        w_pp16 = w_ple_proj[l].astype(BF16)
        d_ff = w1g.shape[1]
        tf = _pick(d_ff, 512)

        def ffn(h, g, wg, wu, wd, tm):
            hid = _ffn_up(h, row2(g), wg, wu, tm, tf)
            return _ffn_down(hid, wd, h, tm, _pick(d_model, 512))

        def pre_mix(h, tm):
            h = ffn(h, g_ffn1[l], w1g, w1u, w1d, tm)
            return (h,) + _in_proj(h, row2(g_mix[l]), w_in16, qk_cols, attn_width, conv_width, tm, 512)

        def post_mix(h, o_attn, o_conv, p_l, tm):
            h = _out_proj(o_attn, o_conv, w_out16, h, tm, _pick(d_model, 512))
            h = ffn(h, g_ffn2[l], w2g, w2u, w2d, tm)
            return _ple(h, row2(g_ple[l]), w_pg16, p_l, w_pp16, min(tm, 256))

        conv_args = (w_dw[l], row2(b_dw[l]), row2(g_conv_norm[l]), row2(b_conv_norm[l]))

        h_p, q16, k32, k16, v32, v16, glu = pre_mix(h_p, tm_p)
        o_attn = _prompt_attn(q16, k16, v16, row2(g_subln[l]), scal, batch, seq, n_heads, post_scale, tq)
        zero_hist = jnp.zeros((batch, CONV_KERNEL - 1, conv_width), F32)
        o_conv, conv_new = _conv_module(glu.reshape(batch, seq, conv_width), zero_hist, *conv_args)
        h_p = post_mix(h_p, o_attn, o_conv.reshape(batch * seq, conv_width), p_prompt[l].reshape(batch * seq, -1), tm_p)
        outs["kp"].append(k32.reshape(batch, seq, n_heads, 2 * HEAD_DIM_QK))
        outs["vp"].append(v32.reshape(batch, seq, n_heads, HEAD_DIM_V))
        outs["cp"].append(conv_new)

        h_s, q16, k32, k16, v32, v16, glu = pre_mix(h_s, tm_s)
        k_new = k32.reshape(dec_batch, dec_seq, qk_cols)
        v_new = v32.reshape(dec_batch, dec_seq, attn_width)
        q5 = q16.reshape(dec_batch, dec_seq, n_heads, 2, HEAD_DIM_QK)
        eye_h = jnp.eye(n_heads, dtype=BF16)
        eye_m = jnp.eye(2, dtype=BF16)
        qblk = jnp.einsum("bthmd,hg,mn->bhmtgnd", q5, eye_h, eye_m).reshape(dec_batch, 2 * n_heads * dec_seq, qk_cols)
        o_attn = _decode_attn(qblk, cache_k[l].reshape(n_pool, PAGE_SIZE, qk_cols),
                              cache_v[l].reshape(n_pool, PAGE_SIZE, attn_width), k_new, v_new, row2(g_subln[l]),
                              page_table, scal, n_heads, dec_seq, post_scale)
        o_conv, conv_new = _conv_module(glu.reshape(dec_batch, dec_seq, conv_width), state_conv[l], *conv_args)
        h_s = post_mix(h_s, o_attn.reshape(dec_batch * dec_seq, attn_width),
                       o_conv.reshape(dec_batch * dec_seq, conv_width), p_sample[l].reshape(dec_batch * dec_seq, -1), tm_s)
        outs["ks"].append(k_new.reshape(dec_batch, dec_seq, n_heads, 2 * HEAD_DIM_QK))
        outs["vs"].append(v_new.reshape(dec_batch, dec_seq, n_heads, HEAD_DIM_V))
        outs["cs"].append(conv_new)

    y_prompt = _final_norm(h_p, row2(g_final), tm_p).reshape(batch, seq, d_model)
    y_sample = _final_norm(h_s, row2(g_final), tm_s).reshape(dec_batch, dec_seq, d_model)
    return (y_prompt, y_sample, jnp.stack(outs["kp"]), jnp.stack(outs["vp"]), jnp.stack(outs["cp"]),
            jnp.stack(outs["ks"]), jnp.stack(outs["vs"]), jnp.stack(outs["cs"]))
```

```python
import functools
import math

import jax
import jax.numpy as jnp
from jax import lax
from jax.experimental import pallas as pl
from jax.experimental.pallas import tpu as pltpu

F32 = jnp.float32
BF16 = jnp.bfloat16

HEAD_DIM_QK = 64
HEAD_DIM_V = 2 * HEAD_DIM_QK
CONV_GROUPS = 8
CONV_KERNEL = 31
PAGE_SIZE = 128
RMS_EPS = 1e-6
LN_EPS = 1e-5
NEG_INF = -1e30

V7X_VMEM_BYTES = 64 * 1024 * 1024
LANES = 128
SUBLANES = 8
CONV_PAD = 32
CONV_PARTIALS = 4
DECODE_PAGES_PER_STEP = 4
TOKEN_TILE = 1024


def _vmem_limit(nbytes):
    return int(min(V7X_VMEM_BYTES - 4 * 1024 * 1024, nbytes + 16 * 1024 * 1024))


def _params(sem, nbytes):
    return pltpu.CompilerParams(dimension_semantics=sem, vmem_limit_bytes=_vmem_limit(nbytes))


def _rms(x, g):
    ms = jnp.mean(x * x, axis=-1, keepdims=True)
    return x * lax.rsqrt(ms + RMS_EPS) * g


def _sigmoid(x):
    return 1.0 / (1.0 + jnp.exp(-x))


def _pick(n, pref):
    t = min(n, pref)
    assert n % t == 0, (n, t)
    return t


def _ffn_up_kernel(x_ref, g_ref, wg_ref, wu_ref, o_ref, n_ref):
    @pl.when(pl.program_id(1) == 0)
    def _():
        n_ref[...] = _rms(x_ref[...], g_ref[...]).astype(BF16)

    n = n_ref[...]
    a = jnp.dot(n, wg_ref[...], preferred_element_type=F32)
    u = jnp.dot(n, wu_ref[...], preferred_element_type=F32)
    o_ref[...] = (a * _sigmoid(a) * u).astype(BF16)


def _ffn_up(x, g, wg, wu, tm, tf):
    m, d = x.shape
    f = wg.shape[1]
    nbytes = 2 * tm * d * 4 + tm * d * 2 + 2 * 2 * d * tf * 2 + 2 * tm * tf * 2 + 3 * tm * tf * 4
    return pl.pallas_call(
        _ffn_up_kernel,
        grid=(m // tm, f // tf),
        in_specs=[
            pl.BlockSpec((tm, d), lambda i, j: (i, 0)),
            pl.BlockSpec((1, d), lambda i, j: (0, 0)),
            pl.BlockSpec((d, tf), lambda i, j: (0, j)),
            pl.BlockSpec((d, tf), lambda i, j: (0, j)),
        ],
        out_specs=pl.BlockSpec((tm, tf), lambda i, j: (i, j)),
        out_shape=jax.ShapeDtypeStruct((m, f), BF16),
        scratch_shapes=[pltpu.VMEM((tm, d), BF16)],
        compiler_params=_params(("parallel", "arbitrary"), nbytes),
        name="ffn_up",
    )(x, g, wg, wu)


def _ffn_down_kernel(h_ref, w_ref, r_ref, o_ref):
    o_ref[...] = r_ref[...] + 0.5 * jnp.dot(h_ref[...], w_ref[...], preferred_element_type=F32)


def _ffn_down(hid, wd, res, tm, tn):
    m, f = hid.shape
    d = wd.shape[1]
    nbytes = 2 * tm * f * 2 + 2 * f * tn * 2 + 4 * tm * tn * 4 + tm * tn * 4
    return pl.pallas_call(
        _ffn_down_kernel,
        grid=(m // tm, d // tn),
        in_specs=[
            pl.BlockSpec((tm, f), lambda i, j: (i, 0)),
            pl.BlockSpec((f, tn), lambda i, j: (0, j)),
            pl.BlockSpec((tm, tn), lambda i, j: (i, j)),
        ],
        out_specs=pl.BlockSpec((tm, tn), lambda i, j: (i, j)),
        out_shape=jax.ShapeDtypeStruct((m, d), F32),
        compiler_params=_params(("parallel", "arbitrary"), nbytes),
        name="ffn_down",
    )(hid, wd, res)


def _q_proj_kernel(scale, x_ref, g_ref, w_ref, o_ref, n_ref):
    @pl.when(pl.program_id(1) == 0)
    def _():
        n_ref[...] = _rms(x_ref[...], g_ref[...]).astype(BF16)

    u = jnp.dot(n_ref[...], w_ref[...], preferred_element_type=F32)
    o_ref[...] = (u * scale).astype(BF16)


def _kv_proj_kernel(x_ref, g_ref, w_ref, o32_ref, o16_ref, n_ref):
    @pl.when(pl.program_id(1) == 0)
    def _():
        n_ref[...] = _rms(x_ref[...], g_ref[...]).astype(BF16)

    u = jnp.dot(n_ref[...], w_ref[...], preferred_element_type=F32)
    o32_ref[...] = u
    o16_ref[...] = u.astype(BF16)


def _glu_proj_kernel(x_ref, g_ref, wa_ref, wg_ref, o_ref, n_ref):
    @pl.when(pl.program_id(1) == 0)
    def _():
        n_ref[...] = _rms(x_ref[...], g_ref[...]).astype(BF16)

    n = n_ref[...]
    a = jnp.dot(n, wa_ref[...], preferred_element_type=F32)
    gt = jnp.dot(n, wg_ref[...], preferred_element_type=F32)
    o_ref[...] = a * _sigmoid(gt)


def _in_proj(x, g, w_in, qk_cols, attn_width, conv_width, tm, tn):
    m, d = x.shape
    scale = HEAD_DIM_QK ** -0.5
    x_spec = pl.BlockSpec((tm, d), lambda i, j: (i, 0))
    g_spec = pl.BlockSpec((1, d), lambda i, j: (0, 0))

    def w_spec(col0):
        off = col0 // tn
        return pl.BlockSpec((d, tn), lambda i, j: (0, j + off))

    o_spec = pl.BlockSpec((tm, tn), lambda i, j: (i, j))
    scratch = [pltpu.VMEM((tm, d), BF16)]
    nbytes = 2 * tm * d * 4 + tm * d * 2 + 4 * d * tn * 2 + 8 * tm * tn * 4

    q16 = pl.pallas_call(
        functools.partial(_q_proj_kernel, scale),
        grid=(m // tm, qk_cols // tn),
        in_specs=[x_spec, g_spec, w_spec(0)],
        out_specs=o_spec,
        out_shape=jax.ShapeDtypeStruct((m, qk_cols), BF16),
        scratch_shapes=scratch,
        compiler_params=_params(("parallel", "arbitrary"), nbytes),
        name="q_proj",
    )(x, g, w_in)

    def proj32_16(col0, cols, name):
        return pl.pallas_call(
            _kv_proj_kernel,
            grid=(m // tm, cols // tn),
            in_specs=[x_spec, g_spec, w_spec(col0)],
            out_specs=[o_spec, o_spec],
            out_shape=[jax.ShapeDtypeStruct((m, cols), F32), jax.ShapeDtypeStruct((m, cols), BF16)],
            scratch_shapes=scratch,
            compiler_params=_params(("parallel", "arbitrary"), nbytes),
            name=name,
        )(x, g, w_in)

    k32, k16 = proj32_16(qk_cols, qk_cols, "k_proj")
    v32, v16 = proj32_16(2 * qk_cols, attn_width, "v_proj")

    c0 = 2 * qk_cols + attn_width
    glu = pl.pallas_call(
        _glu_proj_kernel,
        grid=(m // tm, conv_width // tn),
        in_specs=[x_spec, g_spec, w_spec(c0), w_spec(c0 + conv_width)],
        out_specs=o_spec,
        out_shape=jax.ShapeDtypeStruct((m, conv_width), F32),
        scratch_shapes=scratch,
        compiler_params=_params(("parallel", "arbitrary"), nbytes),
        name="glu_proj",
    )(x, g, w_in, w_in)
    return q16, k32, k16, v32, v16, glu


def _split_maps(q):
    lane = lax.broadcasted_iota(jnp.int32, q.shape, 1)
    zero = jnp.zeros_like(q)
    return jnp.concatenate([jnp.where(lane < HEAD_DIM_QK, q, zero), jnp.where(lane >= HEAD_DIM_QK, q, zero)], axis=0)


def _subln(o, g, post_scale):
    ms = jnp.mean(o * o, axis=-1, keepdims=True)
    return o * lax.rsqrt(ms + RMS_EPS) * g * post_scale


def _prompt_attn_kernel(post_scale, tq, n_heads, sc_ref, q_ref, k_ref, v_ref, g_ref, o_ref):
    h = pl.program_id(1)
    qi = pl.program_id(2)
    slope = sc_ref[h]
    lam = sc_ref[n_heads]
    qq = _split_maps(q_ref[...])

    def block(ki, carry, masked):
        m_prev, l_prev, acc = carry
        start = pl.multiple_of(ki * tq, tq)
        k = k_ref[pl.ds(start, tq), :]
        v = v_ref[pl.ds(start, tq), :]
        s = lax.dot_general(qq, k, (((1,), (1,)), ((), ())), preferred_element_type=F32)
        col = lax.broadcasted_iota(jnp.int32, (1, tq), 1)
        rel = (ki - qi) * tq + col
        s = s + slope * rel.astype(F32)
        if masked:
            row = lax.broadcasted_iota(jnp.int32, (2 * tq, tq), 0)
            row = jnp.where(row >= tq, row - tq, row)
            colf = lax.broadcasted_iota(jnp.int32, (2 * tq, tq), 1)
            s = jnp.where(colf <= row, s, NEG_INF)
        m_new = jnp.maximum(m_prev, jnp.max(s, axis=-1, keepdims=True))
        alpha = jnp.exp(m_prev - m_new)
        p = jnp.exp(s - m_new)
        l_new = alpha * l_prev + jnp.sum(p, axis=-1, keepdims=True)
        acc = alpha * acc + jnp.dot(p.astype(BF16), v, preferred_element_type=F32)
        return m_new, l_new, acc

    init = (jnp.full((2 * tq, 1), NEG_INF, F32), jnp.zeros((2 * tq, 1), F32), jnp.zeros((2 * tq, HEAD_DIM_V), F32))
    carry = lax.fori_loop(0, qi, lambda ki, c: block(ki, c, False), init)
    _, l, acc = block(qi, carry, True)
    o = acc / l
    o = o[:tq] - lam * o[tq:]
    o_ref[...] = _subln(o, g_ref[...], post_scale).astype(o_ref.dtype)


def _prompt_attn(q16, k16, v16, g_subln, scal, batch, seq, n_heads, post_scale, tq):
    nq = seq // tq
    grid_spec = pltpu.PrefetchScalarGridSpec(
        num_scalar_prefetch=1,
        grid=(batch, n_heads, nq),
        in_specs=[
            pl.BlockSpec((tq, HEAD_DIM_V), lambda b, h, i, sc: (b * nq + i, h)),
            pl.BlockSpec((seq, HEAD_DIM_V), lambda b, h, i, sc: (b, h)),
            pl.BlockSpec((seq, HEAD_DIM_V), lambda b, h, i, sc: (b, h)),
            pl.BlockSpec((1, HEAD_DIM_V), lambda b, h, i, sc: (0, 0)),
        ],
        out_specs=pl.BlockSpec((tq, HEAD_DIM_V), lambda b, h, i, sc: (b * nq + i, h)),
    )
    nbytes = 4 * seq * HEAD_DIM_V * 2 + 4 * tq * HEAD_DIM_V * 2 + 8 * 2 * tq * tq * 4
    return pl.pallas_call(
        functools.partial(_prompt_attn_kernel, post_scale, tq, n_heads),
        grid_spec=grid_spec,
        out_shape=jax.ShapeDtypeStruct((batch * seq, n_heads * HEAD_DIM_V), BF16),
        compiler_params=_params(("parallel", "parallel", "arbitrary"), nbytes),
        name="prompt_attn",
    )(scal, q16, k16, v16, g_subln)


def _decode_attn_kernel(post_scale, n_heads, t_new, past_len, grp, pt_ref, sc_ref, q_ref, *refs):
    k_refs, v_refs = refs[:grp], refs[grp:2 * grp]
    kn_ref, vn_ref, g_ref, o_ref, bias_ref, slope_ref, m_ref, l_ref, acc_ref = refs[2 * grp:]
    p = pl.program_id(1)
    rows = 2 * n_heads * t_new
    rows_per_head = 2 * t_new
    cols = PAGE_SIZE * n_heads

    def head_geometry(ncols):
        row = lax.broadcasted_iota(jnp.int32, (rows, ncols), 0)
        col = lax.broadcasted_iota(jnp.int32, (rows, ncols), 1)
        rhead = row // rows_per_head
        slope = jnp.zeros((rows, ncols), F32)
        for hh in range(n_heads):
            slope = jnp.where(rhead == hh, sc_ref[hh], slope)
        return row, col, rhead == col % n_heads, col // n_heads, slope

    @pl.when(p == 0)
    def _():
        m_ref[...] = jnp.full(m_ref.shape, NEG_INF, F32)
        l_ref[...] = jnp.zeros(l_ref.shape, F32)
        acc_ref[...] = jnp.zeros(acc_ref.shape, F32)
        _, _, same_head, tok, slope = head_geometry(cols)
        bias_ref[...] = jnp.where(same_head, slope * tok.astype(F32), NEG_INF)
        slope_ref[...] = slope[:, 0:1]

    q = q_ref[...]

    def update(k2, v2, bias, shift):
        s = lax.dot_general(q, k2, (((1,), (1,)), ((), ())), preferred_element_type=F32) + bias
        m_prev = m_ref[...]
        m_new = jnp.maximum(m_prev, jnp.max(s, axis=-1, keepdims=True) + shift)
        alpha = jnp.exp(m_prev - m_new)
        pr = jnp.exp(s - (m_new - shift))
        l_ref[...] = alpha * l_ref[...] + jnp.sum(pr, axis=-1, keepdims=True)
        acc_ref[...] = alpha * acc_ref[...] + jnp.dot(pr.astype(BF16), v2, preferred_element_type=F32)
        m_ref[...] = m_new

    for g in range(grp):
        k2 = k_refs[g][...].reshape(cols, HEAD_DIM_V).astype(BF16)
        v2 = v_refs[g][...].reshape(cols, HEAD_DIM_V).astype(BF16)
        page_start = ((p * grp + g) * PAGE_SIZE - past_len).astype(F32)
        update(k2, v2, bias_ref[...], slope_ref[...] * page_start)

    @pl.when(p == pl.num_programs(1) - 1)
    def _():
        n_new = t_new * n_heads
        ncols = pl.cdiv(n_new, LANES) * LANES
        pad = jnp.zeros((ncols - n_new, HEAD_DIM_V), F32)
        kn = jnp.concatenate([kn_ref[...].reshape(n_new, HEAD_DIM_V), pad], axis=0).astype(BF16)
        vn = jnp.concatenate([vn_ref[...].reshape(n_new, HEAD_DIM_V), pad], axis=0).astype(BF16)
        row, col, same_head, tok, slope = head_geometry(ncols)
        visible = jnp.where(same_head, tok, ncols) <= row % t_new
        update(kn, vn, jnp.where(visible, slope * tok.astype(F32), NEG_INF), 0.0)
        o = acc_ref[...] / l_ref[...]
        lam = sc_ref[n_heads]
        gain = g_ref[...]
        outs = []
        for hh in range(n_heads):
            r0 = hh * rows_per_head
            oh = o[r0:r0 + t_new] - lam * o[r0 + t_new:r0 + 2 * t_new]
            outs.append(_subln(oh, gain, post_scale))
        o_ref[...] = jnp.concatenate(outs, axis=1).astype(o_ref.dtype)


def _decode_attn(q2, cache_k, cache_v, layer, k_new, v_new, g_subln, page_table, scal, post_scale):
    batch, n_pages = page_table.shape
    _, t_new, n_heads, _ = k_new.shape
    rows = 2 * n_heads * t_new
    width = n_heads * HEAD_DIM_V
    past_len = n_pages * PAGE_SIZE
    grp = _pick(n_pages, DECODE_PAGES_PER_STEP)
    page_block = (None, None, PAGE_SIZE, n_heads, HEAD_DIM_V)

    def page_spec(g):
        return pl.BlockSpec(page_block, lambda b, p, pt, sc: (layer, pt[b, p * grp + g], 0, 0, 0))

    new_spec = pl.BlockSpec((None, t_new, n_heads, HEAD_DIM_V), lambda b, p, pt, sc: (b, 0, 0, 0))
    grid_spec = pltpu.PrefetchScalarGridSpec(
        num_scalar_prefetch=2,
        grid=(batch, n_pages // grp),
        in_specs=[pl.BlockSpec((None, rows, HEAD_DIM_V), lambda b, p, pt, sc: (b, 0, 0))]
        + [page_spec(g) for g in range(grp)] * 2
        + [new_spec, new_spec, pl.BlockSpec((1, HEAD_DIM_V), lambda b, p, pt, sc: (0, 0))],
        out_specs=pl.BlockSpec((None, t_new, width), lambda b, p, pt, sc: (b, 0, 0)),
        scratch_shapes=[
            pltpu.VMEM((rows, PAGE_SIZE * n_heads), F32),
            pltpu.VMEM((rows, 1), F32),
            pltpu.VMEM((rows, 1), F32),
            pltpu.VMEM((rows, 1), F32),
            pltpu.VMEM((rows, HEAD_DIM_V), F32),
        ],
    )
    page_bytes = PAGE_SIZE * width * 4
    nbytes = 4 * grp * page_bytes + 6 * rows * PAGE_SIZE * n_heads * 4
    return pl.pallas_call(
        functools.partial(_decode_attn_kernel, post_scale, n_heads, t_new, past_len, grp),
        grid_spec=grid_spec,
        out_shape=jax.ShapeDtypeStruct((batch, t_new, width), BF16),
        compiler_params=_params(("parallel", "arbitrary"), nbytes),
        name="decode_attn",
    )(page_table, scal, q2, *([cache_k] * grp), *([cache_v] * grp), k_new, v_new, g_subln)


def _conv_kernel(seq, chunk, u_ref, prev_ref, w_ref, bdw_ref, gcn_ref, bcn_ref, o_ref, new_ref, ext_ref):
    hist = CONV_KERNEL - 1
    lead = CONV_PAD - hist
    ext_ref[0:lead, :] = jnp.zeros((lead, LANES), F32)
    ext_ref[lead:CONV_PAD, :] = prev_ref[0]
    ext_ref[CONV_PAD:CONV_PAD + seq, :] = u_ref[0]
    new_ref[0] = ext_ref[lead + seq:CONV_PAD + seq, :]
    bdw = bdw_ref[...]
    gcn = gcn_ref[...]
    bcn = bcn_ref[...]

    def body(c, _):
        r0 = pl.multiple_of(c * chunk, chunk)
        parts = [None] * CONV_PARTIALS
        for j in range(CONV_KERNEL):
            term = w_ref[j:j + 1, :] * ext_ref[pl.ds(r0 + (lead + j), chunk), :]
            slot = j % CONV_PARTIALS
            parts[slot] = term if parts[slot] is None else parts[slot] + term
        y = functools.reduce(lambda a, b: a + b, parts) + bdw
        mu = jnp.mean(y, axis=-1, keepdims=True)
        yc = y - mu
        var = jnp.mean(yc * yc, axis=-1, keepdims=True)
        yn = yc * lax.rsqrt(var + LN_EPS) * gcn + bcn
        o_ref[0, pl.ds(r0, chunk), :] = (yn * _sigmoid(yn)).astype(o_ref.dtype)
        return 0

    lax.fori_loop(0, seq // chunk, body, 0)


def _conv_module(u, prev, w_dw, b_dw, g_cn, b_cn):
    batch, seq, width = u.shape
    assert width // CONV_GROUPS == LANES
    chunk = _pick(seq, 64)
    hist = CONV_KERNEL - 1
    vec = pl.BlockSpec((1, LANES), lambda b, c: (0, c))
    return pl.pallas_call(
        functools.partial(_conv_kernel, seq, chunk),
        grid=(batch, width // LANES),
        in_specs=[
            pl.BlockSpec((1, seq, LANES), lambda b, c: (b, 0, c)),
            pl.BlockSpec((1, hist, LANES), lambda b, c: (b, 0, c)),
            pl.BlockSpec((CONV_KERNEL, LANES), lambda b, c: (0, c)),
            vec, vec, vec,
        ],
        out_specs=[
            pl.BlockSpec((1, seq, LANES), lambda b, c: (b, 0, c)),
            pl.BlockSpec((1, hist, LANES), lambda b, c: (b, 0, c)),
        ],
        out_shape=[jax.ShapeDtypeStruct((batch, seq, width), BF16), jax.ShapeDtypeStruct((batch, hist, width), F32)],
        scratch_shapes=[pltpu.VMEM((CONV_PAD + seq, LANES), F32)],
        compiler_params=_params(("parallel", "parallel"), 8 * (seq + CONV_PAD) * LANES * 4),
        name="conv_module",
    )(u, prev, w_dw, b_dw, g_cn, b_cn)


def _out_proj_kernel(ka, a_ref, b_ref, w_ref, r_ref, o_ref):
    acc = jnp.dot(a_ref[...], w_ref[0:ka, :], preferred_element_type=F32)
    acc = acc + jnp.dot(b_ref[...], w_ref[ka:, :], preferred_element_type=F32)
    o_ref[...] = r_ref[...] + acc


def _out_proj(a, b, w, res, tm, tn):
    m, ka = a.shape
    kb = b.shape[1]
    d = w.shape[1]
    nbytes = 2 * tm * (ka + kb) * 2 + 2 * (ka + kb) * tn * 2 + 5 * tm * tn * 4
    return pl.pallas_call(
        functools.partial(_out_proj_kernel, ka),
        grid=(m // tm, d // tn),
        in_specs=[
            pl.BlockSpec((tm, ka), lambda i, j: (i, 0)),
            pl.BlockSpec((tm, kb), lambda i, j: (i, 0)),
            pl.BlockSpec((ka + kb, tn), lambda i, j: (0, j)),
            pl.BlockSpec((tm, tn), lambda i, j: (i, j)),
        ],
        out_specs=pl.BlockSpec((tm, tn), lambda i, j: (i, j)),
        out_shape=jax.ShapeDtypeStruct((m, d), F32),
        compiler_params=_params(("parallel", "arbitrary"), nbytes),
        name="out_proj",
    )(a, b, w, res)


def _ple_kernel(final_norm, h_ref, g_ref, wg_ref, p_ref, wp_ref, gf_ref, o_ref):
    h = h_ref[...]
    n = _rms(h, g_ref[...]).astype(BF16)
    gate = _sigmoid(jnp.dot(n, wg_ref[...], preferred_element_type=F32))
    proj = jnp.dot(p_ref[...].astype(BF16), wp_ref[...], preferred_element_type=F32)
    h = h + proj * gate
    o_ref[...] = _rms(h, gf_ref[...]) if final_norm else h


def _ple(h, g, wg, p, wp, g_final, final_norm, tm):
    m, d = h.shape
    dp = p.shape[1]
    nbytes = 4 * tm * d * 4 + 2 * d * d * 2 + 2 * dp * d * 2 + 2 * tm * dp * 4 + 4 * tm * d * 4
    vec = pl.BlockSpec((1, d), lambda i: (0, 0))
    return pl.pallas_call(
        functools.partial(_ple_kernel, final_norm),
        grid=(m // tm,),
        in_specs=[
            pl.BlockSpec((tm, d), lambda i: (i, 0)),
            vec,
            pl.BlockSpec((d, d), lambda i: (0, 0)),
            pl.BlockSpec((tm, dp), lambda i: (i, 0)),
            pl.BlockSpec((dp, d), lambda i: (0, 0)),
            vec,
        ],
        out_specs=pl.BlockSpec((tm, d), lambda i: (i, 0)),
        out_shape=jax.ShapeDtypeStruct((m, d), F32),
        compiler_params=_params(("parallel",), nbytes),
        name="ple",
    )(h, g, wg, p, wp, g_final)


def _lambda_init(layer_idx):
    return 0.8 - 0.6 * math.exp(-0.3 * layer_idx)


def kernel(x_prompt, x_sample, cache_k, cache_v, state_conv, page_table, p_prompt, p_sample, g_ffn1, w_ffn1_gate, w_ffn1_up, w_ffn1_down, g_mix, w_in, lambda_q1, lambda_k1, lambda_q2, lambda_k2, g_subln, w_dw, b_dw, g_conv_norm, b_conv_norm, w_out, g_ffn2, w_ffn2_gate, w_ffn2_up, w_ffn2_down, g_ple, w_ple_gate, w_ple_proj, g_final):
    depth = w_in.shape[0]
    batch, seq, d_model = x_prompt.shape
    dec_batch, dec_seq, _ = x_sample.shape
    n_heads = cache_k.shape[3]
    assert cache_k.shape[4] == 2 * HEAD_DIM_QK and cache_v.shape[4] == HEAD_DIM_V and cache_k.shape[2] == PAGE_SIZE
    qk_cols = n_heads * 2 * HEAD_DIM_QK
    attn_width = n_heads * HEAD_DIM_V
    conv_width = d_model - attn_width
    h_p = x_prompt.reshape(batch * seq, d_model)
    h_s = x_sample.reshape(dec_batch * dec_seq, d_model)
    tm_p = _pick(batch * seq, TOKEN_TILE)
    tm_s = dec_batch * dec_seq
    tq = _pick(seq, 256)
    slopes = jnp.asarray([2.0 ** (-8.0 * (h + 1) / n_heads) for h in range(n_heads)], F32)
    row2 = lambda a: a.reshape(1, -1)
    gf = row2(g_final)

    outs = {name: [] for name in ("kp", "vp", "cp", "ks", "vs", "cs")}
    for l in range(depth):
        last = l == depth - 1
        lam_init = _lambda_init(l)
        lam = (jnp.exp(jnp.sum(lambda_q1[l] * lambda_k1[l])) - jnp.exp(jnp.sum(lambda_q2[l] * lambda_k2[l])) + lam_init)
        scal = jnp.concatenate([slopes, lam.reshape(1).astype(F32)])
        post_scale = 1.0 - lam_init
        w1g, w1u, w1d = w_ffn1_gate[l].astype(BF16), w_ffn1_up[l].astype(BF16), w_ffn1_down[l].astype(BF16)
        w2g, w2u, w2d = w_ffn2_gate[l].astype(BF16), w_ffn2_up[l].astype(BF16), w_ffn2_down[l].astype(BF16)
        w_in16 = w_in[l].astype(BF16)
        w_out16 = w_out[l].astype(BF16)
        w_pg16 = w_ple_gate[l].astype(BF16)
        w_pp16 = w_ple_proj[l].astype(BF16)
        tf = _pick(w1g.shape[1], 512)
        tn_proj = math.gcd(math.gcd(qk_cols, attn_width), math.gcd(conv_width, 1024))

        def ffn(h, g, wg, wu, wd, tm):
            hid = _ffn_up(h, row2(g), wg, wu, tm, tf)
            return _ffn_down(hid, wd, h, tm, _pick(d_model, 512))

        def pre_mix(h, tm):
            h = ffn(h, g_ffn1[l], w1g, w1u, w1d, tm)
            return (h,) + _in_proj(h, row2(g_mix[l]), w_in16, qk_cols, attn_width, conv_width, tm, tn_proj)

        def post_mix(h, o_attn, o_conv, p_l, tm):
            h = _out_proj(o_attn, o_conv, w_out16, h, tm, _pick(d_model, 1024))
            h = ffn(h, g_ffn2[l], w2g, w2u, w2d, tm)
            return _ple(h, row2(g_ple[l]), w_pg16, p_l, w_pp16, gf, last, min(tm, 512))

        conv_args = (w_dw[l], row2(b_dw[l]), row2(g_conv_norm[l]), row2(b_conv_norm[l]))

        h_p, q16, k32, k16, v32, v16, glu = pre_mix(h_p, tm_p)
        o_attn = _prompt_attn(q16, k16, v16, row2(g_subln[l]), scal, batch, seq, n_heads, post_scale, tq)
        zero_hist = jnp.zeros((batch, CONV_KERNEL - 1, conv_width), F32)
        o_conv, conv_new = _conv_module(glu.reshape(batch, seq, conv_width), zero_hist, *conv_args)
        h_p = post_mix(h_p, o_attn, o_conv.reshape(batch * seq, conv_width), p_prompt[l].reshape(batch * seq, -1), tm_p)
        outs["kp"].append(k32.reshape(batch, seq, n_heads, 2 * HEAD_DIM_QK))
        outs["vp"].append(v32.reshape(batch, seq, n_heads, HEAD_DIM_V))
        outs["cp"].append(conv_new)

        h_s, q16, k32, k16, v32, v16, glu = pre_mix(h_s, tm_s)
        k_new = k32.reshape(dec_batch, dec_seq, n_heads, 2 * HEAD_DIM_QK)
        v_new = v32.reshape(dec_batch, dec_seq, n_heads, HEAD_DIM_V)
        q5 = q16.reshape(dec_batch, dec_seq, n_heads, 2, HEAD_DIM_QK)
        q2 = jnp.einsum("bthmd,mn->bhmtnd", q5, jnp.eye(2, dtype=BF16))
        q2 = q2.reshape(dec_batch, 2 * n_heads * dec_seq, 2 * HEAD_DIM_QK)
        o_attn = _decode_attn(q2, cache_k, cache_v, l, k_new, v_new, row2(g_subln[l]), page_table, scal, post_scale)
        o_conv, conv_new = _conv_module(glu.reshape(dec_batch, dec_seq, conv_width), state_conv[l], *conv_args)
        h_s = post_mix(h_s, o_attn.reshape(dec_batch * dec_seq, attn_width),
                       o_conv.reshape(dec_batch * dec_seq, conv_width), p_sample[l].reshape(dec_batch * dec_seq, -1), tm_s)
        outs["ks"].append(k_new.reshape(dec_batch, dec_seq, n_heads, 2 * HEAD_DIM_QK))
        outs["vs"].append(v_new.reshape(dec_batch, dec_seq, n_heads, HEAD_DIM_V))
        outs["cs"].append(conv_new)

    y_prompt = h_p.reshape(batch, seq, d_model)
    y_sample = h_s.reshape(dec_batch, dec_seq, d_model)
    return (y_prompt, y_sample, jnp.stack(outs["kp"]), jnp.stack(outs["vp"]), jnp.stack(outs["cp"]),
            jnp.stack(outs["ks"]), jnp.stack(outs["vs"]), jnp.stack(outs["cs"]))
```

```python
import functools
import math

import jax
import jax.numpy as jnp
from jax import lax
from jax.experimental import pallas as pl
from jax.experimental.pallas import tpu as pltpu

F32 = jnp.float32
BF16 = jnp.bfloat16

HEAD_DIM_QK = 64
HEAD_DIM_V = 2 * HEAD_DIM_QK
CONV_GROUPS = 8
CONV_KERNEL = 31
PAGE_SIZE = 128
RMS_EPS = 1e-6
LN_EPS = 1e-5
NEG_INF = -1e30

V7X_VMEM_BYTES = 64 * 1024 * 1024
LANES = 128
SUBLANES = 8
CONV_PAD = 32
CONV_PARTIALS = 4
DECODE_PAGES_PER_STEP = 8
CONV_CHUNKS_PER_ITER = 4
ATTN_Q_TILE = 512
TOKEN_TILE = 1024


def _vmem_limit(nbytes):
    return int(min(V7X_VMEM_BYTES - 4 * 1024 * 1024, nbytes + 16 * 1024 * 1024))


def _params(sem, nbytes):
    return pltpu.CompilerParams(dimension_semantics=sem, vmem_limit_bytes=_vmem_limit(nbytes))


def _rms(x, g):
    ms = jnp.mean(x * x, axis=-1, keepdims=True)
    return x * lax.rsqrt(ms + RMS_EPS) * g


def _sigmoid(x):
    return 1.0 / (1.0 + jnp.exp(-x))


def _pick(n, pref):
    t = min(n, pref)
    assert n % t == 0, (n, t)
    return t


def _ffn_up_kernel(x_ref, g_ref, wg_ref, wu_ref, o_ref, n_ref):
    @pl.when(pl.program_id(1) == 0)
    def _():
        n_ref[...] = _rms(x_ref[...], g_ref[...]).astype(BF16)

    n = n_ref[...]
    a = jnp.dot(n, wg_ref[...], preferred_element_type=F32)
    u = jnp.dot(n, wu_ref[...], preferred_element_type=F32)
    o_ref[...] = (a * _sigmoid(a) * u).astype(BF16)


def _ffn_up(x, g, wg, wu, tm, tf):
    m, d = x.shape
    f = wg.shape[1]
    nbytes = 2 * tm * d * 4 + tm * d * 2 + 2 * 2 * d * tf * 2 + 2 * tm * tf * 2 + 3 * tm * tf * 4
    return pl.pallas_call(
        _ffn_up_kernel,
        grid=(m // tm, f // tf),
        in_specs=[
            pl.BlockSpec((tm, d), lambda i, j: (i, 0)),
            pl.BlockSpec((1, d), lambda i, j: (0, 0)),
            pl.BlockSpec((d, tf), lambda i, j: (0, j)),
            pl.BlockSpec((d, tf), lambda i, j: (0, j)),
        ],
        out_specs=pl.BlockSpec((tm, tf), lambda i, j: (i, j)),
        out_shape=jax.ShapeDtypeStruct((m, f), BF16),
        scratch_shapes=[pltpu.VMEM((tm, d), BF16)],
        compiler_params=_params(("parallel", "arbitrary"), nbytes),
        name="ffn_up",
    )(x, g, wg, wu)


def _ffn_down_kernel(h_ref, w_ref, r_ref, o_ref):
    o_ref[...] = r_ref[...] + 0.5 * jnp.dot(h_ref[...], w_ref[...], preferred_element_type=F32)


def _ffn_down(hid, wd, res, tm, tn):
    m, f = hid.shape
    d = wd.shape[1]
    nbytes = 2 * tm * f * 2 + 2 * f * tn * 2 + 4 * tm * tn * 4 + tm * tn * 4
    return pl.pallas_call(
        _ffn_down_kernel,
        grid=(m // tm, d // tn),
        in_specs=[
            pl.BlockSpec((tm, f), lambda i, j: (i, 0)),
            pl.BlockSpec((f, tn), lambda i, j: (0, j)),
            pl.BlockSpec((tm, tn), lambda i, j: (i, j)),
        ],
        out_specs=pl.BlockSpec((tm, tn), lambda i, j: (i, j)),
        out_shape=jax.ShapeDtypeStruct((m, d), F32),
        compiler_params=_params(("parallel", "arbitrary"), nbytes),
        name="ffn_down",
    )(hid, wd, res)


def _q_proj_kernel(scale, x_ref, g_ref, w_ref, o_ref, n_ref):
    @pl.when(pl.program_id(1) == 0)
    def _():
        n_ref[...] = _rms(x_ref[...], g_ref[...]).astype(BF16)

    u = jnp.dot(n_ref[...], w_ref[...], preferred_element_type=F32)
    o_ref[...] = (u * scale).astype(BF16)


def _kv_proj_kernel(x_ref, g_ref, w_ref, o32_ref, o16_ref, n_ref):
    @pl.when(pl.program_id(1) == 0)
    def _():
        n_ref[...] = _rms(x_ref[...], g_ref[...]).astype(BF16)

    u = jnp.dot(n_ref[...], w_ref[...], preferred_element_type=F32)
    o32_ref[...] = u
    o16_ref[...] = u.astype(BF16)


def _glu_proj_kernel(x_ref, g_ref, wa_ref, wg_ref, o_ref, n_ref):
    @pl.when(pl.program_id(1) == 0)
    def _():
        n_ref[...] = _rms(x_ref[...], g_ref[...]).astype(BF16)

    n = n_ref[...]
    a = jnp.dot(n, wa_ref[...], preferred_element_type=F32)
    gt = jnp.dot(n, wg_ref[...], preferred_element_type=F32)
    o_ref[...] = a * _sigmoid(gt)


def _in_proj(x, g, w_in, qk_cols, attn_width, conv_width, tm, tn):
    m, d = x.shape
    scale = HEAD_DIM_QK ** -0.5
    x_spec = pl.BlockSpec((tm, d), lambda i, j: (i, 0))
    g_spec = pl.BlockSpec((1, d), lambda i, j: (0, 0))

    def w_spec(col0):
        off = col0 // tn
        return pl.BlockSpec((d, tn), lambda i, j: (0, j + off))

    o_spec = pl.BlockSpec((tm, tn), lambda i, j: (i, j))
    scratch = [pltpu.VMEM((tm, d), BF16)]
    nbytes = 2 * tm * d * 4 + tm * d * 2 + 4 * d * tn * 2 + 8 * tm * tn * 4

    q16 = pl.pallas_call(
        functools.partial(_q_proj_kernel, scale),
        grid=(m // tm, qk_cols // tn),
        in_specs=[x_spec, g_spec, w_spec(0)],
        out_specs=o_spec,
        out_shape=jax.ShapeDtypeStruct((m, qk_cols), BF16),
        scratch_shapes=scratch,
        compiler_params=_params(("parallel", "arbitrary"), nbytes),
        name="q_proj",
    )(x, g, w_in)

    def proj32_16(col0, cols, name):
        return pl.pallas_call(
            _kv_proj_kernel,
            grid=(m // tm, cols // tn),
            in_specs=[x_spec, g_spec, w_spec(col0)],
            out_specs=[o_spec, o_spec],
            out_shape=[jax.ShapeDtypeStruct((m, cols), F32), jax.ShapeDtypeStruct((m, cols), BF16)],
            scratch_shapes=scratch,
            compiler_params=_params(("parallel", "arbitrary"), nbytes),
            name=name,
        )(x, g, w_in)

    k32, k16 = proj32_16(qk_cols, qk_cols, "k_proj")
    v32, v16 = proj32_16(2 * qk_cols, attn_width, "v_proj")

    c0 = 2 * qk_cols + attn_width
    glu = pl.pallas_call(
        _glu_proj_kernel,
        grid=(m // tm, conv_width // tn),
        in_specs=[x_spec, g_spec, w_spec(c0), w_spec(c0 + conv_width)],
        out_specs=o_spec,
        out_shape=jax.ShapeDtypeStruct((m, conv_width), F32),
        scratch_shapes=scratch,
        compiler_params=_params(("parallel", "arbitrary"), nbytes),
        name="glu_proj",
    )(x, g, w_in, w_in)
    return q16, k32, k16, v32, v16, glu


def _split_maps(q):
    lane = lax.broadcasted_iota(jnp.int32, q.shape, 1)
    zero = jnp.zeros_like(q)
    return jnp.concatenate([jnp.where(lane < HEAD_DIM_QK, q, zero), jnp.where(lane >= HEAD_DIM_QK, q, zero)], axis=0)


def _subln(o, g, post_scale):
    ms = jnp.mean(o * o, axis=-1, keepdims=True)
    return o * lax.rsqrt(ms + RMS_EPS) * g * post_scale


def _prompt_attn_kernel(post_scale, tq, n_heads, sc_ref, q_ref, k_ref, v_ref, g_ref, o_ref):
    h = pl.program_id(1)
    qi = pl.program_id(2)
    slope = sc_ref[h]
    lam = sc_ref[n_heads]
    qq = _split_maps(q_ref[...])

    def block(ki, carry, masked):
        m_prev, l_prev, acc = carry
        start = pl.multiple_of(ki * tq, tq)
        k = k_ref[pl.ds(start, tq), :]
        v = v_ref[pl.ds(start, tq), :]
        s = lax.dot_general(qq, k, (((1,), (1,)), ((), ())), preferred_element_type=F32)
        col = lax.broadcasted_iota(jnp.int32, (1, tq), 1)
        rel = (ki - qi) * tq + col
        s = s + slope * rel.astype(F32)
        if masked:
            row = lax.broadcasted_iota(jnp.int32, (2 * tq, tq), 0)
            row = jnp.where(row >= tq, row - tq, row)
            colf = lax.broadcasted_iota(jnp.int32, (2 * tq, tq), 1)
            s = jnp.where(colf <= row, s, NEG_INF)
        m_new = jnp.maximum(m_prev, jnp.max(s, axis=-1, keepdims=True))
        alpha = jnp.exp(m_prev - m_new)
        p = jnp.exp(s - m_new)
        l_new = alpha * l_prev + jnp.sum(p, axis=-1, keepdims=True)
        acc = alpha * acc + jnp.dot(p.astype(BF16), v, preferred_element_type=F32)
        return m_new, l_new, acc

    init = (jnp.full((2 * tq, 1), NEG_INF, F32), jnp.zeros((2 * tq, 1), F32), jnp.zeros((2 * tq, HEAD_DIM_V), F32))
    carry = lax.fori_loop(0, qi, lambda ki, c: block(ki, c, False), init)
    _, l, acc = block(qi, carry, True)
    o = acc / l
    o = o[:tq] - lam * o[tq:]
    o_ref[...] = _subln(o, g_ref[...], post_scale).astype(o_ref.dtype)


def _prompt_attn(q16, k16, v16, g_subln, scal, batch, seq, n_heads, post_scale, tq):
    nq = seq // tq
    grid_spec = pltpu.PrefetchScalarGridSpec(
        num_scalar_prefetch=1,
        grid=(batch, n_heads, nq),
        in_specs=[
            pl.BlockSpec((tq, HEAD_DIM_V), lambda b, h, i, sc: (b * nq + i, h)),
            pl.BlockSpec((seq, HEAD_DIM_V), lambda b, h, i, sc: (b, h)),
            pl.BlockSpec((seq, HEAD_DIM_V), lambda b, h, i, sc: (b, h)),
            pl.BlockSpec((1, HEAD_DIM_V), lambda b, h, i, sc: (0, 0)),
        ],
        out_specs=pl.BlockSpec((tq, HEAD_DIM_V), lambda b, h, i, sc: (b * nq + i, h)),
    )
    nbytes = 4 * seq * HEAD_DIM_V * 2 + 4 * tq * HEAD_DIM_V * 2 + 8 * 2 * tq * tq * 4
    return pl.pallas_call(
        functools.partial(_prompt_attn_kernel, post_scale, tq, n_heads),
        grid_spec=grid_spec,
        out_shape=jax.ShapeDtypeStruct((batch * seq, n_heads * HEAD_DIM_V), BF16),
        compiler_params=_params(("parallel", "parallel", "arbitrary"), nbytes),
        name="prompt_attn",
    )(scal, q16, k16, v16, g_subln)


def _decode_attn_kernel(post_scale, n_heads, t_new, past_len, grp, pt_ref, sc_ref, q_ref, *refs):
    k_refs, v_refs = refs[:grp], refs[grp:2 * grp]
    kn_ref, vn_ref, g_ref, o_ref, bias_ref, slope_ref, m_ref, l_ref, acc_ref = refs[2 * grp:]
    p = pl.program_id(1)
    rows = 2 * n_heads * t_new
    rows_per_head = 2 * t_new
    cols = PAGE_SIZE * n_heads

    def head_geometry(ncols):
        row = lax.broadcasted_iota(jnp.int32, (rows, ncols), 0)
        col = lax.broadcasted_iota(jnp.int32, (rows, ncols), 1)
        rhead = row // rows_per_head
        slope = jnp.zeros((rows, ncols), F32)
        for hh in range(n_heads):
            slope = jnp.where(rhead == hh, sc_ref[hh], slope)
        return row, col, rhead == col % n_heads, col // n_heads, slope

    @pl.when(p == 0)
    def _():
        m_ref[...] = jnp.full(m_ref.shape, NEG_INF, F32)
        l_ref[...] = jnp.zeros(l_ref.shape, F32)
        acc_ref[...] = jnp.zeros(acc_ref.shape, F32)
        _, _, same_head, tok, slope = head_geometry(cols)
        bias_ref[...] = jnp.where(same_head, slope * tok.astype(F32), NEG_INF)
        slope_ref[...] = slope[:, 0:1]

    q = q_ref[...]

    def update(blocks):
        scores = [lax.dot_general(q, k2, (((1,), (1,)), ((), ())), preferred_element_type=F32) + bias
                  for k2, _, bias, _ in blocks]
        m_prev = m_ref[...]
        m_new = m_prev
        for s, (_, _, _, shift) in zip(scores, blocks):
            m_new = jnp.maximum(m_new, jnp.max(s, axis=-1, keepdims=True) + shift)
        alpha = jnp.exp(m_prev - m_new)
        l_new = alpha * l_ref[...]
        acc = alpha * acc_ref[...]
        for s, (_, v2, _, shift) in zip(scores, blocks):
            pr = jnp.exp(s - (m_new - shift))
            l_new = l_new + jnp.sum(pr, axis=-1, keepdims=True)
            acc = acc + jnp.dot(pr.astype(BF16), v2, preferred_element_type=F32)
        l_ref[...] = l_new
        acc_ref[...] = acc
        m_ref[...] = m_new

    blocks = []
    for g in range(grp):
        k2 = k_refs[g][...].reshape(cols, HEAD_DIM_V).astype(BF16)
        v2 = v_refs[g][...].reshape(cols, HEAD_DIM_V).astype(BF16)
        page_start = ((p * grp + g) * PAGE_SIZE - past_len).astype(F32)
        blocks.append((k2, v2, bias_ref[...], slope_ref[...] * page_start))
    update(blocks)

    @pl.when(p == pl.num_programs(1) - 1)
    def _():
        n_new = t_new * n_heads
        ncols = pl.cdiv(n_new, LANES) * LANES
        pad = jnp.zeros((ncols - n_new, HEAD_DIM_V), F32)
        kn = jnp.concatenate([kn_ref[...].reshape(n_new, HEAD_DIM_V), pad], axis=0).astype(BF16)
        vn = jnp.concatenate([vn_ref[...].reshape(n_new, HEAD_DIM_V), pad], axis=0).astype(BF16)
        row, col, same_head, tok, slope = head_geometry(ncols)
        visible = jnp.where(same_head, tok, ncols) <= row % t_new
        update([(kn, vn, jnp.where(visible, slope * tok.astype(F32), NEG_INF), 0.0)])
        o = acc_ref[...] / l_ref[...]
        lam = sc_ref[n_heads]
        gain = g_ref[...]
        outs = []
        for hh in range(n_heads):
            r0 = hh * rows_per_head
            oh = o[r0:r0 + t_new] - lam * o[r0 + t_new:r0 + 2 * t_new]
            outs.append(_subln(oh, gain, post_scale))
        o_ref[...] = jnp.concatenate(outs, axis=1).astype(o_ref.dtype)


def _decode_attn(q2, cache_k, cache_v, layer, k_new, v_new, g_subln, page_table, scal, post_scale):
    batch, n_pages = page_table.shape
    _, t_new, n_heads, _ = k_new.shape
    rows = 2 * n_heads * t_new
    width = n_heads * HEAD_DIM_V
    past_len = n_pages * PAGE_SIZE
    grp = _pick(n_pages, DECODE_PAGES_PER_STEP)
    page_block = (None, None, PAGE_SIZE, n_heads, HEAD_DIM_V)

    def page_spec(g):
        return pl.BlockSpec(page_block, lambda b, p, pt, sc: (layer, pt[b, p * grp + g], 0, 0, 0))

    new_spec = pl.BlockSpec((None, t_new, n_heads, HEAD_DIM_V), lambda b, p, pt, sc: (b, 0, 0, 0))
    grid_spec = pltpu.PrefetchScalarGridSpec(
        num_scalar_prefetch=2,
        grid=(batch, n_pages // grp),
        in_specs=[pl.BlockSpec((None, rows, HEAD_DIM_V), lambda b, p, pt, sc: (b, 0, 0))]
        + [page_spec(g) for g in range(grp)] * 2
        + [new_spec, new_spec, pl.BlockSpec((1, HEAD_DIM_V), lambda b, p, pt, sc: (0, 0))],
        out_specs=pl.BlockSpec((None, t_new, width), lambda b, p, pt, sc: (b, 0, 0)),
        scratch_shapes=[
            pltpu.VMEM((rows, PAGE_SIZE * n_heads), F32),
            pltpu.VMEM((rows, 1), F32),
            pltpu.VMEM((rows, 1), F32),
            pltpu.VMEM((rows, 1), F32),
            pltpu.VMEM((rows, HEAD_DIM_V), F32),
        ],
    )
    page_bytes = PAGE_SIZE * width * 4
    nbytes = 4 * grp * page_bytes + 6 * rows * PAGE_SIZE * n_heads * 4
    return pl.pallas_call(
        functools.partial(_decode_attn_kernel, post_scale, n_heads, t_new, past_len, grp),
        grid_spec=grid_spec,
        out_shape=jax.ShapeDtypeStruct((batch, t_new, width), BF16),
        compiler_params=_params(("parallel", "arbitrary"), nbytes),
        name="decode_attn",
    )(page_table, scal, q2, *([cache_k] * grp), *([cache_v] * grp), k_new, v_new, g_subln)


def _conv_kernel(seq, chunk, u_ref, prev_ref, w_ref, bdw_ref, gcn_ref, bcn_ref, o_ref, new_ref, ext_ref):
    hist = CONV_KERNEL - 1
    lead = CONV_PAD - hist
    ext_ref[0:lead, :] = jnp.zeros((lead, LANES), F32)
    ext_ref[lead:CONV_PAD, :] = prev_ref[0]
    ext_ref[CONV_PAD:CONV_PAD + seq, :] = u_ref[0]
    new_ref[0] = ext_ref[lead + seq:CONV_PAD + seq, :]
    bdw = bdw_ref[...]
    gcn = gcn_ref[...]
    bcn = bcn_ref[...]

    def body(c, _):
        r0 = pl.multiple_of(c * chunk, chunk)
        parts = [None] * CONV_PARTIALS
        for j in range(CONV_KERNEL):
            term = w_ref[j:j + 1, :] * ext_ref[pl.ds(r0 + (lead + j), chunk), :]
            slot = j % CONV_PARTIALS
            parts[slot] = term if parts[slot] is None else parts[slot] + term
        y = functools.reduce(lambda a, b: a + b, parts) + bdw
        mu = jnp.mean(y, axis=-1, keepdims=True)
        yc = y - mu
        var = jnp.mean(yc * yc, axis=-1, keepdims=True)
        yn = yc * lax.rsqrt(var + LN_EPS) * gcn + bcn
        o_ref[0, pl.ds(r0, chunk), :] = (yn * _sigmoid(yn)).astype(o_ref.dtype)
        return 0

    n_chunks = seq // chunk
    lax.fori_loop(0, n_chunks, body, 0, unroll=math.gcd(n_chunks, CONV_CHUNKS_PER_ITER))


def _conv_module(u, prev, w_dw, b_dw, g_cn, b_cn):
    batch, seq, width = u.shape
    assert width // CONV_GROUPS == LANES
    chunk = _pick(seq, 64)
    hist = CONV_KERNEL - 1
    vec = pl.BlockSpec((1, LANES), lambda b, c: (0, c))
    return pl.pallas_call(
        functools.partial(_conv_kernel, seq, chunk),
        grid=(batch, width // LANES),
        in_specs=[
            pl.BlockSpec((1, seq, LANES), lambda b, c: (b, 0, c)),
            pl.BlockSpec((1, hist, LANES), lambda b, c: (b, 0, c)),
            pl.BlockSpec((CONV_KERNEL, LANES), lambda b, c: (0, c)),
            vec, vec, vec,
        ],
        out_specs=[
            pl.BlockSpec((1, seq, LANES), lambda b, c: (b, 0, c)),
            pl.BlockSpec((1, hist, LANES), lambda b, c: (b, 0, c)),
        ],
        out_shape=[jax.ShapeDtypeStruct((batch, seq, width), BF16), jax.ShapeDtypeStruct((batch, hist, width), F32)],
        scratch_shapes=[pltpu.VMEM((CONV_PAD + seq, LANES), F32)],
        compiler_params=_params(("parallel", "parallel"), 8 * (seq + CONV_PAD) * LANES * 4),
        name="conv_module",
    )(u, prev, w_dw, b_dw, g_cn, b_cn)


def _out_proj_kernel(ka, a_ref, b_ref, w_ref, r_ref, o_ref):
    acc = jnp.dot(a_ref[...], w_ref[0:ka, :], preferred_element_type=F32)
    acc = acc + jnp.dot(b_ref[...], w_ref[ka:, :], preferred_element_type=F32)
    o_ref[...] = r_ref[...] + acc


def _out_proj(a, b, w, res, tm, tn):
    m, ka = a.shape
    kb = b.shape[1]
    d = w.shape[1]
    nbytes = 2 * tm * (ka + kb) * 2 + 2 * (ka + kb) * tn * 2 + 5 * tm * tn * 4
    return pl.pallas_call(
        functools.partial(_out_proj_kernel, ka),
        grid=(m // tm, d // tn),
        in_specs=[
            pl.BlockSpec((tm, ka), lambda i, j: (i, 0)),
            pl.BlockSpec((tm, kb), lambda i, j: (i, 0)),
            pl.BlockSpec((ka + kb, tn), lambda i, j: (0, j)),
            pl.BlockSpec((tm, tn), lambda i, j: (i, j)),
        ],
        out_specs=pl.BlockSpec((tm, tn), lambda i, j: (i, j)),
        out_shape=jax.ShapeDtypeStruct((m, d), F32),
        compiler_params=_params(("parallel", "arbitrary"), nbytes),
        name="out_proj",
    )(a, b, w, res)


def _ple_kernel(final_norm, h_ref, g_ref, wg_ref, p_ref, wp_ref, gf_ref, o_ref):
    h = h_ref[...]
    n = _rms(h, g_ref[...]).astype(BF16)
    gate = _sigmoid(jnp.dot(n, wg_ref[...], preferred_element_type=F32))
    proj = jnp.dot(p_ref[...].astype(BF16), wp_ref[...], preferred_element_type=F32)
    h = h + proj * gate
    o_ref[...] = _rms(h, gf_ref[...]) if final_norm else h


def _ple(h, g, wg, p, wp, g_final, final_norm, tm):
    m, d = h.shape
    dp = p.shape[1]
    nbytes = 4 * tm * d * 4 + 2 * d * d * 2 + 2 * dp * d * 2 + 2 * tm * dp * 4 + 4 * tm * d * 4
    vec = pl.BlockSpec((1, d), lambda i: (0, 0))
    return pl.pallas_call(
        functools.partial(_ple_kernel, final_norm),
        grid=(m // tm,),
        in_specs=[
            pl.BlockSpec((tm, d), lambda i: (i, 0)),
            vec,
            pl.BlockSpec((d, d), lambda i: (0, 0)),
            pl.BlockSpec((tm, dp), lambda i: (i, 0)),
            pl.BlockSpec((dp, d), lambda i: (0, 0)),
            vec,
        ],
        out_specs=pl.BlockSpec((tm, d), lambda i: (i, 0)),
        out_shape=jax.ShapeDtypeStruct((m, d), F32),
        compiler_params=_params(("parallel",), nbytes),
        name="ple",
    )(h, g, wg, p, wp, g_final)


def _lambda_init(layer_idx):
    return 0.8 - 0.6 * math.exp(-0.3 * layer_idx)


def kernel(x_prompt, x_sample, cache_k, cache_v, state_conv, page_table, p_prompt, p_sample, g_ffn1, w_ffn1_gate, w_ffn1_up, w_ffn1_down, g_mix, w_in, lambda_q1, lambda_k1, lambda_q2, lambda_k2, g_subln, w_dw, b_dw, g_conv_norm, b_conv_norm, w_out, g_ffn2, w_ffn2_gate, w_ffn2_up, w_ffn2_down, g_ple, w_ple_gate, w_ple_proj, g_final):
    depth = w_in.shape[0]
    batch, seq, d_model = x_prompt.shape
    dec_batch, dec_seq, _ = x_sample.shape
    n_heads = cache_k.shape[3]
    assert cache_k.shape[4] == 2 * HEAD_DIM_QK and cache_v.shape[4] == HEAD_DIM_V and cache_k.shape[2] == PAGE_SIZE
    qk_cols = n_heads * 2 * HEAD_DIM_QK
    attn_width = n_heads * HEAD_DIM_V
    conv_width = d_model - attn_width
    h_p = x_prompt.reshape(batch * seq, d_model)
    h_s = x_sample.reshape(dec_batch * dec_seq, d_model)
    tm_p = _pick(batch * seq, TOKEN_TILE)
    tm_s = dec_batch * dec_seq
    tq = _pick(seq, ATTN_Q_TILE)
    slopes = jnp.asarray([2.0 ** (-8.0 * (h + 1) / n_heads) for h in range(n_heads)], F32)
    row2 = lambda a: a.reshape(1, -1)
    gf = row2(g_final)

    outs = {name: [] for name in ("kp", "vp", "cp", "ks", "vs", "cs")}
    for l in range(depth):
        last = l == depth - 1
        lam_init = _lambda_init(l)
        lam = (jnp.exp(jnp.sum(lambda_q1[l] * lambda_k1[l])) - jnp.exp(jnp.sum(lambda_q2[l] * lambda_k2[l])) + lam_init)
        scal = jnp.concatenate([slopes, lam.reshape(1).astype(F32)])
        post_scale = 1.0 - lam_init
        w1g, w1u, w1d = w_ffn1_gate[l].astype(BF16), w_ffn1_up[l].astype(BF16), w_ffn1_down[l].astype(BF16)
        w2g, w2u, w2d = w_ffn2_gate[l].astype(BF16), w_ffn2_up[l].astype(BF16), w_ffn2_down[l].astype(BF16)
        w_in16 = w_in[l].astype(BF16)
        w_out16 = w_out[l].astype(BF16)
        w_pg16 = w_ple_gate[l].astype(BF16)
        w_pp16 = w_ple_proj[l].astype(BF16)
        tf = _pick(w1g.shape[1], 512)
        tn_proj = math.gcd(math.gcd(qk_cols, attn_width), math.gcd(conv_width, 1024))

        def ffn(h, g, wg, wu, wd, tm):
            hid = _ffn_up(h, row2(g), wg, wu, tm, tf)
            return _ffn_down(hid, wd, h, tm, _pick(d_model, 512))

        def pre_mix(h, tm):
            h = ffn(h, g_ffn1[l], w1g, w1u, w1d, tm)
            return (h,) + _in_proj(h, row2(g_mix[l]), w_in16, qk_cols, attn_width, conv_width, tm, tn_proj)

        def post_mix(h, o_attn, o_conv, p_l, tm):
            h = _out_proj(o_attn, o_conv, w_out16, h, tm, _pick(d_model, 1024))
            h = ffn(h, g_ffn2[l], w2g, w2u, w2d, tm)
            return _ple(h, row2(g_ple[l]), w_pg16, p_l, w_pp16, gf, last, min(tm, 512))

        conv_args = (w_dw[l], row2(b_dw[l]), row2(g_conv_norm[l]), row2(b_conv_norm[l]))

        h_p, q16, k32, k16, v32, v16, glu = pre_mix(h_p, tm_p)
        o_attn = _prompt_attn(q16, k16, v16, row2(g_subln[l]), scal, batch, seq, n_heads, post_scale, tq)
        zero_hist = jnp.zeros((batch, CONV_KERNEL - 1, conv_width), F32)
        o_conv, conv_new = _conv_module(glu.reshape(batch, seq, conv_width), zero_hist, *conv_args)
        h_p = post_mix(h_p, o_attn, o_conv.reshape(batch * seq, conv_width), p_prompt[l].reshape(batch * seq, -1), tm_p)
        outs["kp"].append(k32.reshape(batch, seq, n_heads, 2 * HEAD_DIM_QK))
        outs["vp"].append(v32.reshape(batch, seq, n_heads, HEAD_DIM_V))
        outs["cp"].append(conv_new)

        h_s, q16, k32, k16, v32, v16, glu = pre_mix(h_s, tm_s)
        k_new = k32.reshape(dec_batch, dec_seq, n_heads, 2 * HEAD_DIM_QK)
        v_new = v32.reshape(dec_batch, dec_seq, n_heads, HEAD_DIM_V)
        q5 = q16.reshape(dec_batch, dec_seq, n_heads, 2, HEAD_DIM_QK)
        q2 = jnp.einsum("bthmd,mn->bhmtnd", q5, jnp.eye(2, dtype=BF16))
        q2 = q2.reshape(dec_batch, 2 * n_heads * dec_seq, 2 * HEAD_DIM_QK)
        o_attn = _decode_attn(q2, cache_k, cache_v, l, k_new, v_new, row2(g_subln[l]), page_table, scal, post_scale)
        o_conv, conv_new = _conv_module(glu.reshape(dec_batch, dec_seq, conv_width), state_conv[l], *conv_args)
        h_s = post_mix(h_s, o_attn.reshape(dec_batch * dec_seq, attn_width),
                       o_conv.reshape(dec_batch * dec_seq, conv_width), p_sample[l].reshape(dec_batch * dec_seq, -1), tm_s)
        outs["ks"].append(k_new.reshape(dec_batch, dec_seq, n_heads, 2 * HEAD_DIM_QK))
        outs["vs"].append(v_new.reshape(dec_batch, dec_seq, n_heads, HEAD_DIM_V))
        outs["cs"].append(conv_new)

    y_prompt = h_p.reshape(batch, seq, d_model)
    y_sample = h_s.reshape(dec_batch, dec_seq, d_model)
    return (y_prompt, y_sample, jnp.stack(outs["kp"]), jnp.stack(outs["vp"]), jnp.stack(outs["cp"]),
            jnp.stack(outs["ks"]), jnp.stack(outs["vs"]), jnp.stack(outs["cs"]))
```

```python
import functools
import math

import jax
import jax.numpy as jnp
from jax import lax
from jax.experimental import pallas as pl
from jax.experimental.pallas import tpu as pltpu

F32 = jnp.float32
BF16 = jnp.bfloat16

HEAD_DIM_QK = 64
HEAD_DIM_V = 2 * HEAD_DIM_QK
CONV_GROUPS = 8
CONV_KERNEL = 31
PAGE_SIZE = 128
RMS_EPS = 1e-6
LN_EPS = 1e-5
NEG_INF = -1e30
LOG2E = math.log2(math.e)

V7X_VMEM_BYTES = 64 * 1024 * 1024
LANES = 128
SUBLANES = 8
CONV_PAD = 32
CONV_PARTIALS = 4
CONV_CHUNKS_PER_ITER = 4
DECODE_PAGES_PER_STEP = 8
ATTN_Q_TILE = 1024
ATTN_COL_GROUPS = 4
TOKEN_TILE = 1024


def _vmem_limit(nbytes):
    return int(min(V7X_VMEM_BYTES - 4 * 1024 * 1024, nbytes + 16 * 1024 * 1024))


def _params(sem, nbytes):
    return pltpu.CompilerParams(dimension_semantics=sem, vmem_limit_bytes=_vmem_limit(nbytes))


def _rms(x, g):
    ms = jnp.mean(x * x, axis=-1, keepdims=True)
    return x * lax.rsqrt(ms + RMS_EPS) * g


def _sigmoid(x):
    return 1.0 / (1.0 + jnp.exp(-x))


def _pick(n, pref):
    t = min(n, pref)
    assert n % t == 0, (n, t)
    return t


def _row_spec(tm, cols):
    return pl.BlockSpec((tm, cols), lambda i, j: (i, 0))


def _col_spec(rows, tn, col0=0):
    off = col0 // tn
    return pl.BlockSpec((rows, tn), lambda i, j: (0, j + off))


def _tile_spec(tm, tn):
    return pl.BlockSpec((tm, tn), lambda i, j: (i, j))


def _const_spec(rows, cols):
    return pl.BlockSpec((rows, cols), lambda i, j: (0, 0))


def _side_tile_spec(rows, tn, n_col_tiles):
    return pl.BlockSpec((rows, tn), lambda i, j: (0, jnp.where(i == 0, j, n_col_tiles - 1)))


def _stage_norm(x_ref, xs_ref, g_ref, n_ref, ns_ref):
    i, j = pl.program_id(0), pl.program_id(1)

    @pl.when(j == 0)
    def _():
        n_ref[...] = _rms(x_ref[...], g_ref[...]).astype(BF16)

    @pl.when(jnp.logical_and(i == 0, j == 0))
    def _():
        ns_ref[...] = _rms(xs_ref[...], g_ref[...]).astype(BF16)


def _norm_scratch(tm, ms, d):
    return [pltpu.VMEM((tm, d), BF16), pltpu.VMEM((ms, d), BF16)]


def _ffn_up_kernel(x_ref, xs_ref, g_ref, wg_ref, wu_ref, o_ref, os_ref, n_ref, ns_ref):
    _stage_norm(x_ref, xs_ref, g_ref, n_ref, ns_ref)

    def act(n):
        a = jnp.dot(n, wg_ref[...], preferred_element_type=F32)
        u = jnp.dot(n, wu_ref[...], preferred_element_type=F32)
        return (a * _sigmoid(a) * u).astype(BF16)

    o_ref[...] = act(n_ref[...])

    @pl.when(pl.program_id(0) == 0)
    def _():
        os_ref[...] = act(ns_ref[...])


def _ffn_up(x, xs, g, wg, wu, tm, tf):
    m, d = x.shape
    ms = xs.shape[0]
    f = wg.shape[1]
    nj = f // tf
    nbytes = 2 * tm * d * 4 + tm * d * 2 + 2 * 2 * d * tf * 2 + 2 * tm * tf * 2 + 3 * tm * tf * 4
    return pl.pallas_call(
        _ffn_up_kernel,
        grid=(m // tm, nj),
        in_specs=[_row_spec(tm, d), _const_spec(ms, d), _const_spec(1, d), _col_spec(d, tf), _col_spec(d, tf)],
        out_specs=[_tile_spec(tm, tf), _side_tile_spec(ms, tf, nj)],
        out_shape=[jax.ShapeDtypeStruct((m, f), BF16), jax.ShapeDtypeStruct((ms, f), BF16)],
        scratch_shapes=_norm_scratch(tm, ms, d),
        compiler_params=_params(("arbitrary", "arbitrary"), nbytes),
        name="ffn_up",
    )(x, xs, g, wg, wu)


def _ffn_down_kernel(h_ref, hs_ref, w_ref, r_ref, rs_ref, o_ref, os_ref):
    o_ref[...] = r_ref[...] + 0.5 * jnp.dot(h_ref[...], w_ref[...], preferred_element_type=F32)

    @pl.when(pl.program_id(0) == 0)
    def _():
        os_ref[...] = rs_ref[...] + 0.5 * jnp.dot(hs_ref[...], w_ref[...], preferred_element_type=F32)


def _ffn_down(hid, hid_s, wd, res, res_s, tm, tn):
    m, f = hid.shape
    ms = hid_s.shape[0]
    d = wd.shape[1]
    nj = d // tn
    nbytes = 2 * tm * f * 2 + 2 * f * tn * 2 + 4 * tm * tn * 4 + tm * tn * 4
    return pl.pallas_call(
        _ffn_down_kernel,
        grid=(m // tm, nj),
        in_specs=[_row_spec(tm, f), _const_spec(ms, f), _col_spec(f, tn), _tile_spec(tm, tn),
                  _side_tile_spec(ms, tn, nj)],
        out_specs=[_tile_spec(tm, tn), _side_tile_spec(ms, tn, nj)],
        out_shape=[jax.ShapeDtypeStruct((m, d), F32), jax.ShapeDtypeStruct((ms, d), F32)],
        compiler_params=_params(("arbitrary", "arbitrary"), nbytes),
        name="ffn_down",
    )(hid, hid_s, wd, res, res_s)


def _proj_kernel(scale, x_ref, xs_ref, g_ref, w_ref, o_ref, os_ref, n_ref, ns_ref):
    _stage_norm(x_ref, xs_ref, g_ref, n_ref, ns_ref)

    def proj(n):
        u = jnp.dot(n, w_ref[...], preferred_element_type=F32)
        return (u if scale is None else u * scale).astype(o_ref.dtype)

    o_ref[...] = proj(n_ref[...])

    @pl.when(pl.program_id(0) == 0)
    def _():
        os_ref[...] = proj(ns_ref[...])


def _glu_proj_kernel(x_ref, xs_ref, g_ref, wa_ref, wg_ref, o_ref, os_ref, n_ref, ns_ref):
    _stage_norm(x_ref, xs_ref, g_ref, n_ref, ns_ref)

    def glu(n):
        a = jnp.dot(n, wa_ref[...], preferred_element_type=F32)
        gt = jnp.dot(n, wg_ref[...], preferred_element_type=F32)
        return a * _sigmoid(gt)

    o_ref[...] = glu(n_ref[...])

    @pl.when(pl.program_id(0) == 0)
    def _():
        os_ref[...] = glu(ns_ref[...])


def _in_proj(x, xs, g, w_in, qk_cols, attn_width, conv_width, tm, tn):
    m, d = x.shape
    ms = xs.shape[0]
    q_scale = HEAD_DIM_QK ** -0.5 * LOG2E
    nbytes = 2 * tm * d * 4 + tm * d * 2 + 4 * d * tn * 2 + 8 * tm * tn * 4

    def call(kernel, col0s, cols, dtype, name):
        nj = cols // tn
        return pl.pallas_call(
            kernel,
            grid=(m // tm, nj),
            in_specs=[_row_spec(tm, d), _const_spec(ms, d), _const_spec(1, d)] + [_col_spec(d, tn, c) for c in col0s],
            out_specs=[_tile_spec(tm, tn), _side_tile_spec(ms, tn, nj)],
            out_shape=[jax.ShapeDtypeStruct((m, cols), dtype), jax.ShapeDtypeStruct((ms, cols), dtype)],
            scratch_shapes=_norm_scratch(tm, ms, d),
            compiler_params=_params(("arbitrary", "arbitrary"), nbytes),
            name=name,
        )(x, xs, g, *([w_in] * len(col0s)))

    q = call(functools.partial(_proj_kernel, q_scale), [0], qk_cols, BF16, "q_proj")
    k = call(functools.partial(_proj_kernel, None), [qk_cols], qk_cols, F32, "k_proj")
    v = call(functools.partial(_proj_kernel, None), [2 * qk_cols], attn_width, F32, "v_proj")
    c0 = 2 * qk_cols + attn_width
    glu = call(_glu_proj_kernel, [c0, c0 + conv_width], conv_width, F32, "glu_proj")
    return q, k, v, glu


def _split_maps(q):
    lane = lax.broadcasted_iota(jnp.int32, q.shape, 1)
    zero = jnp.zeros_like(q)
    return jnp.concatenate([jnp.where(lane < HEAD_DIM_QK, q, zero), jnp.where(lane >= HEAD_DIM_QK, q, zero)], axis=0)


def _subln(o, g, post_scale):
    ms = jnp.mean(o * o, axis=-1, keepdims=True)
    return o * lax.rsqrt(ms + RMS_EPS) * g * post_scale


def _prompt_attn_kernel(post_scale, tq, n_heads, n_groups, sc_ref, q_ref, k_ref, v_ref, g_ref, o_ref, qq_ref, k16_ref,
                        vt_ref, *stat_refs):
    h = pl.program_id(1)
    qi = pl.program_id(2)
    slope = sc_ref[h]
    lam = sc_ref[n_heads]
    gw = 2 * tq // n_groups
    groups = [stat_refs[3 * g:3 * g + 3] for g in range(n_groups)]

    @pl.when(qi == 0)
    def _():
        k16_ref[...] = k_ref[...].astype(BF16)
        vt_ref[...] = v_ref[...].T.astype(BF16)

    qq_ref[...] = _split_maps(q_ref[...])
    for m_ref, l_ref, acc_ref in groups:
        m_ref[...] = jnp.full(m_ref.shape, NEG_INF, F32)
        l_ref[...] = jnp.zeros(l_ref.shape, F32)
        acc_ref[...] = jnp.zeros(acc_ref.shape, F32)

    def block(ki, masked):
        start = pl.multiple_of(ki * tq, tq)
        k = k16_ref[pl.ds(start, tq), :]
        vt = vt_ref[:, pl.ds(start, tq)]
        key = lax.broadcasted_iota(jnp.int32, (tq, LANES), 0)
        kbias = slope * ((ki - qi) * tq + key).astype(F32)
        kbias = jnp.concatenate([kbias] * (gw // LANES), axis=1)
        for g, (m_ref, l_ref, acc_ref) in enumerate(groups):
            qg = qq_ref[g * gw:(g + 1) * gw, :]
            s = lax.dot_general(k, qg, (((1,), (1,)), ((), ())), preferred_element_type=F32) + kbias
            if masked:
                keyf = lax.broadcasted_iota(jnp.int32, (tq, gw), 0)
                query = (g * gw) % tq + lax.broadcasted_iota(jnp.int32, (tq, gw), 1)
                s = jnp.where(keyf <= query, s, NEG_INF)
            m_prev = m_ref[...]
            m_new = jnp.maximum(m_prev, jnp.max(s, axis=0, keepdims=True))
            alpha = jnp.exp2(m_prev - m_new)
            p = jnp.exp2(s - m_new)
            l_ref[...] = alpha * l_ref[...] + jnp.sum(p, axis=0, keepdims=True)
            acc_ref[...] = alpha * acc_ref[...] + jnp.dot(vt, p.astype(BF16), preferred_element_type=F32)
            m_ref[...] = m_new

    def body(ki, carry):
        block(ki, False)
        return carry

    lax.fori_loop(0, qi, body, 0)
    block(qi, True)
    o = jnp.concatenate([acc_ref[...] / l_ref[...] for _, l_ref, acc_ref in groups], axis=1)
    o = o[:, :tq] - lam * o[:, tq:]
    ms = jnp.mean(o * o, axis=0, keepdims=True)
    o = o * (lax.rsqrt(ms + RMS_EPS) * post_scale)
    o_ref[...] = (o.T * g_ref[...]).astype(o_ref.dtype)


def _prompt_attn(q16, k32, v32, g_subln, scal, batch, seq, n_heads, post_scale, tq):
    nq = seq // tq
    gw = 2 * tq // ATTN_COL_GROUPS
    grid_spec = pltpu.PrefetchScalarGridSpec(
        num_scalar_prefetch=1,
        grid=(batch, n_heads, nq),
        in_specs=[
            pl.BlockSpec((tq, HEAD_DIM_V), lambda b, h, i, sc: (b * nq + i, h)),
            pl.BlockSpec((seq, HEAD_DIM_V), lambda b, h, i, sc: (b, h)),
            pl.BlockSpec((seq, HEAD_DIM_V), lambda b, h, i, sc: (b, h)),
            pl.BlockSpec((1, HEAD_DIM_V), lambda b, h, i, sc: (0, 0)),
        ],
        out_specs=pl.BlockSpec((tq, HEAD_DIM_V), lambda b, h, i, sc: (b * nq + i, h)),
        scratch_shapes=[
            pltpu.VMEM((2 * tq, HEAD_DIM_V), BF16),
            pltpu.VMEM((seq, HEAD_DIM_V), BF16),
            pltpu.VMEM((HEAD_DIM_V, seq), BF16),
        ] + [pltpu.VMEM((1, gw), F32), pltpu.VMEM((1, gw), F32), pltpu.VMEM((HEAD_DIM_V, gw), F32)] * ATTN_COL_GROUPS,
    )
    nbytes = 4 * seq * HEAD_DIM_V * 4 + 2 * seq * HEAD_DIM_V * 2 + 8 * 2 * tq * tq * 4
    return pl.pallas_call(
        functools.partial(_prompt_attn_kernel, post_scale, tq, n_heads, ATTN_COL_GROUPS),
        grid_spec=grid_spec,
        out_shape=jax.ShapeDtypeStruct((batch * seq, n_heads * HEAD_DIM_V), BF16),
        compiler_params=_params(("parallel", "parallel", "arbitrary"), nbytes),
        name="prompt_attn",
    )(scal, q16, k32, v32, g_subln)


def _decode_attn_kernel(post_scale, n_heads, t_new, past_len, grp, pt_ref, sc_ref, q_ref, *refs):
    k_refs, v_refs = refs[:grp], refs[grp:2 * grp]
    kn_ref, vn_ref, g_ref, o_ref, bias_ref, slope_ref, m_ref, l_ref, acc_ref = refs[2 * grp:]
    p = pl.program_id(1)
    rows = 2 * n_heads * t_new
    rows_per_head = 2 * t_new
    cols = PAGE_SIZE * n_heads

    def head_geometry(ncols):
        row = lax.broadcasted_iota(jnp.int32, (rows, ncols), 0)
        col = lax.broadcasted_iota(jnp.int32, (rows, ncols), 1)
        rhead = row // rows_per_head
        slope = jnp.zeros((rows, ncols), F32)
        for hh in range(n_heads):
            slope = jnp.where(rhead == hh, sc_ref[hh], slope)
        return row, col, rhead == col % n_heads, col // n_heads, slope

    @pl.when(p == 0)
    def _():
        m_ref[...] = jnp.full(m_ref.shape, NEG_INF, F32)
        l_ref[...] = jnp.zeros(l_ref.shape, F32)
        acc_ref[...] = jnp.zeros(acc_ref.shape, F32)
        _, _, same_head, tok, slope = head_geometry(cols)
        bias_ref[...] = jnp.where(same_head, slope * tok.astype(F32), NEG_INF)
        slope_ref[...] = slope[:, 0:1]

    q = q_ref[...]

    def update(blocks):
        scores = [lax.dot_general(q, k2, (((1,), (1,)), ((), ())), preferred_element_type=F32) + bias
                  for k2, _, bias, _ in blocks]
        m_prev = m_ref[...]
        m_new = m_prev
        for s, (_, _, _, shift) in zip(scores, blocks):
            m_new = jnp.maximum(m_new, jnp.max(s, axis=-1, keepdims=True) + shift)
        alpha = jnp.exp2(m_prev - m_new)
        l_new = alpha * l_ref[...]
        acc = alpha * acc_ref[...]
        for s, (_, v2, _, shift) in zip(scores, blocks):
            pr = jnp.exp2(s - (m_new - shift))
            l_new = l_new + jnp.sum(pr, axis=-1, keepdims=True)
            acc = acc + jnp.dot(pr.astype(BF16), v2, preferred_element_type=F32)
        l_ref[...] = l_new
        acc_ref[...] = acc
        m_ref[...] = m_new

    blocks = []
    for g in range(grp):
        k2 = k_refs[g][...].reshape(cols, HEAD_DIM_V).astype(BF16)
        v2 = v_refs[g][...].reshape(cols, HEAD_DIM_V).astype(BF16)
        page_start = ((p * grp + g) * PAGE_SIZE - past_len).astype(F32)
        blocks.append((k2, v2, bias_ref[...], slope_ref[...] * page_start))
    update(blocks)

    @pl.when(p == pl.num_programs(1) - 1)
    def _():
        n_new = t_new * n_heads
        ncols = pl.cdiv(n_new, LANES) * LANES
        pad = jnp.zeros((ncols - n_new, HEAD_DIM_V), F32)
        kn = jnp.concatenate([kn_ref[...].reshape(n_new, HEAD_DIM_V), pad], axis=0).astype(BF16)
        vn = jnp.concatenate([vn_ref[...].reshape(n_new, HEAD_DIM_V), pad], axis=0).astype(BF16)
        row, col, same_head, tok, slope = head_geometry(ncols)
        visible = jnp.where(same_head, tok, ncols) <= row % t_new
        update([(kn, vn, jnp.where(visible, slope * tok.astype(F32), NEG_INF), 0.0)])
        o = acc_ref[...] / l_ref[...]
        lam = sc_ref[n_heads]
        gain = g_ref[...]
        outs = []
        for hh in range(n_heads):
            r0 = hh * rows_per_head
            oh = o[r0:r0 + t_new] - lam * o[r0 + t_new:r0 + 2 * t_new]
            outs.append(_subln(oh, gain, post_scale))
        o_ref[...] = jnp.concatenate(outs, axis=1).astype(o_ref.dtype)


def _decode_attn(q2, cache_k, cache_v, layer, k_new, v_new, g_subln, page_table, scal, post_scale):
    batch, n_pages = page_table.shape
    _, t_new, n_heads, _ = k_new.shape
    rows = 2 * n_heads * t_new
    width = n_heads * HEAD_DIM_V
    past_len = n_pages * PAGE_SIZE
    grp = _pick(n_pages, DECODE_PAGES_PER_STEP)
    page_block = (None, None, PAGE_SIZE, n_heads, HEAD_DIM_V)

    def page_spec(g):
        return pl.BlockSpec(page_block, lambda b, p, pt, sc: (layer, pt[b, p * grp + g], 0, 0, 0))

    new_spec = pl.BlockSpec((None, t_new, n_heads, HEAD_DIM_V), lambda b, p, pt, sc: (b, 0, 0, 0))
    grid_spec = pltpu.PrefetchScalarGridSpec(
        num_scalar_prefetch=2,
        grid=(batch, n_pages // grp),
        in_specs=[pl.BlockSpec((None, rows, HEAD_DIM_V), lambda b, p, pt, sc: (b, 0, 0))]
        + [page_spec(g) for g in range(grp)] * 2
        + [new_spec, new_spec, pl.BlockSpec((1, HEAD_DIM_V), lambda b, p, pt, sc: (0, 0))],
        out_specs=pl.BlockSpec((None, t_new, width), lambda b, p, pt, sc: (b, 0, 0)),
        scratch_shapes=[
            pltpu.VMEM((rows, PAGE_SIZE * n_heads), F32),
            pltpu.VMEM((rows, 1), F32),
            pltpu.VMEM((rows, 1), F32),
            pltpu.VMEM((rows, 1), F32),
            pltpu.VMEM((rows, HEAD_DIM_V), F32),
        ],
    )
    page_bytes = PAGE_SIZE * width * 4
    nbytes = 4 * grp * page_bytes + 6 * rows * PAGE_SIZE * n_heads * 4
    return pl.pallas_call(
        functools.partial(_decode_attn_kernel, post_scale, n_heads, t_new, past_len, grp),
        grid_spec=grid_spec,
        out_shape=jax.ShapeDtypeStruct((batch, t_new, width), BF16),
        compiler_params=_params(("parallel", "arbitrary"), nbytes),
        name="decode_attn",
    )(page_table, scal, q2, *([cache_k] * grp), *([cache_v] * grp), k_new, v_new, g_subln)


def _conv_kernel(seq, chunk, u_ref, prev_ref, w_ref, bdw_ref, gcn_ref, bcn_ref, o_ref, new_ref, ext_ref):
    hist = CONV_KERNEL - 1
    lead = CONV_PAD - hist
    ext_ref[0:lead, :] = jnp.zeros((lead, LANES), F32)
    ext_ref[lead:CONV_PAD, :] = prev_ref[0]
    ext_ref[CONV_PAD:CONV_PAD + seq, :] = u_ref[0]
    new_ref[0] = ext_ref[lead + seq:CONV_PAD + seq, :]
    bdw = bdw_ref[...]
    gcn = gcn_ref[...]
    bcn = bcn_ref[...]

    def body(c, _):
        r0 = pl.multiple_of(c * chunk, chunk)
        parts = [None] * CONV_PARTIALS
        for j in range(CONV_KERNEL):
            term = w_ref[j:j + 1, :] * ext_ref[pl.ds(r0 + (lead + j), chunk), :]
            slot = j % CONV_PARTIALS
            parts[slot] = term if parts[slot] is None else parts[slot] + term
        y = functools.reduce(lambda a, b: a + b, parts) + bdw
        mu = jnp.mean(y, axis=-1, keepdims=True)
        yc = y - mu
        var = jnp.mean(yc * yc, axis=-1, keepdims=True)
        yn = yc * lax.rsqrt(var + LN_EPS) * gcn + bcn
        o_ref[0, pl.ds(r0, chunk), :] = (yn * _sigmoid(yn)).astype(o_ref.dtype)
        return 0

    n_chunks = seq // chunk
    lax.fori_loop(0, n_chunks, body, 0, unroll=math.gcd(n_chunks, CONV_CHUNKS_PER_ITER))


def _conv_module(u, prev, w_dw, b_dw, g_cn, b_cn):
    batch, seq, width = u.shape
    assert width // CONV_GROUPS == LANES
    chunk = _pick(seq, 64)
    hist = CONV_KERNEL - 1
    vec = pl.BlockSpec((1, LANES), lambda b, c: (0, c))
    return pl.pallas_call(
        functools.partial(_conv_kernel, seq, chunk),
        grid=(batch, width // LANES),
        in_specs=[
            pl.BlockSpec((1, seq, LANES), lambda b, c: (b, 0, c)),
            pl.BlockSpec((1, hist, LANES), lambda b, c: (b, 0, c)),
            pl.BlockSpec((CONV_KERNEL, LANES), lambda b, c: (0, c)),
            vec, vec, vec,
        ],
        out_specs=[
            pl.BlockSpec((1, seq, LANES), lambda b, c: (b, 0, c)),
            pl.BlockSpec((1, hist, LANES), lambda b, c: (b, 0, c)),
        ],
        out_shape=[jax.ShapeDtypeStruct((batch, seq, width), BF16), jax.ShapeDtypeStruct((batch, hist, width), F32)],
        scratch_shapes=[pltpu.VMEM((CONV_PAD + seq, LANES), F32)],
        compiler_params=_params(("parallel", "parallel"), 8 * (seq + CONV_PAD) * LANES * 4),
        name="conv_module",
    )(u, prev, w_dw, b_dw, g_cn, b_cn)


def _out_proj_kernel(ka, a_ref, b_ref, as_ref, bs_ref, w_ref, r_ref, rs_ref, o_ref, os_ref):
    def proj(a, b):
        acc = jnp.dot(a, w_ref[0:ka, :], preferred_element_type=F32)
        return acc + jnp.dot(b, w_ref[ka:, :], preferred_element_type=F32)

    o_ref[...] = r_ref[...] + proj(a_ref[...], b_ref[...])

    @pl.when(pl.program_id(0) == 0)
    def _():
        os_ref[...] = rs_ref[...] + proj(as_ref[...], bs_ref[...])


def _out_proj(a, b, a_s, b_s, w, res, res_s, tm, tn):
    m, ka = a.shape
    ms = a_s.shape[0]
    kb = b.shape[1]
    d = w.shape[1]
    nj = d // tn
    nbytes = 2 * tm * (ka + kb) * 2 + 2 * (ka + kb) * tn * 2 + 5 * tm * tn * 4
    return pl.pallas_call(
        functools.partial(_out_proj_kernel, ka),
        grid=(m // tm, nj),
        in_specs=[_row_spec(tm, ka), _row_spec(tm, kb), _const_spec(ms, ka), _const_spec(ms, kb),
                  _col_spec(ka + kb, tn), _tile_spec(tm, tn), _side_tile_spec(ms, tn, nj)],
        out_specs=[_tile_spec(tm, tn), _side_tile_spec(ms, tn, nj)],
        out_shape=[jax.ShapeDtypeStruct((m, d), F32), jax.ShapeDtypeStruct((ms, d), F32)],
        compiler_params=_params(("arbitrary", "arbitrary"), nbytes),
        name="out_proj",
    )(a, b, a_s, b_s, w, res, res_s)


def _ple_kernel(final_norm, h_ref, p_ref, hs_ref, ps_ref, g_ref, wg_ref, wp_ref, gf_ref, o_ref, os_ref):
    def ple(h, p):
        n = _rms(h, g_ref[...]).astype(BF16)
        gate = _sigmoid(jnp.dot(n, wg_ref[...], preferred_element_type=F32))
        proj = jnp.dot(p.astype(BF16), wp_ref[...], preferred_element_type=F32)
        h = h + proj * gate
        return _rms(h, gf_ref[...]) if final_norm else h

    o_ref[...] = ple(h_ref[...], p_ref[...])

    @pl.when(pl.program_id(0) == 0)
    def _():
        os_ref[...] = ple(hs_ref[...], ps_ref[...])


def _ple(h, p, h_s, p_s, g, wg, wp, g_final, final_norm, tm):
    m, d = h.shape
    ms = h_s.shape[0]
    dp = p.shape[1]
    nbytes = 4 * tm * d * 4 + 2 * d * d * 2 + 2 * dp * d * 2 + 2 * tm * dp * 4 + 4 * tm * d * 4
    row = lambda cols: pl.BlockSpec((tm, cols), lambda i: (i, 0))
    const = lambda rows, cols: pl.BlockSpec((rows, cols), lambda i: (0, 0))
    return pl.pallas_call(
        functools.partial(_ple_kernel, final_norm),
        grid=(m // tm,),
        in_specs=[row(d), row(dp), const(ms, d), const(ms, dp), const(1, d), const(d, d), const(dp, d), const(1, d)],
        out_specs=[row(d), const(ms, d)],
        out_shape=[jax.ShapeDtypeStruct((m, d), F32), jax.ShapeDtypeStruct((ms, d), F32)],
        compiler_params=_params(("arbitrary",), nbytes),
        name="ple",
    )(h, p, h_s, p_s, g, wg, wp, g_final)


def _lambda_init(layer_idx):
    return 0.8 - 0.6 * math.exp(-0.3 * layer_idx)


def kernel(x_prompt, x_sample, cache_k, cache_v, state_conv, page_table, p_prompt, p_sample, g_ffn1, w_ffn1_gate, w_ffn1_up, w_ffn1_down, g_mix, w_in, lambda_q1, lambda_k1, lambda_q2, lambda_k2, g_subln, w_dw, b_dw, g_conv_norm, b_conv_norm, w_out, g_ffn2, w_ffn2_gate, w_ffn2_up, w_ffn2_down, g_ple, w_ple_gate, w_ple_proj, g_final):
    depth = w_in.shape[0]
    batch, seq, d_model = x_prompt.shape
    dec_batch, dec_seq, _ = x_sample.shape
    n_heads = cache_k.shape[3]
    assert cache_k.shape[4] == 2 * HEAD_DIM_QK and cache_v.shape[4] == HEAD_DIM_V and cache_k.shape[2] == PAGE_SIZE
    qk_cols = n_heads * 2 * HEAD_DIM_QK
    attn_width = n_heads * HEAD_DIM_V
    conv_width = d_model - attn_width
    n_p, n_s = batch * seq, dec_batch * dec_seq
    h_p = x_prompt.reshape(n_p, d_model)
    h_s = x_sample.reshape(n_s, d_model)
    tm = _pick(n_p, TOKEN_TILE)
    tq = _pick(seq, ATTN_Q_TILE)
    slopes = jnp.asarray([LOG2E * 2.0 ** (-8.0 * (h + 1) / n_heads) for h in range(n_heads)], F32)
    row2 = lambda a: a.reshape(1, -1)
    gf = row2(g_final)

    outs = {name: [] for name in ("kp", "vp", "cp", "ks", "vs", "cs")}
    for l in range(depth):
        last = l == depth - 1
        lam_init = _lambda_init(l)
        lam = (jnp.exp(jnp.sum(lambda_q1[l] * lambda_k1[l])) - jnp.exp(jnp.sum(lambda_q2[l] * lambda_k2[l])) + lam_init)
        scal = jnp.concatenate([slopes, lam.reshape(1).astype(F32)])
        post_scale = 1.0 - lam_init
        w1g, w1u, w1d = w_ffn1_gate[l].astype(BF16), w_ffn1_up[l].astype(BF16), w_ffn1_down[l].astype(BF16)
        w2g, w2u, w2d = w_ffn2_gate[l].astype(BF16), w_ffn2_up[l].astype(BF16), w_ffn2_down[l].astype(BF16)
        w_in16 = w_in[l].astype(BF16)
        w_out16 = w_out[l].astype(BF16)
        w_pg16 = w_ple_gate[l].astype(BF16)
        w_pp16 = w_ple_proj[l].astype(BF16)
        tf = _pick(w1g.shape[1], 512)
        tn_proj = math.gcd(math.gcd(qk_cols, attn_width), math.gcd(conv_width, 1024))
        g_sub = row2(g_subln[l])

        def ffn(hp, hs, g, wg, wu, wd):
            hid_p, hid_s = _ffn_up(hp, hs, row2(g), wg, wu, tm, tf)
            return _ffn_down(hid_p, hid_s, wd, hp, hs, tm, _pick(d_model, 512))

        h_p, h_s = ffn(h_p, h_s, g_ffn1[l], w1g, w1u, w1d)
        (q_p, q_s), (k_p, k_s), (v_p, v_s), (glu_p, glu_s) = _in_proj(
            h_p, h_s, row2(g_mix[l]), w_in16, qk_cols, attn_width, conv_width, tm, tn_proj)
        conv_args = (w_dw[l], row2(b_dw[l]), row2(g_conv_norm[l]), row2(b_conv_norm[l]))

        attn_p = _prompt_attn(q_p, k_p, v_p, g_sub, scal, batch, seq, n_heads, post_scale, tq)
        zero_hist = jnp.zeros((batch, CONV_KERNEL - 1, conv_width), F32)
        conv_p, hist_p = _conv_module(glu_p.reshape(batch, seq, conv_width), zero_hist, *conv_args)
        outs["kp"].append(k_p.reshape(batch, seq, n_heads, 2 * HEAD_DIM_QK))
        outs["vp"].append(v_p.reshape(batch, seq, n_heads, HEAD_DIM_V))
        outs["cp"].append(hist_p)

        k_new = k_s.reshape(dec_batch, dec_seq, n_heads, 2 * HEAD_DIM_QK)
        v_new = v_s.reshape(dec_batch, dec_seq, n_heads, HEAD_DIM_V)
        q5 = q_s.reshape(dec_batch, dec_seq, n_heads, 2, HEAD_DIM_QK)
        q2 = jnp.einsum("bthmd,mn->bhmtnd", q5, jnp.eye(2, dtype=BF16))
        q2 = q2.reshape(dec_batch, 2 * n_heads * dec_seq, 2 * HEAD_DIM_QK)
        attn_s = _decode_attn(q2, cache_k, cache_v, l, k_new, v_new, g_sub, page_table, scal, post_scale)
        conv_s, hist_s = _conv_module(glu_s.reshape(dec_batch, dec_seq, conv_width), state_conv[l], *conv_args)
        outs["ks"].append(k_new)
        outs["vs"].append(v_new)
        outs["cs"].append(hist_s)

        h_p, h_s = _out_proj(attn_p, conv_p.reshape(n_p, conv_width), attn_s.reshape(n_s, attn_width),
                             conv_s.reshape(n_s, conv_width), w_out16, h_p, h_s, tm, _pick(d_model, 1024))
        h_p, h_s = ffn(h_p, h_s, g_ffn2[l], w2g, w2u, w2d)
        h_p, h_s = _ple(h_p, p_prompt[l].reshape(n_p, -1), h_s, p_sample[l].reshape(n_s, -1), row2(g_ple[l]),
                        w_pg16, w_pp16, gf, last, min(tm, 512))

    y_prompt = h_p.reshape(batch, seq, d_model)
    y_sample = h_s.reshape(dec_batch, dec_seq, d_model)
    return (y_prompt, y_sample, jnp.stack(outs["kp"]), jnp.stack(outs["vp"]), jnp.stack(outs["cp"]),
            jnp.stack(outs["ks"]), jnp.stack(outs["vs"]), jnp.stack(outs["cs"]))
```

```python
import functools
import math

import jax
import jax.numpy as jnp
from jax import lax
from jax.experimental import pallas as pl
from jax.experimental.pallas import tpu as pltpu

F32 = jnp.float32
BF16 = jnp.bfloat16

HEAD_DIM_QK = 64
HEAD_DIM_V = 2 * HEAD_DIM_QK
CONV_GROUPS = 8
CONV_KERNEL = 31
PAGE_SIZE = 128
RMS_EPS = 1e-6
LN_EPS = 1e-5
NEG_INF = -1e30
LOG2E = math.log2(math.e)

V7X_VMEM_BYTES = 64 * 1024 * 1024
LANES = 128
SUBLANES = 8
CONV_PAD = 32
CONV_PARTIALS = 4
CONV_CHUNKS_PER_ITER = 4
DECODE_PAGES_PER_STEP = 8
ATTN_Q_TILE = 1024
ATTN_COL_GROUPS = 4
TOKEN_TILE = 1024


def _vmem_limit(nbytes):
    return int(min(V7X_VMEM_BYTES - 4 * 1024 * 1024, nbytes + 16 * 1024 * 1024))


def _params(sem, nbytes):
    return pltpu.CompilerParams(dimension_semantics=sem, vmem_limit_bytes=_vmem_limit(nbytes))


def _rms(x, g):
    ms = jnp.mean(x * x, axis=-1, keepdims=True)
    return x * lax.rsqrt(ms + RMS_EPS) * g


def _sigmoid(x):
    return 1.0 / (1.0 + jnp.exp(-x))


def _pick(n, pref):
    t = min(n, pref)
    assert n % t == 0, (n, t)
    return t


def _row_spec(tm, cols):
    return pl.BlockSpec((tm, cols), lambda i, j: (i, 0))


def _col_spec(rows, tn, col0=0):
    off = col0 // tn
    return pl.BlockSpec((rows, tn), lambda i, j: (0, j + off))


def _tile_spec(tm, tn):
    return pl.BlockSpec((tm, tn), lambda i, j: (i, j))


def _const_spec(rows, cols):
    return pl.BlockSpec((rows, cols), lambda i, j: (0, 0))


def _side_tile_spec(rows, tn, n_col_tiles):
    return pl.BlockSpec((rows, tn), lambda i, j: (0, jnp.where(i == 0, j, n_col_tiles - 1)))


def _stage_norm(x_ref, xs_ref, g_ref, n_ref, ns_ref):
    i, j = pl.program_id(0), pl.program_id(1)

    @pl.when(j == 0)
    def _():
        n_ref[...] = _rms(x_ref[...], g_ref[...]).astype(BF16)

    @pl.when(jnp.logical_and(i == 0, j == 0))
    def _():
        ns_ref[...] = _rms(xs_ref[...], g_ref[...]).astype(BF16)


def _norm_scratch(tm, ms, d):
    return [pltpu.VMEM((tm, d), BF16), pltpu.VMEM((ms, d), BF16)]


def _ffn_up_kernel(x_ref, xs_ref, g_ref, wg_ref, wu_ref, o_ref, os_ref, n_ref, ns_ref):
    _stage_norm(x_ref, xs_ref, g_ref, n_ref, ns_ref)

    wg = wg_ref[...].astype(BF16)
    wu = wu_ref[...].astype(BF16)

    def act(n):
        a = jnp.dot(n, wg, preferred_element_type=F32)
        u = jnp.dot(n, wu, preferred_element_type=F32)
        return (a * _sigmoid(a) * u).astype(BF16)

    o_ref[...] = act(n_ref[...])

    @pl.when(pl.program_id(0) == 0)
    def _():
        os_ref[...] = act(ns_ref[...])


def _ffn_up(x, xs, g, wg, wu, tm, tf):
    m, d = x.shape
    ms = xs.shape[0]
    f = wg.shape[1]
    nj = f // tf
    nbytes = 2 * tm * d * 4 + tm * d * 2 + 2 * 2 * d * tf * 4 + 2 * d * tf * 2 + 2 * tm * tf * 2 + 3 * tm * tf * 4
    return pl.pallas_call(
        _ffn_up_kernel,
        grid=(m // tm, nj),
        in_specs=[_row_spec(tm, d), _const_spec(ms, d), _const_spec(1, d), _col_spec(d, tf), _col_spec(d, tf)],
        out_specs=[_tile_spec(tm, tf), _side_tile_spec(ms, tf, nj)],
        out_shape=[jax.ShapeDtypeStruct((m, f), BF16), jax.ShapeDtypeStruct((ms, f), BF16)],
        scratch_shapes=_norm_scratch(tm, ms, d),
        compiler_params=_params(("arbitrary", "arbitrary"), nbytes),
        name="ffn_up",
    )(x, xs, g, wg, wu)


def _ffn_down_kernel(h_ref, hs_ref, w_ref, r_ref, rs_ref, o_ref, os_ref):
    w = w_ref[...].astype(BF16)
    o_ref[...] = r_ref[...] + 0.5 * jnp.dot(h_ref[...], w, preferred_element_type=F32)

    @pl.when(pl.program_id(0) == 0)
    def _():
        os_ref[...] = rs_ref[...] + 0.5 * jnp.dot(hs_ref[...], w, preferred_element_type=F32)


def _ffn_down(hid, hid_s, wd, res, res_s, tm, tn):
    m, f = hid.shape
    ms = hid_s.shape[0]
    d = wd.shape[1]
    nj = d // tn
    nbytes = 2 * tm * f * 2 + 2 * f * tn * 4 + f * tn * 2 + 4 * tm * tn * 4 + tm * tn * 4
    return pl.pallas_call(
        _ffn_down_kernel,
        grid=(m // tm, nj),
        in_specs=[_row_spec(tm, f), _const_spec(ms, f), _col_spec(f, tn), _tile_spec(tm, tn),
                  _side_tile_spec(ms, tn, nj)],
        out_specs=[_tile_spec(tm, tn), _side_tile_spec(ms, tn, nj)],
        out_shape=[jax.ShapeDtypeStruct((m, d), F32), jax.ShapeDtypeStruct((ms, d), F32)],
        compiler_params=_params(("arbitrary", "arbitrary"), nbytes),
        name="ffn_down",
    )(hid, hid_s, wd, res, res_s)


def _proj_kernel(scale, x_ref, xs_ref, g_ref, w_ref, o_ref, os_ref, n_ref, ns_ref):
    _stage_norm(x_ref, xs_ref, g_ref, n_ref, ns_ref)

    w = w_ref[...].astype(BF16)

    def proj(n):
        u = jnp.dot(n, w, preferred_element_type=F32)
        return (u if scale is None else u * scale).astype(o_ref.dtype)

    o_ref[...] = proj(n_ref[...])

    @pl.when(pl.program_id(0) == 0)
    def _():
        os_ref[...] = proj(ns_ref[...])


def _glu_proj_kernel(x_ref, xs_ref, g_ref, wa_ref, wg_ref, o_ref, os_ref, n_ref, ns_ref):
    _stage_norm(x_ref, xs_ref, g_ref, n_ref, ns_ref)

    wa = wa_ref[...].astype(BF16)
    wg = wg_ref[...].astype(BF16)

    def glu(n):
        a = jnp.dot(n, wa, preferred_element_type=F32)
        gt = jnp.dot(n, wg, preferred_element_type=F32)
        return a * _sigmoid(gt)

    o_ref[...] = glu(n_ref[...])

    @pl.when(pl.program_id(0) == 0)
    def _():
        os_ref[...] = glu(ns_ref[...])


def _in_proj(x, xs, g, w_in, qk_cols, attn_width, conv_width, tm, tn):
    m, d = x.shape
    ms = xs.shape[0]
    q_scale = HEAD_DIM_QK ** -0.5 * LOG2E
    nbytes = 2 * tm * d * 4 + tm * d * 2 + 4 * d * tn * 4 + 2 * d * tn * 2 + 6 * tm * tn * 4

    def call(kernel, col0s, cols, dtype, name):
        nj = cols // tn
        return pl.pallas_call(
            kernel,
            grid=(m // tm, nj),
            in_specs=[_row_spec(tm, d), _const_spec(ms, d), _const_spec(1, d)] + [_col_spec(d, tn, c) for c in col0s],
            out_specs=[_tile_spec(tm, tn), _side_tile_spec(ms, tn, nj)],
            out_shape=[jax.ShapeDtypeStruct((m, cols), dtype), jax.ShapeDtypeStruct((ms, cols), dtype)],
            scratch_shapes=_norm_scratch(tm, ms, d),
            compiler_params=_params(("arbitrary", "arbitrary"), nbytes),
            name=name,
        )(x, xs, g, *([w_in] * len(col0s)))

    q = call(functools.partial(_proj_kernel, q_scale), [0], qk_cols, BF16, "q_proj")
    k = call(functools.partial(_proj_kernel, None), [qk_cols], qk_cols, F32, "k_proj")
    v = call(functools.partial(_proj_kernel, None), [2 * qk_cols], attn_width, F32, "v_proj")
    c0 = 2 * qk_cols + attn_width
    glu = call(_glu_proj_kernel, [c0, c0 + conv_width], conv_width, F32, "glu_proj")
    return q, k, v, glu


def _split_maps(q):
    lane = lax.broadcasted_iota(jnp.int32, q.shape, 1)
    zero = jnp.zeros_like(q)
    return jnp.concatenate([jnp.where(lane < HEAD_DIM_QK, q, zero), jnp.where(lane >= HEAD_DIM_QK, q, zero)], axis=0)


def _subln(o, g, post_scale):
    ms = jnp.mean(o * o, axis=-1, keepdims=True)
    return o * lax.rsqrt(ms + RMS_EPS) * g * post_scale


def _prompt_attn_kernel(post_scale, tq, n_heads, n_groups, sc_ref, q_ref, k_ref, v_ref, g_ref, o_ref, qq_ref, k16_ref,
                        vt_ref, *stat_refs):
    h = pl.program_id(1)
    qi = pl.program_id(2)
    slope = sc_ref[h]
    lam = sc_ref[n_heads]
    gw = 2 * tq // n_groups
    groups = [stat_refs[3 * g:3 * g + 3] for g in range(n_groups)]

    @pl.when(qi == 0)
    def _():
        k16_ref[...] = k_ref[...].astype(BF16)
        vt_ref[...] = v_ref[...].T.astype(BF16)

    qq_ref[...] = _split_maps(q_ref[...])
    for m_ref, l_ref, acc_ref in groups:
        m_ref[...] = jnp.full(m_ref.shape, NEG_INF, F32)
        l_ref[...] = jnp.zeros(l_ref.shape, F32)
        acc_ref[...] = jnp.zeros(acc_ref.shape, F32)

    def block(ki, masked):
        start = pl.multiple_of(ki * tq, tq)
        k = k16_ref[pl.ds(start, tq), :]
        vt = vt_ref[:, pl.ds(start, tq)]
        key = lax.broadcasted_iota(jnp.int32, (tq, LANES), 0)
        kbias = slope * ((ki - qi) * tq + key).astype(F32)
        kbias = jnp.concatenate([kbias] * (gw // LANES), axis=1)
        for g, (m_ref, l_ref, acc_ref) in enumerate(groups):
            qg = qq_ref[g * gw:(g + 1) * gw, :]
            s = lax.dot_general(k, qg, (((1,), (1,)), ((), ())), preferred_element_type=F32) + kbias
            if masked:
                keyf = lax.broadcasted_iota(jnp.int32, (tq, gw), 0)
                query = (g * gw) % tq + lax.broadcasted_iota(jnp.int32, (tq, gw), 1)
                s = jnp.where(keyf <= query, s, NEG_INF)
            m_prev = m_ref[...]
            m_new = jnp.maximum(m_prev, jnp.max(s, axis=0, keepdims=True))
            alpha = jnp.exp2(m_prev - m_new)
            p = jnp.exp2(s - m_new)
            l_ref[...] = alpha * l_ref[...] + jnp.sum(p, axis=0, keepdims=True)
            acc_ref[...] = alpha * acc_ref[...] + jnp.dot(vt, p.astype(BF16), preferred_element_type=F32)
            m_ref[...] = m_new

    def body(ki, carry):
        block(ki, False)
        return carry

    lax.fori_loop(0, qi, body, 0)
    block(qi, True)
    o = jnp.concatenate([acc_ref[...] / l_ref[...] for _, l_ref, acc_ref in groups], axis=1)
    o = o[:, :tq] - lam * o[:, tq:]
    ms = jnp.mean(o * o, axis=0, keepdims=True)
    o = o * (lax.rsqrt(ms + RMS_EPS) * post_scale)
    o_ref[...] = (o.T * g_ref[...]).astype(o_ref.dtype)


def _prompt_attn(q16, k32, v32, g_subln, scal, batch, seq, n_heads, post_scale, tq):
    nq = seq // tq
    gw = 2 * tq // ATTN_COL_GROUPS
    grid_spec = pltpu.PrefetchScalarGridSpec(
        num_scalar_prefetch=1,
        grid=(batch, n_heads, nq),
        in_specs=[
            pl.BlockSpec((tq, HEAD_DIM_V), lambda b, h, i, sc: (b * nq + i, h)),
            pl.BlockSpec((seq, HEAD_DIM_V), lambda b, h, i, sc: (b, h)),
            pl.BlockSpec((seq, HEAD_DIM_V), lambda b, h, i, sc: (b, h)),
            pl.BlockSpec((1, HEAD_DIM_V), lambda b, h, i, sc: (0, 0)),
        ],
        out_specs=pl.BlockSpec((tq, HEAD_DIM_V), lambda b, h, i, sc: (b * nq + i, h)),
        scratch_shapes=[
            pltpu.VMEM((2 * tq, HEAD_DIM_V), BF16),
            pltpu.VMEM((seq, HEAD_DIM_V), BF16),
            pltpu.VMEM((HEAD_DIM_V, seq), BF16),
        ] + [pltpu.VMEM((1, gw), F32), pltpu.VMEM((1, gw), F32), pltpu.VMEM((HEAD_DIM_V, gw), F32)] * ATTN_COL_GROUPS,
    )
    nbytes = 4 * seq * HEAD_DIM_V * 4 + 2 * seq * HEAD_DIM_V * 2 + 8 * 2 * tq * tq * 4
    return pl.pallas_call(
        functools.partial(_prompt_attn_kernel, post_scale, tq, n_heads, ATTN_COL_GROUPS),
        grid_spec=grid_spec,
        out_shape=jax.ShapeDtypeStruct((batch * seq, n_heads * HEAD_DIM_V), BF16),
        compiler_params=_params(("parallel", "parallel", "arbitrary"), nbytes),
        name="prompt_attn",
    )(scal, q16, k32, v32, g_subln)


def _decode_attn_kernel(post_scale, n_heads, t_new, past_len, grp, pt_ref, sc_ref, q_ref, *refs):
    k_refs, v_refs = refs[:grp], refs[grp:2 * grp]
    kn_ref, vn_ref, g_ref, o_ref, bias_ref, slope_ref, m_ref, l_ref, acc_ref = refs[2 * grp:]
    p = pl.program_id(1)
    rows = 2 * n_heads * t_new
    rows_per_head = 2 * t_new
    cols = PAGE_SIZE * n_heads

    def head_geometry(ncols):
        row = lax.broadcasted_iota(jnp.int32, (rows, ncols), 0)
        col = lax.broadcasted_iota(jnp.int32, (rows, ncols), 1)
        rhead = row // rows_per_head
        slope = jnp.zeros((rows, ncols), F32)
        for hh in range(n_heads):
            slope = jnp.where(rhead == hh, sc_ref[hh], slope)
        return row, col, rhead == col % n_heads, col // n_heads, slope

    @pl.when(p == 0)
    def _():
        m_ref[...] = jnp.full(m_ref.shape, NEG_INF, F32)
        l_ref[...] = jnp.zeros(l_ref.shape, F32)
        acc_ref[...] = jnp.zeros(acc_ref.shape, F32)
        _, _, same_head, tok, slope = head_geometry(cols)
        bias_ref[...] = jnp.where(same_head, slope * tok.astype(F32), NEG_INF)
        slope_ref[...] = slope[:, 0:1]

    q = q_ref[...]

    def update(blocks):
        scores = [lax.dot_general(q, k2, (((1,), (1,)), ((), ())), preferred_element_type=F32) + bias
                  for k2, _, bias, _ in blocks]
        m_prev = m_ref[...]
        m_new = m_prev
        for s, (_, _, _, shift) in zip(scores, blocks):
            m_new = jnp.maximum(m_new, jnp.max(s, axis=-1, keepdims=True) + shift)
        alpha = jnp.exp2(m_prev - m_new)
        l_new = alpha * l_ref[...]
        acc = alpha * acc_ref[...]
        for s, (_, v2, _, shift) in zip(scores, blocks):
            pr = jnp.exp2(s - (m_new - shift))
            l_new = l_new + jnp.sum(pr, axis=-1, keepdims=True)
            acc = acc + jnp.dot(pr.astype(BF16), v2, preferred_element_type=F32)
        l_ref[...] = l_new
        acc_ref[...] = acc
        m_ref[...] = m_new

    blocks = []
    for g in range(grp):
        k2 = k_refs[g][...].reshape(cols, HEAD_DIM_V).astype(BF16)
        v2 = v_refs[g][...].reshape(cols, HEAD_DIM_V).astype(BF16)
        page_start = ((p * grp + g) * PAGE_SIZE - past_len).astype(F32)
        blocks.append((k2, v2, bias_ref[...], slope_ref[...] * page_start))
    update(blocks)

    @pl.when(p == pl.num_programs(1) - 1)
    def _():
        n_new = t_new * n_heads
        ncols = pl.cdiv(n_new, LANES) * LANES
        pad = jnp.zeros((ncols - n_new, HEAD_DIM_V), F32)
        kn = jnp.concatenate([kn_ref[...].reshape(n_new, HEAD_DIM_V), pad], axis=0).astype(BF16)
        vn = jnp.concatenate([vn_ref[...].reshape(n_new, HEAD_DIM_V), pad], axis=0).astype(BF16)
        row, col, same_head, tok, slope = head_geometry(ncols)
        visible = jnp.where(same_head, tok, ncols) <= row % t_new
        update([(kn, vn, jnp.where(visible, slope * tok.astype(F32), NEG_INF), 0.0)])
        o = acc_ref[...] / l_ref[...]
        lam = sc_ref[n_heads]
        gain = g_ref[...]
        outs = []
        for hh in range(n_heads):
            r0 = hh * rows_per_head
            oh = o[r0:r0 + t_new] - lam * o[r0 + t_new:r0 + 2 * t_new]
            outs.append(_subln(oh, gain, post_scale))
        o_ref[...] = jnp.concatenate(outs, axis=1).astype(o_ref.dtype)


def _decode_attn(q2, cache_k, cache_v, layer, k_new, v_new, g_subln, page_table, scal, post_scale):
    batch, n_pages = page_table.shape
    _, t_new, n_heads, _ = k_new.shape
    rows = 2 * n_heads * t_new
    width = n_heads * HEAD_DIM_V
    past_len = n_pages * PAGE_SIZE
    grp = _pick(n_pages, DECODE_PAGES_PER_STEP)
    page_block = (None, None, PAGE_SIZE, n_heads, HEAD_DIM_V)

    def page_spec(g):
        return pl.BlockSpec(page_block, lambda b, p, pt, sc: (layer, pt[b, p * grp + g], 0, 0, 0))

    new_spec = pl.BlockSpec((None, t_new, n_heads, HEAD_DIM_V), lambda b, p, pt, sc: (b, 0, 0, 0))
    grid_spec = pltpu.PrefetchScalarGridSpec(
        num_scalar_prefetch=2,
        grid=(batch, n_pages // grp),
        in_specs=[pl.BlockSpec((None, rows, HEAD_DIM_V), lambda b, p, pt, sc: (b, 0, 0))]
        + [page_spec(g) for g in range(grp)] * 2
        + [new_spec, new_spec, pl.BlockSpec((1, HEAD_DIM_V), lambda b, p, pt, sc: (0, 0))],
        out_specs=pl.BlockSpec((None, t_new, width), lambda b, p, pt, sc: (b, 0, 0)),
        scratch_shapes=[
            pltpu.VMEM((rows, PAGE_SIZE * n_heads), F32),
            pltpu.VMEM((rows, 1), F32),
            pltpu.VMEM((rows, 1), F32),
            pltpu.VMEM((rows, 1), F32),
            pltpu.VMEM((rows, HEAD_DIM_V), F32),
        ],
    )
    page_bytes = PAGE_SIZE * width * 4
    nbytes = 4 * grp * page_bytes + 6 * rows * PAGE_SIZE * n_heads * 4
    return pl.pallas_call(
        functools.partial(_decode_attn_kernel, post_scale, n_heads, t_new, past_len, grp),
        grid_spec=grid_spec,
        out_shape=jax.ShapeDtypeStruct((batch, t_new, width), BF16),
        compiler_params=_params(("parallel", "arbitrary"), nbytes),
        name="decode_attn",
    )(page_table, scal, q2, *([cache_k] * grp), *([cache_v] * grp), k_new, v_new, g_subln)


def _conv_kernel(seq, chunk, u_ref, prev_ref, w_ref, bdw_ref, gcn_ref, bcn_ref, o_ref, new_ref, ext_ref):
    hist = CONV_KERNEL - 1
    lead = CONV_PAD - hist
    ext_ref[0:lead, :] = jnp.zeros((lead, LANES), F32)
    ext_ref[lead:CONV_PAD, :] = prev_ref[0]
    ext_ref[CONV_PAD:CONV_PAD + seq, :] = u_ref[0]
    new_ref[0] = ext_ref[lead + seq:CONV_PAD + seq, :]
    bdw = bdw_ref[...]
    gcn = gcn_ref[...]
    bcn = bcn_ref[...]

    def body(c, _):
        r0 = pl.multiple_of(c * chunk, chunk)
        parts = [None] * CONV_PARTIALS
        for j in range(CONV_KERNEL):
            term = w_ref[j:j + 1, :] * ext_ref[pl.ds(r0 + (lead + j), chunk), :]
            slot = j % CONV_PARTIALS
            parts[slot] = term if parts[slot] is None else parts[slot] + term
        y = functools.reduce(lambda a, b: a + b, parts) + bdw
        mu = jnp.mean(y, axis=-1, keepdims=True)
        yc = y - mu
        var = jnp.mean(yc * yc, axis=-1, keepdims=True)
        yn = yc * lax.rsqrt(var + LN_EPS) * gcn + bcn
        o_ref[0, pl.ds(r0, chunk), :] = (yn * _sigmoid(yn)).astype(o_ref.dtype)
        return 0

    n_chunks = seq // chunk
    lax.fori_loop(0, n_chunks, body, 0, unroll=math.gcd(n_chunks, CONV_CHUNKS_PER_ITER))


def _conv_module(u, prev, w_dw, b_dw, g_cn, b_cn):
    batch, seq, width = u.shape
    assert width // CONV_GROUPS == LANES
    chunk = _pick(seq, 64)
    hist = CONV_KERNEL - 1
    vec = pl.BlockSpec((1, LANES), lambda b, c: (0, c))
    return pl.pallas_call(
        functools.partial(_conv_kernel, seq, chunk),
        grid=(batch, width // LANES),
        in_specs=[
            pl.BlockSpec((1, seq, LANES), lambda b, c: (b, 0, c)),
            pl.BlockSpec((1, hist, LANES), lambda b, c: (b, 0, c)),
            pl.BlockSpec((CONV_KERNEL, LANES), lambda b, c: (0, c)),
            vec, vec, vec,
        ],
        out_specs=[
            pl.BlockSpec((1, seq, LANES), lambda b, c: (b, 0, c)),
            pl.BlockSpec((1, hist, LANES), lambda b, c: (b, 0, c)),
        ],
        out_shape=[jax.ShapeDtypeStruct((batch, seq, width), BF16), jax.ShapeDtypeStruct((batch, hist, width), F32)],
        scratch_shapes=[pltpu.VMEM((CONV_PAD + seq, LANES), F32)],
        compiler_params=_params(("parallel", "parallel"), 8 * (seq + CONV_PAD) * LANES * 4),
        name="conv_module",
    )(u, prev, w_dw, b_dw, g_cn, b_cn)


def _out_proj_kernel(ka, a_ref, b_ref, as_ref, bs_ref, w_ref, r_ref, rs_ref, o_ref, os_ref):
    wa = w_ref[0:ka, :].astype(BF16)
    wb = w_ref[ka:, :].astype(BF16)

    def proj(a, b):
        return jnp.dot(a, wa, preferred_element_type=F32) + jnp.dot(b, wb, preferred_element_type=F32)

    o_ref[...] = r_ref[...] + proj(a_ref[...], b_ref[...])

    @pl.when(pl.program_id(0) == 0)
    def _():
        os_ref[...] = rs_ref[...] + proj(as_ref[...], bs_ref[...])


def _out_proj(a, b, a_s, b_s, w, res, res_s, tm, tn):
    m, ka = a.shape
    ms = a_s.shape[0]
    kb = b.shape[1]
    d = w.shape[1]
    nj = d // tn
    nbytes = 2 * tm * (ka + kb) * 2 + 2 * (ka + kb) * tn * 4 + (ka + kb) * tn * 2 + 5 * tm * tn * 4
    return pl.pallas_call(
        functools.partial(_out_proj_kernel, ka),
        grid=(m // tm, nj),
        in_specs=[_row_spec(tm, ka), _row_spec(tm, kb), _const_spec(ms, ka), _const_spec(ms, kb),
                  _col_spec(ka + kb, tn), _tile_spec(tm, tn), _side_tile_spec(ms, tn, nj)],
        out_specs=[_tile_spec(tm, tn), _side_tile_spec(ms, tn, nj)],
        out_shape=[jax.ShapeDtypeStruct((m, d), F32), jax.ShapeDtypeStruct((ms, d), F32)],
        compiler_params=_params(("arbitrary", "arbitrary"), nbytes),
        name="out_proj",
    )(a, b, a_s, b_s, w, res, res_s)


def _ple_kernel(final_norm, h_ref, p_ref, hs_ref, ps_ref, g_ref, wg_ref, wp_ref, gf_ref, o_ref, os_ref):
    def ple(h, p):
        n = _rms(h, g_ref[...]).astype(BF16)
        gate = _sigmoid(jnp.dot(n, wg_ref[...], preferred_element_type=F32))
        proj = jnp.dot(p.astype(BF16), wp_ref[...], preferred_element_type=F32)
        h = h + proj * gate
        return _rms(h, gf_ref[...]) if final_norm else h

    o_ref[...] = ple(h_ref[...], p_ref[...])

    @pl.when(pl.program_id(0) == 0)
    def _():
        os_ref[...] = ple(hs_ref[...], ps_ref[...])


def _ple(h, p, h_s, p_s, g, wg, wp, g_final, final_norm, tm):
    m, d = h.shape
    ms = h_s.shape[0]
    dp = p.shape[1]
    nbytes = 4 * tm * d * 4 + 2 * d * d * 2 + 2 * dp * d * 2 + 2 * tm * dp * 4 + 4 * tm * d * 4
    row = lambda cols: pl.BlockSpec((tm, cols), lambda i: (i, 0))
    const = lambda rows, cols: pl.BlockSpec((rows, cols), lambda i: (0, 0))
    return pl.pallas_call(
        functools.partial(_ple_kernel, final_norm),
        grid=(m // tm,),
        in_specs=[row(d), row(dp), const(ms, d), const(ms, dp), const(1, d), const(d, d), const(dp, d), const(1, d)],
        out_specs=[row(d), const(ms, d)],
        out_shape=[jax.ShapeDtypeStruct((m, d), F32), jax.ShapeDtypeStruct((ms, d), F32)],
        compiler_params=_params(("arbitrary",), nbytes),
        name="ple",
    )(h, p, h_s, p_s, g, wg, wp, g_final)


def _lambda_init(layer_idx):
    return 0.8 - 0.6 * math.exp(-0.3 * layer_idx)


def kernel(x_prompt, x_sample, cache_k, cache_v, state_conv, page_table, p_prompt, p_sample, g_ffn1, w_ffn1_gate, w_ffn1_up, w_ffn1_down, g_mix, w_in, lambda_q1, lambda_k1, lambda_q2, lambda_k2, g_subln, w_dw, b_dw, g_conv_norm, b_conv_norm, w_out, g_ffn2, w_ffn2_gate, w_ffn2_up, w_ffn2_down, g_ple, w_ple_gate, w_ple_proj, g_final):
    depth = w_in.shape[0]
    batch, seq, d_model = x_prompt.shape
    dec_batch, dec_seq, _ = x_sample.shape
    n_heads = cache_k.shape[3]
    assert cache_k.shape[4] == 2 * HEAD_DIM_QK and cache_v.shape[4] == HEAD_DIM_V and cache_k.shape[2] == PAGE_SIZE
    qk_cols = n_heads * 2 * HEAD_DIM_QK
    attn_width = n_heads * HEAD_DIM_V
    conv_width = d_model - attn_width
    n_p, n_s = batch * seq, dec_batch * dec_seq
    h_p = x_prompt.reshape(n_p, d_model)
    h_s = x_sample.reshape(n_s, d_model)
    tm = _pick(n_p, TOKEN_TILE)
    tq = _pick(seq, ATTN_Q_TILE)
    slopes = jnp.asarray([LOG2E * 2.0 ** (-8.0 * (h + 1) / n_heads) for h in range(n_heads)], F32)
    row2 = lambda a: a.reshape(1, -1)
    gf = row2(g_final)

    outs = {name: [] for name in ("kp", "vp", "cp", "ks", "vs", "cs")}
    for l in range(depth):
        last = l == depth - 1
        lam_init = _lambda_init(l)
        lam = (jnp.exp(jnp.sum(lambda_q1[l] * lambda_k1[l])) - jnp.exp(jnp.sum(lambda_q2[l] * lambda_k2[l])) + lam_init)
        scal = jnp.concatenate([slopes, lam.reshape(1).astype(F32)])
        post_scale = 1.0 - lam_init
        w1g, w1u, w1d = w_ffn1_gate[l], w_ffn1_up[l], w_ffn1_down[l]
        w2g, w2u, w2d = w_ffn2_gate[l], w_ffn2_up[l], w_ffn2_down[l]
        w_pg16 = w_ple_gate[l].astype(BF16)
        w_pp16 = w_ple_proj[l].astype(BF16)
        tf = _pick(w1g.shape[1], 512)
        tn_proj = math.gcd(math.gcd(qk_cols, attn_width), math.gcd(conv_width, 512))
        g_sub = row2(g_subln[l])

        def ffn(hp, hs, g, wg, wu, wd):
            hid_p, hid_s = _ffn_up(hp, hs, row2(g), wg, wu, tm, tf)
            return _ffn_down(hid_p, hid_s, wd, hp, hs, tm, _pick(d_model, 256))

        h_p, h_s = ffn(h_p, h_s, g_ffn1[l], w1g, w1u, w1d)
        (q_p, q_s), (k_p, k_s), (v_p, v_s), (glu_p, glu_s) = _in_proj(
            h_p, h_s, row2(g_mix[l]), w_in[l], qk_cols, attn_width, conv_width, tm, tn_proj)
        conv_args = (w_dw[l], row2(b_dw[l]), row2(g_conv_norm[l]), row2(b_conv_norm[l]))

        attn_p = _prompt_attn(q_p, k_p, v_p, g_sub, scal, batch, seq, n_heads, post_scale, tq)
        zero_hist = jnp.zeros((batch, CONV_KERNEL - 1, conv_width), F32)
        conv_p, hist_p = _conv_module(glu_p.reshape(batch, seq, conv_width), zero_hist, *conv_args)
        outs["kp"].append(k_p.reshape(batch, seq, n_heads, 2 * HEAD_DIM_QK))
        outs["vp"].append(v_p.reshape(batch, seq, n_heads, HEAD_DIM_V))
        outs["cp"].append(hist_p)

        k_new = k_s.reshape(dec_batch, dec_seq, n_heads, 2 * HEAD_DIM_QK)
        v_new = v_s.reshape(dec_batch, dec_seq, n_heads, HEAD_DIM_V)
        q5 = q_s.reshape(dec_batch, dec_seq, n_heads, 2, HEAD_DIM_QK)
        q2 = jnp.einsum("bthmd,mn->bhmtnd", q5, jnp.eye(2, dtype=BF16))
        q2 = q2.reshape(dec_batch, 2 * n_heads * dec_seq, 2 * HEAD_DIM_QK)
        attn_s = _decode_attn(q2, cache_k, cache_v, l, k_new, v_new, g_sub, page_table, scal, post_scale)
        conv_s, hist_s = _conv_module(glu_s.reshape(dec_batch, dec_seq, conv_width), state_conv[l], *conv_args)
        outs["ks"].append(k_new)
        outs["vs"].append(v_new)
        outs["cs"].append(hist_s)

        h_p, h_s = _out_proj(attn_p, conv_p.reshape(n_p, conv_width), attn_s.reshape(n_s, attn_width),
                             conv_s.reshape(n_s, conv_width), w_out[l], h_p, h_s, tm, _pick(d_model, 1024))
        h_p, h_s = ffn(h_p, h_s, g_ffn2[l], w2g, w2u, w2d)
        h_p, h_s = _ple(h_p, p_prompt[l].reshape(n_p, -1), h_s, p_sample[l].reshape(n_s, -1), row2(g_ple[l]),
                        w_pg16, w_pp16, gf, last, min(tm, 512))

    y_prompt = h_p.reshape(batch, seq, d_model)
    y_sample = h_s.reshape(dec_batch, dec_seq, d_model)
    return (y_prompt, y_sample, jnp.stack(outs["kp"]), jnp.stack(outs["vp"]), jnp.stack(outs["cp"]),
            jnp.stack(outs["ks"]), jnp.stack(outs["vs"]), jnp.stack(outs["cs"]))
```

```python
import functools
import math

import jax
import jax.numpy as jnp
from jax import lax
from jax.experimental import pallas as pl
from jax.experimental.pallas import tpu as pltpu

F32 = jnp.float32
BF16 = jnp.bfloat16

HEAD_DIM_QK = 64
HEAD_DIM_V = 2 * HEAD_DIM_QK
CONV_GROUPS = 8
CONV_KERNEL = 31
PAGE_SIZE = 128
RMS_EPS = 1e-6
LN_EPS = 1e-5
NEG_INF = -1e30
LOG2E = math.log2(math.e)

V7X_VMEM_BYTES = 64 * 1024 * 1024
LANES = 128
SUBLANES = 8
CONV_PAD = 32
CONV_PARTIALS = 4
CONV_CHUNKS_PER_ITER = 4
DECODE_PAGES_PER_STEP = 16
ATTN_Q_TILE = 1024
ATTN_COL_GROUPS = 4
TOKEN_TILE = 1024


def _vmem_limit(nbytes):
    return int(min(V7X_VMEM_BYTES - 4 * 1024 * 1024, nbytes + 16 * 1024 * 1024))


def _params(sem, nbytes):
    return pltpu.CompilerParams(dimension_semantics=sem, vmem_limit_bytes=_vmem_limit(nbytes))


def _rms(x, g):
    ms = jnp.mean(x * x, axis=-1, keepdims=True)
    return x * lax.rsqrt(ms + RMS_EPS) * g


def _sigmoid(x):
    return 1.0 / (1.0 + jnp.exp(-x))


def _pick(n, pref):
    t = min(n, pref)
    assert n % t == 0, (n, t)
    return t


def _row_spec(tm, cols):
    return pl.BlockSpec((tm, cols), lambda i, j: (i, 0))


def _col_spec(rows, tn, col0=0):
    off = col0 // tn
    return pl.BlockSpec((rows, tn), lambda i, j: (0, j + off))


def _tile_spec(tm, tn):
    return pl.BlockSpec((tm, tn), lambda i, j: (i, j))


def _const_spec(rows, cols):
    return pl.BlockSpec((rows, cols), lambda i, j: (0, 0))


def _side_tile_spec(rows, tn, n_col_tiles):
    return pl.BlockSpec((rows, tn), lambda i, j: (0, jnp.where(i == 0, j, n_col_tiles - 1)))


def _stage_norm(x_ref, xs_ref, g_ref, n_ref, ns_ref):
    i, j = pl.program_id(0), pl.program_id(1)

    @pl.when(j == 0)
    def _():
        n_ref[...] = _rms(x_ref[...], g_ref[...]).astype(BF16)

    @pl.when(jnp.logical_and(i == 0, j == 0))
    def _():
        ns_ref[...] = _rms(xs_ref[...], g_ref[...]).astype(BF16)


def _norm_scratch(tm, ms, d):
    return [pltpu.VMEM((tm, d), BF16), pltpu.VMEM((ms, d), BF16)]


def _ffn_up_kernel(x_ref, xs_ref, g_ref, wg_ref, wu_ref, o_ref, os_ref, n_ref, ns_ref):
    _stage_norm(x_ref, xs_ref, g_ref, n_ref, ns_ref)

    wg = wg_ref[...].astype(BF16)
    wu = wu_ref[...].astype(BF16)

    def act(n):
        a = jnp.dot(n, wg, preferred_element_type=F32)
        u = jnp.dot(n, wu, preferred_element_type=F32)
        return (a * _sigmoid(a) * u).astype(BF16)

    o_ref[...] = act(n_ref[...])

    @pl.when(pl.program_id(0) == 0)
    def _():
        os_ref[...] = act(ns_ref[...])


def _ffn_up(x, xs, g, wg, wu, tm, tf):
    m, d = x.shape
    ms = xs.shape[0]
    f = wg.shape[1]
    nj = f // tf
    wb = wg.dtype.itemsize
    nbytes = 2 * tm * d * 4 + tm * d * 2 + 2 * 2 * d * tf * wb + 2 * d * tf * 2 + 2 * tm * tf * 2 + 3 * tm * tf * 4
    return pl.pallas_call(
        _ffn_up_kernel,
        grid=(m // tm, nj),
        in_specs=[_row_spec(tm, d), _const_spec(ms, d), _const_spec(1, d), _col_spec(d, tf), _col_spec(d, tf)],
        out_specs=[_tile_spec(tm, tf), _side_tile_spec(ms, tf, nj)],
        out_shape=[jax.ShapeDtypeStruct((m, f), BF16), jax.ShapeDtypeStruct((ms, f), BF16)],
        scratch_shapes=_norm_scratch(tm, ms, d),
        compiler_params=_params(("arbitrary", "arbitrary"), nbytes),
        name="ffn_up",
    )(x, xs, g, wg, wu)


def _ffn_down_kernel(h_ref, hs_ref, w_ref, r_ref, rs_ref, o_ref, os_ref):
    w = w_ref[...].astype(BF16)
    o_ref[...] = r_ref[...] + 0.5 * jnp.dot(h_ref[...], w, preferred_element_type=F32)

    @pl.when(pl.program_id(0) == 0)
    def _():
        os_ref[...] = rs_ref[...] + 0.5 * jnp.dot(hs_ref[...], w, preferred_element_type=F32)


def _ffn_down(hid, hid_s, wd, res, res_s, tm, tn):
    m, f = hid.shape
    ms = hid_s.shape[0]
    d = wd.shape[1]
    nj = d // tn
    nbytes = 2 * tm * f * 2 + 2 * f * tn * wd.dtype.itemsize + f * tn * 2 + 4 * tm * tn * 4 + tm * tn * 4
    return pl.pallas_call(
        _ffn_down_kernel,
        grid=(m // tm, nj),
        in_specs=[_row_spec(tm, f), _const_spec(ms, f), _col_spec(f, tn), _tile_spec(tm, tn),
                  _side_tile_spec(ms, tn, nj)],
        out_specs=[_tile_spec(tm, tn), _side_tile_spec(ms, tn, nj)],
        out_shape=[jax.ShapeDtypeStruct((m, d), F32), jax.ShapeDtypeStruct((ms, d), F32)],
        compiler_params=_params(("arbitrary", "arbitrary"), nbytes),
        name="ffn_down",
    )(hid, hid_s, wd, res, res_s)


def _proj_kernel(scale, x_ref, xs_ref, g_ref, w_ref, o_ref, os_ref, n_ref, ns_ref):
    _stage_norm(x_ref, xs_ref, g_ref, n_ref, ns_ref)

    w = w_ref[...].astype(BF16)

    def proj(n):
        u = jnp.dot(n, w, preferred_element_type=F32)
        return (u if scale is None else u * scale).astype(o_ref.dtype)

    o_ref[...] = proj(n_ref[...])

    @pl.when(pl.program_id(0) == 0)
    def _():
        os_ref[...] = proj(ns_ref[...])


def _glu_proj_kernel(x_ref, xs_ref, g_ref, wa_ref, wg_ref, o_ref, os_ref, n_ref, ns_ref):
    _stage_norm(x_ref, xs_ref, g_ref, n_ref, ns_ref)

    wa = wa_ref[...].astype(BF16)
    wg = wg_ref[...].astype(BF16)

    def glu(n):
        a = jnp.dot(n, wa, preferred_element_type=F32)
        gt = jnp.dot(n, wg, preferred_element_type=F32)
        return a * _sigmoid(gt)

    o_ref[...] = glu(n_ref[...])

    @pl.when(pl.program_id(0) == 0)
    def _():
        os_ref[...] = glu(ns_ref[...])


def _in_proj(x, xs, g, w_in, qk_cols, attn_width, conv_width, tm, tn):
    m, d = x.shape
    ms = xs.shape[0]
    q_scale = HEAD_DIM_QK ** -0.5 * LOG2E
    nbytes = 2 * tm * d * 4 + tm * d * 2 + 4 * d * tn * w_in.dtype.itemsize + 2 * d * tn * 2 + 6 * tm * tn * 4

    def call(kernel, col0s, cols, dtype, name):
        nj = cols // tn
        return pl.pallas_call(
            kernel,
            grid=(m // tm, nj),
            in_specs=[_row_spec(tm, d), _const_spec(ms, d), _const_spec(1, d)] + [_col_spec(d, tn, c) for c in col0s],
            out_specs=[_tile_spec(tm, tn), _side_tile_spec(ms, tn, nj)],
            out_shape=[jax.ShapeDtypeStruct((m, cols), dtype), jax.ShapeDtypeStruct((ms, cols), dtype)],
            scratch_shapes=_norm_scratch(tm, ms, d),
            compiler_params=_params(("arbitrary", "arbitrary"), nbytes),
            name=name,
        )(x, xs, g, *([w_in] * len(col0s)))

    q = call(functools.partial(_proj_kernel, q_scale), [0], qk_cols, BF16, "q_proj")
    k = call(functools.partial(_proj_kernel, None), [qk_cols], qk_cols, F32, "k_proj")
    v = call(functools.partial(_proj_kernel, None), [2 * qk_cols], attn_width, F32, "v_proj")
    c0 = 2 * qk_cols + attn_width
    glu = call(_glu_proj_kernel, [c0, c0 + conv_width], conv_width, F32, "glu_proj")
    return q, k, v, glu


def _split_maps(q):
    lane = lax.broadcasted_iota(jnp.int32, q.shape, 1)
    zero = jnp.zeros_like(q)
    return jnp.concatenate([jnp.where(lane < HEAD_DIM_QK, q, zero), jnp.where(lane >= HEAD_DIM_QK, q, zero)], axis=0)


def _subln(o, g, post_scale):
    ms = jnp.mean(o * o, axis=-1, keepdims=True)
    return o * lax.rsqrt(ms + RMS_EPS) * g * post_scale


def _prompt_attn_kernel(post_scale, tq, n_heads, n_groups, sc_ref, q_ref, k_ref, v_ref, g_ref, o_ref, qq_ref, k16_ref,
                        vt_ref, *stat_refs):
    h = pl.program_id(1)
    qi = pl.program_id(2)
    slope = sc_ref[h]
    lam = sc_ref[n_heads]
    gw = 2 * tq // n_groups
    groups = [stat_refs[3 * g:3 * g + 3] for g in range(n_groups)]

    @pl.when(qi == 0)
    def _():
        k16_ref[...] = k_ref[...].astype(BF16)
        vt_ref[...] = v_ref[...].T.astype(BF16)

    qq_ref[...] = _split_maps(q_ref[...])
    for m_ref, l_ref, acc_ref in groups:
        m_ref[...] = jnp.full(m_ref.shape, NEG_INF, F32)
        l_ref[...] = jnp.zeros(l_ref.shape, F32)
        acc_ref[...] = jnp.zeros(acc_ref.shape, F32)

    def block(ki, masked):
        start = pl.multiple_of(ki * tq, tq)
        k = k16_ref[pl.ds(start, tq), :]
        vt = vt_ref[:, pl.ds(start, tq)]
        key = lax.broadcasted_iota(jnp.int32, (tq, LANES), 0)
        kbias = slope * ((ki - qi) * tq + key).astype(F32)
        kbias = jnp.concatenate([kbias] * (gw // LANES), axis=1)
        for g, (m_ref, l_ref, acc_ref) in enumerate(groups):
            qg = qq_ref[g * gw:(g + 1) * gw, :]
            s = lax.dot_general(k, qg, (((1,), (1,)), ((), ())), preferred_element_type=F32) + kbias
            if masked:
                keyf = lax.broadcasted_iota(jnp.int32, (tq, gw), 0)
                query = (g * gw) % tq + lax.broadcasted_iota(jnp.int32, (tq, gw), 1)
                s = jnp.where(keyf <= query, s, NEG_INF)
            m_prev = m_ref[...]
            m_new = jnp.maximum(m_prev, jnp.max(s, axis=0, keepdims=True))
            alpha = jnp.exp2(m_prev - m_new)
            p = jnp.exp2(s - m_new)
            l_ref[...] = alpha * l_ref[...] + jnp.sum(p, axis=0, keepdims=True)
            acc_ref[...] = alpha * acc_ref[...] + jnp.dot(vt, p.astype(BF16), preferred_element_type=F32)
            m_ref[...] = m_new

    def body(ki, carry):
        block(ki, False)
        return carry

    lax.fori_loop(0, qi, body, 0)
    block(qi, True)
    o = jnp.concatenate([acc_ref[...] / l_ref[...] for _, l_ref, acc_ref in groups], axis=1)
    o = o[:, :tq] - lam * o[:, tq:]
    ms = jnp.mean(o * o, axis=0, keepdims=True)
    o = o * (lax.rsqrt(ms + RMS_EPS) * post_scale)
    o_ref[...] = (o.T * g_ref[...]).astype(o_ref.dtype)


def _prompt_attn(q16, k32, v32, g_subln, scal, batch, seq, n_heads, post_scale, tq):
    nq = seq // tq
    gw = 2 * tq // ATTN_COL_GROUPS
    grid_spec = pltpu.PrefetchScalarGridSpec(
        num_scalar_prefetch=1,
        grid=(batch, n_heads, nq),
        in_specs=[
            pl.BlockSpec((tq, HEAD_DIM_V), lambda b, h, i, sc: (b * nq + i, h)),
            pl.BlockSpec((seq, HEAD_DIM_V), lambda b, h, i, sc: (b, h)),
            pl.BlockSpec((seq, HEAD_DIM_V), lambda b, h, i, sc: (b, h)),
            pl.BlockSpec((1, HEAD_DIM_V), lambda b, h, i, sc: (0, 0)),
        ],
        out_specs=pl.BlockSpec((tq, HEAD_DIM_V), lambda b, h, i, sc: (b * nq + i, h)),
        scratch_shapes=[
            pltpu.VMEM((2 * tq, HEAD_DIM_V), BF16),
            pltpu.VMEM((seq, HEAD_DIM_V), BF16),
            pltpu.VMEM((HEAD_DIM_V, seq), BF16),
        ] + [pltpu.VMEM((1, gw), F32), pltpu.VMEM((1, gw), F32), pltpu.VMEM((HEAD_DIM_V, gw), F32)] * ATTN_COL_GROUPS,
    )
    nbytes = 4 * seq * HEAD_DIM_V * 4 + 2 * seq * HEAD_DIM_V * 2 + 8 * 2 * tq * tq * 4
    return pl.pallas_call(
        functools.partial(_prompt_attn_kernel, post_scale, tq, n_heads, ATTN_COL_GROUPS),
        grid_spec=grid_spec,
        out_shape=jax.ShapeDtypeStruct((batch * seq, n_heads * HEAD_DIM_V), BF16),
        compiler_params=_params(("parallel", "parallel", "arbitrary"), nbytes),
        name="prompt_attn",
    )(scal, q16, k32, v32, g_subln)


def _decode_attn_kernel(post_scale, n_heads, t_new, past_len, grp, pt_ref, sc_ref, q_ref, *refs):
    k_refs, v_refs = refs[:grp], refs[grp:2 * grp]
    kn_ref, vn_ref, g_ref, o_ref, bias_ref, slope_ref, m_ref, l_ref, acc_ref = refs[2 * grp:]
    p = pl.program_id(1)
    rows = 2 * n_heads * t_new
    rows_per_head = 2 * t_new
    cols = PAGE_SIZE * n_heads

    def head_geometry(ncols):
        row = lax.broadcasted_iota(jnp.int32, (rows, ncols), 0)
        col = lax.broadcasted_iota(jnp.int32, (rows, ncols), 1)
        rhead = row // rows_per_head
        slope = jnp.zeros((rows, ncols), F32)
        for hh in range(n_heads):
            slope = jnp.where(rhead == hh, sc_ref[hh], slope)
        return row, col, rhead == col % n_heads, col // n_heads, slope

    @pl.when(p == 0)
    def _():
        m_ref[...] = jnp.full(m_ref.shape, NEG_INF, F32)
        l_ref[...] = jnp.zeros(l_ref.shape, F32)
        acc_ref[...] = jnp.zeros(acc_ref.shape, F32)
        _, _, same_head, tok, slope = head_geometry(cols)
        bias_ref[...] = jnp.where(same_head, slope * tok.astype(F32), NEG_INF)
        slope_ref[...] = slope[:, 0:1]

    q = q_ref[...]

    def update(blocks):
        scores = [lax.dot_general(q, k2, (((1,), (1,)), ((), ())), preferred_element_type=F32) + bias
                  for k2, _, bias, _ in blocks]
        m_prev = m_ref[...]
        m_new = m_prev
        for s, (_, _, _, shift) in zip(scores, blocks):
            m_new = jnp.maximum(m_new, jnp.max(s, axis=-1, keepdims=True) + shift)
        alpha = jnp.exp2(m_prev - m_new)
        l_new = alpha * l_ref[...]
        acc = alpha * acc_ref[...]
        for s, (_, v2, _, shift) in zip(scores, blocks):
            pr = jnp.exp2(s - (m_new - shift))
            l_new = l_new + jnp.sum(pr, axis=-1, keepdims=True)
            acc = acc + jnp.dot(pr.astype(BF16), v2, preferred_element_type=F32)
        l_ref[...] = l_new
        acc_ref[...] = acc
        m_ref[...] = m_new

    blocks = []
    for g in range(grp):
        k2 = k_refs[g][...].reshape(cols, HEAD_DIM_V).astype(BF16)
        v2 = v_refs[g][...].reshape(cols, HEAD_DIM_V).astype(BF16)
        page_start = ((p * grp + g) * PAGE_SIZE - past_len).astype(F32)
        blocks.append((k2, v2, bias_ref[...], slope_ref[...] * page_start))
    update(blocks)

    @pl.when(p == pl.num_programs(1) - 1)
    def _():
        n_new = t_new * n_heads
        ncols = pl.cdiv(n_new, LANES) * LANES
        pad = jnp.zeros((ncols - n_new, HEAD_DIM_V), F32)
        kn = jnp.concatenate([kn_ref[...].reshape(n_new, HEAD_DIM_V), pad], axis=0).astype(BF16)
        vn = jnp.concatenate([vn_ref[...].reshape(n_new, HEAD_DIM_V), pad], axis=0).astype(BF16)
        row, col, same_head, tok, slope = head_geometry(ncols)
        visible = jnp.where(same_head, tok, ncols) <= row % t_new
        update([(kn, vn, jnp.where(visible, slope * tok.astype(F32), NEG_INF), 0.0)])
        o = acc_ref[...] / l_ref[...]
        lam = sc_ref[n_heads]
        gain = g_ref[...]
        outs = []
        for hh in range(n_heads):
            r0 = hh * rows_per_head
            oh = o[r0:r0 + t_new] - lam * o[r0 + t_new:r0 + 2 * t_new]
            outs.append(_subln(oh, gain, post_scale))
        o_ref[...] = jnp.concatenate(outs, axis=1).astype(o_ref.dtype)


def _decode_attn(q2, cache_k, cache_v, layer, k_new, v_new, g_subln, page_table, scal, post_scale):
    batch, n_pages = page_table.shape
    _, t_new, n_heads, _ = k_new.shape
    rows = 2 * n_heads * t_new
    width = n_heads * HEAD_DIM_V
    past_len = n_pages * PAGE_SIZE
    grp = _pick(n_pages, DECODE_PAGES_PER_STEP)
    page_block = (None, None, PAGE_SIZE, n_heads, HEAD_DIM_V)

    def page_spec(g):
        return pl.BlockSpec(page_block, lambda b, p, pt, sc: (layer, pt[b, p * grp + g], 0, 0, 0))

    new_spec = pl.BlockSpec((None, t_new, n_heads, HEAD_DIM_V), lambda b, p, pt, sc: (b, 0, 0, 0))
    grid_spec = pltpu.PrefetchScalarGridSpec(
        num_scalar_prefetch=2,
        grid=(batch, n_pages // grp),
        in_specs=[pl.BlockSpec((None, rows, HEAD_DIM_V), lambda b, p, pt, sc: (b, 0, 0))]
        + [page_spec(g) for g in range(grp)] * 2
        + [new_spec, new_spec, pl.BlockSpec((1, HEAD_DIM_V), lambda b, p, pt, sc: (0, 0))],
        out_specs=pl.BlockSpec((None, t_new, width), lambda b, p, pt, sc: (b, 0, 0)),
        scratch_shapes=[
            pltpu.VMEM((rows, PAGE_SIZE * n_heads), F32),
            pltpu.VMEM((rows, 1), F32),
            pltpu.VMEM((rows, 1), F32),
            pltpu.VMEM((rows, 1), F32),
            pltpu.VMEM((rows, HEAD_DIM_V), F32),
        ],
    )
    page_bytes = PAGE_SIZE * width * 4
    nbytes = 4 * grp * page_bytes + 6 * rows * PAGE_SIZE * n_heads * 4
    return pl.pallas_call(
        functools.partial(_decode_attn_kernel, post_scale, n_heads, t_new, past_len, grp),
        grid_spec=grid_spec,
        out_shape=jax.ShapeDtypeStruct((batch, t_new, width), BF16),
        compiler_params=_params(("parallel", "arbitrary"), nbytes),
        name="decode_attn",
    )(page_table, scal, q2, *([cache_k] * grp), *([cache_v] * grp), k_new, v_new, g_subln)


def _conv_kernel(seq, chunk, u_ref, prev_ref, w_ref, bdw_ref, gcn_ref, bcn_ref, o_ref, new_ref, ext_ref):
    hist = CONV_KERNEL - 1
    lead = CONV_PAD - hist
    ext_ref[0:lead, :] = jnp.zeros((lead, LANES), F32)
    ext_ref[lead:CONV_PAD, :] = prev_ref[0]
    ext_ref[CONV_PAD:CONV_PAD + seq, :] = u_ref[0]
    new_ref[0] = ext_ref[lead + seq:CONV_PAD + seq, :]
    bdw = bdw_ref[...]
    gcn = gcn_ref[...]
    bcn = bcn_ref[...]

    def body(c, _):
        r0 = pl.multiple_of(c * chunk, chunk)
        parts = [None] * CONV_PARTIALS
        for j in range(CONV_KERNEL):
            term = w_ref[j:j + 1, :] * ext_ref[pl.ds(r0 + (lead + j), chunk), :]
            slot = j % CONV_PARTIALS
            parts[slot] = term if parts[slot] is None else parts[slot] + term
        y = functools.reduce(lambda a, b: a + b, parts) + bdw
        mu = jnp.mean(y, axis=-1, keepdims=True)
        yc = y - mu
        var = jnp.mean(yc * yc, axis=-1, keepdims=True)
        yn = yc * lax.rsqrt(var + LN_EPS) * gcn + bcn
        o_ref[0, pl.ds(r0, chunk), :] = (yn * _sigmoid(yn)).astype(o_ref.dtype)
        return 0

    n_chunks = seq // chunk
    lax.fori_loop(0, n_chunks, body, 0, unroll=math.gcd(n_chunks, CONV_CHUNKS_PER_ITER))


def _conv_module(u, prev, w_dw, b_dw, g_cn, b_cn):
    batch, seq, width = u.shape
    assert width // CONV_GROUPS == LANES
    chunk = _pick(seq, 64)
    hist = CONV_KERNEL - 1
    vec = pl.BlockSpec((1, LANES), lambda b, c: (0, c))
    return pl.pallas_call(
        functools.partial(_conv_kernel, seq, chunk),
        grid=(batch, width // LANES),
        in_specs=[
            pl.BlockSpec((1, seq, LANES), lambda b, c: (b, 0, c)),
            pl.BlockSpec((1, hist, LANES), lambda b, c: (b, 0, c)),
            pl.BlockSpec((CONV_KERNEL, LANES), lambda b, c: (0, c)),
            vec, vec, vec,
        ],
        out_specs=[
            pl.BlockSpec((1, seq, LANES), lambda b, c: (b, 0, c)),
            pl.BlockSpec((1, hist, LANES), lambda b, c: (b, 0, c)),
        ],
        out_shape=[jax.ShapeDtypeStruct((batch, seq, width), BF16), jax.ShapeDtypeStruct((batch, hist, width), F32)],
        scratch_shapes=[pltpu.VMEM((CONV_PAD + seq, LANES), F32)],
        compiler_params=_params(("parallel", "parallel"), 8 * (seq + CONV_PAD) * LANES * 4),
        name="conv_module",
    )(u, prev, w_dw, b_dw, g_cn, b_cn)


def _out_proj_kernel(ka, a_ref, b_ref, as_ref, bs_ref, w_ref, r_ref, rs_ref, o_ref, os_ref):
    wa = w_ref[0:ka, :].astype(BF16)
    wb = w_ref[ka:, :].astype(BF16)

    def proj(a, b):
        return jnp.dot(a, wa, preferred_element_type=F32) + jnp.dot(b, wb, preferred_element_type=F32)

    o_ref[...] = r_ref[...] + proj(a_ref[...], b_ref[...])

    @pl.when(pl.program_id(0) == 0)
    def _():
        os_ref[...] = rs_ref[...] + proj(as_ref[...], bs_ref[...])


def _out_proj(a, b, a_s, b_s, w, res, res_s, tm, tn):
    m, ka = a.shape
    ms = a_s.shape[0]
    kb = b.shape[1]
    d = w.shape[1]
    nj = d // tn
    nbytes = 2 * tm * (ka + kb) * 2 + 2 * (ka + kb) * tn * w.dtype.itemsize + (ka + kb) * tn * 2 + 5 * tm * tn * 4
    return pl.pallas_call(
        functools.partial(_out_proj_kernel, ka),
        grid=(m // tm, nj),
        in_specs=[_row_spec(tm, ka), _row_spec(tm, kb), _const_spec(ms, ka), _const_spec(ms, kb),
                  _col_spec(ka + kb, tn), _tile_spec(tm, tn), _side_tile_spec(ms, tn, nj)],
        out_specs=[_tile_spec(tm, tn), _side_tile_spec(ms, tn, nj)],
        out_shape=[jax.ShapeDtypeStruct((m, d), F32), jax.ShapeDtypeStruct((ms, d), F32)],
        compiler_params=_params(("arbitrary", "arbitrary"), nbytes),
        name="out_proj",
    )(a, b, a_s, b_s, w, res, res_s)


def _ple_kernel(final_norm, h_ref, p_ref, hs_ref, ps_ref, g_ref, wg_ref, wp_ref, gf_ref, o_ref, os_ref):
    def ple(h, p):
        n = _rms(h, g_ref[...]).astype(BF16)
        gate = _sigmoid(jnp.dot(n, wg_ref[...], preferred_element_type=F32))
        proj = jnp.dot(p.astype(BF16), wp_ref[...], preferred_element_type=F32)
        h = h + proj * gate
        return _rms(h, gf_ref[...]) if final_norm else h

    o_ref[...] = ple(h_ref[...], p_ref[...])

    @pl.when(pl.program_id(0) == 0)
    def _():
        os_ref[...] = ple(hs_ref[...], ps_ref[...])


def _ple(h, p, h_s, p_s, g, wg, wp, g_final, final_norm, tm):
    m, d = h.shape
    ms = h_s.shape[0]
    dp = p.shape[1]
    nbytes = 4 * tm * d * 4 + 2 * d * d * 2 + 2 * dp * d * 2 + 2 * tm * dp * 4 + 4 * tm * d * 4
    row = lambda cols: pl.BlockSpec((tm, cols), lambda i: (i, 0))
    const = lambda rows, cols: pl.BlockSpec((rows, cols), lambda i: (0, 0))
    return pl.pallas_call(
        functools.partial(_ple_kernel, final_norm),
        grid=(m // tm,),
        in_specs=[row(d), row(dp), const(ms, d), const(ms, dp), const(1, d), const(d, d), const(dp, d), const(1, d)],
        out_specs=[row(d), const(ms, d)],
        out_shape=[jax.ShapeDtypeStruct((m, d), F32), jax.ShapeDtypeStruct((ms, d), F32)],
        compiler_params=_params(("arbitrary",), nbytes),
        name="ple",
    )(h, p, h_s, p_s, g, wg, wp, g_final)


def _lambda_init(layer_idx):
    return 0.8 - 0.6 * math.exp(-0.3 * layer_idx)


def kernel(x_prompt, x_sample, cache_k, cache_v, state_conv, page_table, p_prompt, p_sample, g_ffn1, w_ffn1_gate, w_ffn1_up, w_ffn1_down, g_mix, w_in, lambda_q1, lambda_k1, lambda_q2, lambda_k2, g_subln, w_dw, b_dw, g_conv_norm, b_conv_norm, w_out, g_ffn2, w_ffn2_gate, w_ffn2_up, w_ffn2_down, g_ple, w_ple_gate, w_ple_proj, g_final):
    depth = w_in.shape[0]
    batch, seq, d_model = x_prompt.shape
    dec_batch, dec_seq, _ = x_sample.shape
    n_heads = cache_k.shape[3]
    assert cache_k.shape[4] == 2 * HEAD_DIM_QK and cache_v.shape[4] == HEAD_DIM_V and cache_k.shape[2] == PAGE_SIZE
    qk_cols = n_heads * 2 * HEAD_DIM_QK
    attn_width = n_heads * HEAD_DIM_V
    conv_width = d_model - attn_width
    n_p, n_s = batch * seq, dec_batch * dec_seq
    h_p = x_prompt.reshape(n_p, d_model)
    h_s = x_sample.reshape(n_s, d_model)
    tm = _pick(n_p, TOKEN_TILE)
    tq = _pick(seq, ATTN_Q_TILE)
    slopes = jnp.asarray([LOG2E * 2.0 ** (-8.0 * (h + 1) / n_heads) for h in range(n_heads)], F32)
    row2 = lambda a: a.reshape(1, -1)
    gf = row2(g_final)

    outs = {name: [] for name in ("kp", "vp", "cp", "ks", "vs", "cs")}
    for l in range(depth):
        last = l == depth - 1
        lam_init = _lambda_init(l)
        lam = (jnp.exp(jnp.sum(lambda_q1[l] * lambda_k1[l])) - jnp.exp(jnp.sum(lambda_q2[l] * lambda_k2[l])) + lam_init)
        scal = jnp.concatenate([slopes, lam.reshape(1).astype(F32)])
        post_scale = 1.0 - lam_init
        w1g, w1u, w1d = w_ffn1_gate[l], w_ffn1_up[l], w_ffn1_down[l].astype(BF16)
        w2g, w2u, w2d = w_ffn2_gate[l], w_ffn2_up[l], w_ffn2_down[l].astype(BF16)
        w_in16 = w_in[l].astype(BF16)
        w_out16 = w_out[l].astype(BF16)
        w_pg16 = w_ple_gate[l].astype(BF16)
        w_pp16 = w_ple_proj[l].astype(BF16)
        tf = _pick(w1g.shape[1], 512)
        tn_proj = math.gcd(math.gcd(qk_cols, attn_width), math.gcd(conv_width, 1024))
        g_sub = row2(g_subln[l])

        def ffn(hp, hs, g, wg, wu, wd):
            hid_p, hid_s = _ffn_up(hp, hs, row2(g), wg, wu, tm, tf)
            return _ffn_down(hid_p, hid_s, wd, hp, hs, tm, _pick(d_model, 512))

        h_p, h_s = ffn(h_p, h_s, g_ffn1[l], w1g, w1u, w1d)
        (q_p, q_s), (k_p, k_s), (v_p, v_s), (glu_p, glu_s) = _in_proj(
            h_p, h_s, row2(g_mix[l]), w_in16, qk_cols, attn_width, conv_width, tm, tn_proj)
        conv_args = (w_dw[l], row2(b_dw[l]), row2(g_conv_norm[l]), row2(b_conv_norm[l]))

        attn_p = _prompt_attn(q_p, k_p, v_p, g_sub, scal, batch, seq, n_heads, post_scale, tq)
        zero_hist = jnp.zeros((batch, CONV_KERNEL - 1, conv_width), F32)
        conv_p, hist_p = _conv_module(glu_p.reshape(batch, seq, conv_width), zero_hist, *conv_args)
        outs["kp"].append(k_p.reshape(batch, seq, n_heads, 2 * HEAD_DIM_QK))
        outs["vp"].append(v_p.reshape(batch, seq, n_heads, HEAD_DIM_V))
        outs["cp"].append(hist_p)

        k_new = k_s.reshape(dec_batch, dec_seq, n_heads, 2 * HEAD_DIM_QK)
        v_new = v_s.reshape(dec_batch, dec_seq, n_heads, HEAD_DIM_V)
        q5 = q_s.reshape(dec_batch, dec_seq, n_heads, 2, HEAD_DIM_QK)
        q2 = jnp.einsum("bthmd,mn->bhmtnd", q5, jnp.eye(2, dtype=BF16))
        q2 = q2.reshape(dec_batch, 2 * n_heads * dec_seq, 2 * HEAD_DIM_QK)
        attn_s = _decode_attn(q2, cache_k, cache_v, l, k_new, v_new, g_sub, page_table, scal, post_scale)
        conv_s, hist_s = _conv_module(glu_s.reshape(dec_batch, dec_seq, conv_width), state_conv[l], *conv_args)
        outs["ks"].append(k_new)
        outs["vs"].append(v_new)
        outs["cs"].append(hist_s)

        h_p, h_s = _out_proj(attn_p, conv_p.reshape(n_p, conv_width), attn_s.reshape(n_s, attn_width),
                             conv_s.reshape(n_s, conv_width), w_out16, h_p, h_s, tm, _pick(d_model, 1024))
        h_p, h_s = ffn(h_p, h_s, g_ffn2[l], w2g, w2u, w2d)
        h_p, h_s = _ple(h_p, p_prompt[l].reshape(n_p, -1), h_s, p_sample[l].reshape(n_s, -1), row2(g_ple[l]),
                        w_pg16, w_pp16, gf, last, min(tm, 512))

    y_prompt = h_p.reshape(batch, seq, d_model)
    y_sample = h_s.reshape(dec_batch, dec_seq, d_model)
    return (y_prompt, y_sample, jnp.stack(outs["kp"]), jnp.stack(outs["vp"]), jnp.stack(outs["cp"]),
            jnp.stack(outs["ks"]), jnp.stack(outs["vs"]), jnp.stack(outs["cs"]))
```

```python
import functools
import math

import jax
import jax.numpy as jnp
from jax import lax
from jax.experimental import pallas as pl
from jax.experimental.pallas import tpu as pltpu

F32 = jnp.float32
BF16 = jnp.bfloat16

HEAD_DIM_QK = 64
HEAD_DIM_V = 2 * HEAD_DIM_QK
CONV_GROUPS = 8
CONV_KERNEL = 31
PAGE_SIZE = 128
RMS_EPS = 1e-6
LN_EPS = 1e-5
NEG_INF = -1e30
LOG2E = math.log2(math.e)

V7X_VMEM_BYTES = 64 * 1024 * 1024
LANES = 128
SUBLANES = 8
CONV_PAD = 32
CONV_PARTIALS = 4
CONV_CHUNKS_PER_ITER = 4
DECODE_PAGES_PER_STEP = 16
ATTN_Q_TILE = 1024
ATTN_COL_GROUPS = 4
TOKEN_TILE = 1024


def _vmem_limit(nbytes):
    return int(min(V7X_VMEM_BYTES - 4 * 1024 * 1024, nbytes + 16 * 1024 * 1024))


def _params(sem, nbytes):
    return pltpu.CompilerParams(dimension_semantics=sem, vmem_limit_bytes=_vmem_limit(nbytes))


def _rms(x, g):
    ms = jnp.mean(x * x, axis=-1, keepdims=True)
    return x * lax.rsqrt(ms + RMS_EPS) * g


def _sigmoid(x):
    return 1.0 / (1.0 + jnp.exp(-x))


def _pick(n, pref):
    t = min(n, pref)
    assert n % t == 0, (n, t)
    return t


def _row_spec(tm, cols):
    return pl.BlockSpec((tm, cols), lambda i, j: (i, 0))


def _col_spec(rows, tn, col0=0):
    off = col0 // tn
    return pl.BlockSpec((rows, tn), lambda i, j: (0, j + off))


def _tile_spec(tm, tn):
    return pl.BlockSpec((tm, tn), lambda i, j: (i, j))


def _const_spec(rows, cols):
    return pl.BlockSpec((rows, cols), lambda i, j: (0, 0))


def _side_tile_spec(rows, tn, n_col_tiles):
    return pl.BlockSpec((rows, tn), lambda i, j: (0, jnp.where(i == 0, j, n_col_tiles - 1)))


def _stage_norm(x_ref, xs_ref, g_ref, n_ref, ns_ref):
    i, j = pl.program_id(0), pl.program_id(1)

    @pl.when(j == 0)
    def _():
        n_ref[...] = _rms(x_ref[...], g_ref[...]).astype(BF16)

    @pl.when(jnp.logical_and(i == 0, j == 0))
    def _():
        ns_ref[...] = _rms(xs_ref[...], g_ref[...]).astype(BF16)


def _norm_scratch(tm, ms, d):
    return [pltpu.VMEM((tm, d), BF16), pltpu.VMEM((ms, d), BF16)]


def _ffn_up_kernel(x_ref, xs_ref, g_ref, wg_ref, wu_ref, o_ref, os_ref, n_ref, ns_ref):
    _stage_norm(x_ref, xs_ref, g_ref, n_ref, ns_ref)

    wg = wg_ref[...].astype(BF16)
    wu = wu_ref[...].astype(BF16)

    def act(n):
        a = jnp.dot(n, wg, preferred_element_type=F32)
        u = jnp.dot(n, wu, preferred_element_type=F32)
        return (a * _sigmoid(a) * u).astype(BF16)

    o_ref[...] = act(n_ref[...])

    @pl.when(pl.program_id(0) == 0)
    def _():
        os_ref[...] = act(ns_ref[...])


def _ffn_up(x, xs, g, wg, wu, tm, tf):
    m, d = x.shape
    ms = xs.shape[0]
    f = wg.shape[1]
    nj = f // tf
    wb = wg.dtype.itemsize
    nbytes = 2 * tm * d * 4 + tm * d * 2 + 2 * 2 * d * tf * wb + 2 * d * tf * 2 + 2 * tm * tf * 2 + 3 * tm * tf * 4
    return pl.pallas_call(
        _ffn_up_kernel,
        grid=(m // tm, nj),
        in_specs=[_row_spec(tm, d), _const_spec(ms, d), _const_spec(1, d), _col_spec(d, tf), _col_spec(d, tf)],
        out_specs=[_tile_spec(tm, tf), _side_tile_spec(ms, tf, nj)],
        out_shape=[jax.ShapeDtypeStruct((m, f), BF16), jax.ShapeDtypeStruct((ms, f), BF16)],
        scratch_shapes=_norm_scratch(tm, ms, d),
        compiler_params=_params(("arbitrary", "arbitrary"), nbytes),
        name="ffn_up",
    )(x, xs, g, wg, wu)


def _ffn_down_kernel(h_ref, hs_ref, w_ref, r_ref, rs_ref, o_ref, os_ref):
    w = w_ref[...].astype(BF16)
    o_ref[...] = r_ref[...] + 0.5 * jnp.dot(h_ref[...], w, preferred_element_type=F32)

    @pl.when(pl.program_id(0) == 0)
    def _():
        os_ref[...] = rs_ref[...] + 0.5 * jnp.dot(hs_ref[...], w, preferred_element_type=F32)


def _ffn_down(hid, hid_s, wd, res, res_s, tm, tn):
    m, f = hid.shape
    ms = hid_s.shape[0]
    d = wd.shape[1]
    nj = d // tn
    nbytes = 2 * tm * f * 2 + 2 * f * tn * wd.dtype.itemsize + f * tn * 2 + 4 * tm * tn * 4 + tm * tn * 4
    return pl.pallas_call(
        _ffn_down_kernel,
        grid=(m // tm, nj),
        in_specs=[_row_spec(tm, f), _const_spec(ms, f), _col_spec(f, tn), _tile_spec(tm, tn),
                  _side_tile_spec(ms, tn, nj)],
        out_specs=[_tile_spec(tm, tn), _side_tile_spec(ms, tn, nj)],
        out_shape=[jax.ShapeDtypeStruct((m, d), F32), jax.ShapeDtypeStruct((ms, d), F32)],
        compiler_params=_params(("arbitrary", "arbitrary"), nbytes),
        name="ffn_down",
    )(hid, hid_s, wd, res, res_s)


def _qkv_proj_kernel(q_scale, x_ref, xs_ref, g_ref, w_ref, q_ref, k_ref, v_ref, qs_ref, ks_ref, vs_ref, n_ref, ns_ref):
    _stage_norm(x_ref, xs_ref, g_ref, n_ref, ns_ref)
    i, j = pl.program_id(0), pl.program_id(1)
    w = w_ref[...].astype(BF16)

    def emit(n, outs):
        u = jnp.dot(n, w, preferred_element_type=F32)
        for jj, o_ref in enumerate(outs):
            @pl.when(j == jj)
            def _():
                o_ref[...] = (u * q_scale if jj == 0 else u).astype(o_ref.dtype)

    emit(n_ref[...], (q_ref, k_ref, v_ref))

    @pl.when(i == 0)
    def _():
        emit(ns_ref[...], (qs_ref, ks_ref, vs_ref))


def _glu_proj_kernel(x_ref, xs_ref, g_ref, wa_ref, wg_ref, o_ref, os_ref, n_ref, ns_ref):
    _stage_norm(x_ref, xs_ref, g_ref, n_ref, ns_ref)

    wa = wa_ref[...].astype(BF16)
    wg = wg_ref[...].astype(BF16)

    def glu(n):
        a = jnp.dot(n, wa, preferred_element_type=F32)
        gt = jnp.dot(n, wg, preferred_element_type=F32)
        return a * _sigmoid(gt)

    o_ref[...] = glu(n_ref[...])

    @pl.when(pl.program_id(0) == 0)
    def _():
        os_ref[...] = glu(ns_ref[...])


def _in_proj(x, xs, g, w_in, qk_cols, attn_width, conv_width, tm, tn):
    m, d = x.shape
    ms = xs.shape[0]
    q_scale = HEAD_DIM_QK ** -0.5 * LOG2E
    nbytes = 2 * tm * d * 4 + tm * d * 2 + 4 * d * tn * w_in.dtype.itemsize + 2 * d * tn * 2 + 6 * tm * tn * 4

    def call(kernel, col0s, cols, dtype, name):
        nj = cols // tn
        return pl.pallas_call(
            kernel,
            grid=(m // tm, nj),
            in_specs=[_row_spec(tm, d), _const_spec(ms, d), _const_spec(1, d)] + [_col_spec(d, tn, c) for c in col0s],
            out_specs=[_tile_spec(tm, tn), _side_tile_spec(ms, tn, nj)],
            out_shape=[jax.ShapeDtypeStruct((m, cols), dtype), jax.ShapeDtypeStruct((ms, cols), dtype)],
            scratch_shapes=_norm_scratch(tm, ms, d),
            compiler_params=_params(("arbitrary", "arbitrary"), nbytes),
            name=name,
        )(x, xs, g, *([w_in] * len(col0s)))

    assert qk_cols == attn_width
    width = qk_cols
    whole = pl.BlockSpec((tm, width), lambda i, j: (i, 0))
    whole_side = _const_spec(ms, width)
    dtypes = (BF16, F32, F32)
    qkv_bytes = 2 * tm * d * 4 + tm * d * 2 + 2 * d * width * w_in.dtype.itemsize + (2 * 10 + 4) * tm * width
    qkv = pl.pallas_call(
        functools.partial(_qkv_proj_kernel, q_scale),
        grid=(m // tm, 3),
        in_specs=[_row_spec(tm, d), _const_spec(ms, d), _const_spec(1, d), _col_spec(d, width)],
        out_specs=[whole] * 3 + [whole_side] * 3,
        out_shape=[jax.ShapeDtypeStruct((m, width), t) for t in dtypes]
        + [jax.ShapeDtypeStruct((ms, width), t) for t in dtypes],
        scratch_shapes=_norm_scratch(tm, ms, d),
        compiler_params=_params(("arbitrary", "arbitrary"), qkv_bytes),
        name="qkv_proj",
    )(x, xs, g, w_in)
    c0 = 2 * qk_cols + attn_width
    glu = call(_glu_proj_kernel, [c0, c0 + conv_width], conv_width, F32, "glu_proj")
    return (qkv[0], qkv[3]), (qkv[1], qkv[4]), (qkv[2], qkv[5]), glu


def _split_maps(q):
    lane = lax.broadcasted_iota(jnp.int32, q.shape, 1)
    zero = jnp.zeros_like(q)
    return jnp.concatenate([jnp.where(lane < HEAD_DIM_QK, q, zero), jnp.where(lane >= HEAD_DIM_QK, q, zero)], axis=0)


def _subln(o, g, post_scale):
    ms = jnp.mean(o * o, axis=-1, keepdims=True)
    return o * lax.rsqrt(ms + RMS_EPS) * g * post_scale


def _prompt_attn_kernel(post_scale, tq, n_heads, n_groups, sc_ref, q_ref, k_ref, v_ref, g_ref, o_ref, qq_ref, k16_ref,
                        vt_ref, *stat_refs):
    h = pl.program_id(1)
    qi = pl.program_id(2)
    slope = sc_ref[h]
    lam = sc_ref[n_heads]
    gw = 2 * tq // n_groups
    assert tq % gw == 0
    groups = [stat_refs[3 * g:3 * g + 3] for g in range(n_groups)]

    @pl.when(qi == 0)
    def _():
        k16_ref[...] = k_ref[...].astype(BF16)
        vt_ref[...] = v_ref[...].T.astype(BF16)

    qq_ref[...] = _split_maps(q_ref[...])
    for m_ref, l_ref, acc_ref in groups:
        m_ref[...] = jnp.full(m_ref.shape, NEG_INF, F32)
        l_ref[...] = jnp.zeros(l_ref.shape, F32)
        acc_ref[...] = jnp.zeros(acc_ref.shape, F32)

    def block(ki, masked):
        start = pl.multiple_of(ki * tq, tq)
        k = k16_ref[pl.ds(start, tq), :]
        vt = vt_ref[:, pl.ds(start, tq)]
        key = lax.broadcasted_iota(jnp.int32, (tq, LANES), 0)
        kbias = slope * ((ki - qi) * tq + key).astype(F32)
        kbias = jnp.concatenate([kbias] * (gw // LANES), axis=1)
        for g, (m_ref, l_ref, acc_ref) in enumerate(groups):
            qg = qq_ref[g * gw:(g + 1) * gw, :]
            s = lax.dot_general(k, qg, (((1,), (1,)), ((), ())), preferred_element_type=F32) + kbias
            if masked:
                keyf = lax.broadcasted_iota(jnp.int32, (tq, gw), 0)
                query = (g * gw) % tq + lax.broadcasted_iota(jnp.int32, (tq, gw), 1)
                s = jnp.where(keyf <= query, s, NEG_INF)
            m_prev = m_ref[...]
            m_new = jnp.maximum(m_prev, jnp.max(s, axis=0, keepdims=True))
            alpha = jnp.exp2(m_prev - m_new)
            p = jnp.exp2(s - m_new)
            l_ref[...] = alpha * l_ref[...] + jnp.sum(p, axis=0, keepdims=True)
            acc_ref[...] = alpha * acc_ref[...] + jnp.dot(vt, p.astype(BF16), preferred_element_type=F32)
            m_ref[...] = m_new

    def body(ki, carry):
        block(ki, False)
        return carry

    lax.fori_loop(0, qi, body, 0)
    block(qi, True)
    o = jnp.concatenate([acc_ref[...] / l_ref[...] for _, l_ref, acc_ref in groups], axis=1)
    o = o[:, :tq] - lam * o[:, tq:]
    ms = jnp.mean(o * o, axis=0, keepdims=True)
    o = o * (lax.rsqrt(ms + RMS_EPS) * post_scale)
    o_ref[...] = (o.T * g_ref[...]).astype(o_ref.dtype)


def _prompt_attn(q16, k32, v32, g_subln, scal, batch, seq, n_heads, post_scale, tq):
    nq = seq // tq
    gw = 2 * tq // ATTN_COL_GROUPS
    grid_spec = pltpu.PrefetchScalarGridSpec(
        num_scalar_prefetch=1,
        grid=(batch, n_heads, nq),
        in_specs=[
            pl.BlockSpec((tq, HEAD_DIM_V), lambda b, h, i, sc: (b * nq + i, h)),
            pl.BlockSpec((seq, HEAD_DIM_V), lambda b, h, i, sc: (b, h)),
            pl.BlockSpec((seq, HEAD_DIM_V), lambda b, h, i, sc: (b, h)),
            pl.BlockSpec((1, HEAD_DIM_V), lambda b, h, i, sc: (0, 0)),
        ],
        out_specs=pl.BlockSpec((tq, HEAD_DIM_V), lambda b, h, i, sc: (b * nq + i, h)),
        scratch_shapes=[
            pltpu.VMEM((2 * tq, HEAD_DIM_V), BF16),
            pltpu.VMEM((seq, HEAD_DIM_V), BF16),
            pltpu.VMEM((HEAD_DIM_V, seq), BF16),
        ] + [pltpu.VMEM((1, gw), F32), pltpu.VMEM((1, gw), F32), pltpu.VMEM((HEAD_DIM_V, gw), F32)] * ATTN_COL_GROUPS,
    )
    nbytes = 4 * seq * HEAD_DIM_V * 4 + 2 * seq * HEAD_DIM_V * 2 + 8 * 2 * tq * tq * 4
    return pl.pallas_call(
        functools.partial(_prompt_attn_kernel, post_scale, tq, n_heads, ATTN_COL_GROUPS),
        grid_spec=grid_spec,
        out_shape=jax.ShapeDtypeStruct((batch * seq, n_heads * HEAD_DIM_V), BF16),
        compiler_params=_params(("parallel", "parallel", "arbitrary"), nbytes),
        name="prompt_attn",
    )(scal, q16, k32, v32, g_subln)


def _decode_attn_kernel(post_scale, n_heads, t_new, past_len, grp, pt_ref, sc_ref, q_ref, *refs):
    k_refs, v_refs = refs[:grp], refs[grp:2 * grp]
    kn_ref, vn_ref, g_ref, o_ref, bias_ref, slope_ref, m_ref, l_ref, acc_ref = refs[2 * grp:]
    p = pl.program_id(1)
    rows = 2 * n_heads * t_new
    rows_per_head = 2 * t_new
    cols = PAGE_SIZE * n_heads

    def head_geometry(ncols):
        row = lax.broadcasted_iota(jnp.int32, (rows, ncols), 0)
        col = lax.broadcasted_iota(jnp.int32, (rows, ncols), 1)
        rhead = row // rows_per_head
        slope = jnp.zeros((rows, ncols), F32)
        for hh in range(n_heads):
            slope = jnp.where(rhead == hh, sc_ref[hh], slope)
        return row, col, rhead == col % n_heads, col // n_heads, slope

    @pl.when(p == 0)
    def _():
        m_ref[...] = jnp.full(m_ref.shape, NEG_INF, F32)
        l_ref[...] = jnp.zeros(l_ref.shape, F32)
        acc_ref[...] = jnp.zeros(acc_ref.shape, F32)
        _, _, same_head, tok, slope = head_geometry(cols)
        bias_ref[...] = jnp.where(same_head, slope * tok.astype(F32), NEG_INF)
        slope_ref[...] = slope[:, 0:1]

    q = q_ref[...]

    def update(blocks):
        scores = [lax.dot_general(q, k2, (((1,), (1,)), ((), ())), preferred_element_type=F32) + bias
                  for k2, _, bias, _ in blocks]
        m_prev = m_ref[...]
        m_new = m_prev
        for s, (_, _, _, shift) in zip(scores, blocks):
            m_new = jnp.maximum(m_new, jnp.max(s, axis=-1, keepdims=True) + shift)
        alpha = jnp.exp2(m_prev - m_new)
        l_new = alpha * l_ref[...]
        acc = alpha * acc_ref[...]
        for s, (_, v2, _, shift) in zip(scores, blocks):
            pr = jnp.exp2(s - (m_new - shift))
            l_new = l_new + jnp.sum(pr, axis=-1, keepdims=True)
            acc = acc + jnp.dot(pr.astype(BF16), v2, preferred_element_type=F32)
        l_ref[...] = l_new
        acc_ref[...] = acc
        m_ref[...] = m_new

    blocks = []
    for g in range(grp):
        k2 = k_refs[g][...].reshape(cols, HEAD_DIM_V).astype(BF16)
        v2 = v_refs[g][...].reshape(cols, HEAD_DIM_V).astype(BF16)
        page_start = ((p * grp + g) * PAGE_SIZE - past_len).astype(F32)
        blocks.append((k2, v2, bias_ref[...], slope_ref[...] * page_start))
    update(blocks)

    @pl.when(p == pl.num_programs(1) - 1)
    def _():
        n_new = t_new * n_heads
        ncols = pl.cdiv(n_new, LANES) * LANES
        pad = jnp.zeros((ncols - n_new, HEAD_DIM_V), F32)
        kn = jnp.concatenate([kn_ref[...].reshape(n_new, HEAD_DIM_V), pad], axis=0).astype(BF16)
        vn = jnp.concatenate([vn_ref[...].reshape(n_new, HEAD_DIM_V), pad], axis=0).astype(BF16)
        row, col, same_head, tok, slope = head_geometry(ncols)
        visible = jnp.where(same_head, tok, ncols) <= row % t_new
        update([(kn, vn, jnp.where(visible, slope * tok.astype(F32), NEG_INF), 0.0)])
        o = acc_ref[...] / l_ref[...]
        lam = sc_ref[n_heads]
        gain = g_ref[...]
        outs = []
        for hh in range(n_heads):
            r0 = hh * rows_per_head
            oh = o[r0:r0 + t_new] - lam * o[r0 + t_new:r0 + 2 * t_new]
            outs.append(_subln(oh, gain, post_scale))
        o_ref[...] = jnp.concatenate(outs, axis=1).astype(o_ref.dtype)


def _decode_attn(q2, cache_k, cache_v, layer, k_new, v_new, g_subln, page_table, scal, post_scale):
    batch, n_pages = page_table.shape
    _, t_new, n_heads, _ = k_new.shape
    rows = 2 * n_heads * t_new
    width = n_heads * HEAD_DIM_V
    past_len = n_pages * PAGE_SIZE
    grp = _pick(n_pages, DECODE_PAGES_PER_STEP)
    page_block = (None, None, PAGE_SIZE, n_heads, HEAD_DIM_V)

    def page_spec(g):
        return pl.BlockSpec(page_block, lambda b, p, pt, sc: (layer, pt[b, p * grp + g], 0, 0, 0))

    new_spec = pl.BlockSpec((None, t_new, n_heads, HEAD_DIM_V), lambda b, p, pt, sc: (b, 0, 0, 0))
    grid_spec = pltpu.PrefetchScalarGridSpec(
        num_scalar_prefetch=2,
        grid=(batch, n_pages // grp),
        in_specs=[pl.BlockSpec((None, rows, HEAD_DIM_V), lambda b, p, pt, sc: (b, 0, 0))]
        + [page_spec(g) for g in range(grp)] * 2
        + [new_spec, new_spec, pl.BlockSpec((1, HEAD_DIM_V), lambda b, p, pt, sc: (0, 0))],
        out_specs=pl.BlockSpec((None, t_new, width), lambda b, p, pt, sc: (b, 0, 0)),
        scratch_shapes=[
            pltpu.VMEM((rows, PAGE_SIZE * n_heads), F32),
            pltpu.VMEM((rows, 1), F32),
            pltpu.VMEM((rows, 1), F32),
            pltpu.VMEM((rows, 1), F32),
            pltpu.VMEM((rows, HEAD_DIM_V), F32),
        ],
    )
    page_bytes = PAGE_SIZE * width * 4
    nbytes = 4 * grp * page_bytes + 6 * rows * PAGE_SIZE * n_heads * 4
    return pl.pallas_call(
        functools.partial(_decode_attn_kernel, post_scale, n_heads, t_new, past_len, grp),
        grid_spec=grid_spec,
        out_shape=jax.ShapeDtypeStruct((batch, t_new, width), BF16),
        compiler_params=_params(("parallel", "arbitrary"), nbytes),
        name="decode_attn",
    )(page_table, scal, q2, *([cache_k] * grp), *([cache_v] * grp), k_new, v_new, g_subln)


def _conv_kernel(seq, chunk, u_ref, prev_ref, w_ref, bdw_ref, gcn_ref, bcn_ref, o_ref, new_ref, ext_ref):
    hist = CONV_KERNEL - 1
    lead = CONV_PAD - hist
    ext_ref[0:lead, :] = jnp.zeros((lead, LANES), F32)
    ext_ref[lead:CONV_PAD, :] = prev_ref[0]
    ext_ref[CONV_PAD:CONV_PAD + seq, :] = u_ref[0]
    new_ref[0] = ext_ref[lead + seq:CONV_PAD + seq, :]
    bdw = bdw_ref[...]
    gcn = gcn_ref[...]
    bcn = bcn_ref[...]

    def body(c, _):
        r0 = pl.multiple_of(c * chunk, chunk)
        parts = [None] * CONV_PARTIALS
        for j in range(CONV_KERNEL):
            term = w_ref[j:j + 1, :] * ext_ref[pl.ds(r0 + (lead + j), chunk), :]
            slot = j % CONV_PARTIALS
            parts[slot] = term if parts[slot] is None else parts[slot] + term
        y = functools.reduce(lambda a, b: a + b, parts) + bdw
        mu = jnp.mean(y, axis=-1, keepdims=True)
        yc = y - mu
        var = jnp.mean(yc * yc, axis=-1, keepdims=True)
        yn = yc * lax.rsqrt(var + LN_EPS) * gcn + bcn
        o_ref[0, pl.ds(r0, chunk), :] = (yn * _sigmoid(yn)).astype(o_ref.dtype)
        return 0

    n_chunks = seq // chunk
    lax.fori_loop(0, n_chunks, body, 0, unroll=math.gcd(n_chunks, CONV_CHUNKS_PER_ITER))


def _conv_module(u, prev, w_dw, b_dw, g_cn, b_cn):
    batch, seq, width = u.shape
    assert width // CONV_GROUPS == LANES
    chunk = _pick(seq, 64)
    hist = CONV_KERNEL - 1
    vec = pl.BlockSpec((1, LANES), lambda b, c: (0, c))
    return pl.pallas_call(
        functools.partial(_conv_kernel, seq, chunk),
        grid=(batch, width // LANES),
        in_specs=[
            pl.BlockSpec((1, seq, LANES), lambda b, c: (b, 0, c)),
            pl.BlockSpec((1, hist, LANES), lambda b, c: (b, 0, c)),
            pl.BlockSpec((CONV_KERNEL, LANES), lambda b, c: (0, c)),
            vec, vec, vec,
        ],
        out_specs=[
            pl.BlockSpec((1, seq, LANES), lambda b, c: (b, 0, c)),
            pl.BlockSpec((1, hist, LANES), lambda b, c: (b, 0, c)),
        ],
        out_shape=[jax.ShapeDtypeStruct((batch, seq, width), BF16), jax.ShapeDtypeStruct((batch, hist, width), F32)],
        scratch_shapes=[pltpu.VMEM((CONV_PAD + seq, LANES), F32)],
        compiler_params=_params(("parallel", "parallel"), 8 * (seq + CONV_PAD) * LANES * 4),
        name="conv_module",
    )(u, prev, w_dw, b_dw, g_cn, b_cn)


def _out_proj_kernel(ka, a_ref, b_ref, as_ref, bs_ref, w_ref, r_ref, rs_ref, o_ref, os_ref):
    wa = w_ref[0:ka, :].astype(BF16)
    wb = w_ref[ka:, :].astype(BF16)

    def proj(a, b):
        return jnp.dot(a, wa, preferred_element_type=F32) + jnp.dot(b, wb, preferred_element_type=F32)

    o_ref[...] = r_ref[...] + proj(a_ref[...], b_ref[...])

    @pl.when(pl.program_id(0) == 0)
    def _():
        os_ref[...] = rs_ref[...] + proj(as_ref[...], bs_ref[...])


def _out_proj(a, b, a_s, b_s, w, res, res_s, tm, tn):
    m, ka = a.shape
    ms = a_s.shape[0]
    kb = b.shape[1]
    d = w.shape[1]
    nj = d // tn
    nbytes = 2 * tm * (ka + kb) * 2 + 2 * (ka + kb) * tn * w.dtype.itemsize + (ka + kb) * tn * 2 + 5 * tm * tn * 4
    return pl.pallas_call(
        functools.partial(_out_proj_kernel, ka),
        grid=(m // tm, nj),
        in_specs=[_row_spec(tm, ka), _row_spec(tm, kb), _const_spec(ms, ka), _const_spec(ms, kb),
                  _col_spec(ka + kb, tn), _tile_spec(tm, tn), _side_tile_spec(ms, tn, nj)],
        out_specs=[_tile_spec(tm, tn), _side_tile_spec(ms, tn, nj)],
        out_shape=[jax.ShapeDtypeStruct((m, d), F32), jax.ShapeDtypeStruct((ms, d), F32)],
        compiler_params=_params(("arbitrary", "arbitrary"), nbytes),
        name="out_proj",
    )(a, b, a_s, b_s, w, res, res_s)


def _ple_kernel(final_norm, h_ref, p_ref, hs_ref, ps_ref, g_ref, wg_ref, wp_ref, gf_ref, o_ref, os_ref):
    def ple(h, p):
        n = _rms(h, g_ref[...]).astype(BF16)
        gate = _sigmoid(jnp.dot(n, wg_ref[...], preferred_element_type=F32))
        proj = jnp.dot(p.astype(BF16), wp_ref[...], preferred_element_type=F32)
        h = h + proj * gate
        return _rms(h, gf_ref[...]) if final_norm else h

    o_ref[...] = ple(h_ref[...], p_ref[...])

    @pl.when(pl.program_id(0) == 0)
    def _():
        os_ref[...] = ple(hs_ref[...], ps_ref[...])


def _ple(h, p, h_s, p_s, g, wg, wp, g_final, final_norm, tm):
    m, d = h.shape
    ms = h_s.shape[0]
    dp = p.shape[1]
    nbytes = 4 * tm * d * 4 + 2 * d * d * 2 + 2 * dp * d * 2 + 2 * tm * dp * 4 + 4 * tm * d * 4
    row = lambda cols: pl.BlockSpec((tm, cols), lambda i: (i, 0))
    const = lambda rows, cols: pl.BlockSpec((rows, cols), lambda i: (0, 0))
    return pl.pallas_call(
        functools.partial(_ple_kernel, final_norm),
        grid=(m // tm,),
        in_specs=[row(d), row(dp), const(ms, d), const(ms, dp), const(1, d), const(d, d), const(dp, d), const(1, d)],
        out_specs=[row(d), const(ms, d)],
        out_shape=[jax.ShapeDtypeStruct((m, d), F32), jax.ShapeDtypeStruct((ms, d), F32)],
        compiler_params=_params(("arbitrary",), nbytes),
        name="ple",
    )(h, p, h_s, p_s, g, wg, wp, g_final)


def _lambda_init(layer_idx):
    return 0.8 - 0.6 * math.exp(-0.3 * layer_idx)


def kernel(x_prompt, x_sample, cache_k, cache_v, state_conv, page_table, p_prompt, p_sample, g_ffn1, w_ffn1_gate, w_ffn1_up, w_ffn1_down, g_mix, w_in, lambda_q1, lambda_k1, lambda_q2, lambda_k2, g_subln, w_dw, b_dw, g_conv_norm, b_conv_norm, w_out, g_ffn2, w_ffn2_gate, w_ffn2_up, w_ffn2_down, g_ple, w_ple_gate, w_ple_proj, g_final):
    depth = w_in.shape[0]
    batch, seq, d_model = x_prompt.shape
    dec_batch, dec_seq, _ = x_sample.shape
    n_heads = cache_k.shape[3]
    assert cache_k.shape[4] == 2 * HEAD_DIM_QK and cache_v.shape[4] == HEAD_DIM_V and cache_k.shape[2] == PAGE_SIZE
    qk_cols = n_heads * 2 * HEAD_DIM_QK
    attn_width = n_heads * HEAD_DIM_V
    conv_width = d_model - attn_width
    n_p, n_s = batch * seq, dec_batch * dec_seq
    h_p = x_prompt.reshape(n_p, d_model)
    h_s = x_sample.reshape(n_s, d_model)
    tm = _pick(n_p, TOKEN_TILE)
    tq = _pick(seq, ATTN_Q_TILE)
    slopes = jnp.asarray([LOG2E * 2.0 ** (-8.0 * (h + 1) / n_heads) for h in range(n_heads)], F32)
    row2 = lambda a: a.reshape(1, -1)
    gf = row2(g_final)

    outs = {name: [] for name in ("kp", "vp", "cp", "ks", "vs", "cs")}
    for l in range(depth):
        last = l == depth - 1
        lam_init = _lambda_init(l)
        lam = (jnp.exp(jnp.sum(lambda_q1[l] * lambda_k1[l])) - jnp.exp(jnp.sum(lambda_q2[l] * lambda_k2[l])) + lam_init)
        scal = jnp.concatenate([slopes, lam.reshape(1).astype(F32)])
        post_scale = 1.0 - lam_init
        w1g, w1u, w1d = w_ffn1_gate[l], w_ffn1_up[l], w_ffn1_down[l].astype(BF16)
        w2g, w2u, w2d = w_ffn2_gate[l], w_ffn2_up[l], w_ffn2_down[l].astype(BF16)
        w_in16 = w_in[l].astype(BF16)
        w_out16 = w_out[l].astype(BF16)
        w_pg16 = w_ple_gate[l].astype(BF16)
        w_pp16 = w_ple_proj[l].astype(BF16)
        tf = _pick(w1g.shape[1], 512)
        tn_proj = math.gcd(math.gcd(qk_cols, attn_width), math.gcd(conv_width, 1024))
        g_sub = row2(g_subln[l])

        def ffn(hp, hs, g, wg, wu, wd):
            hid_p, hid_s = _ffn_up(hp, hs, row2(g), wg, wu, tm, tf)
            return _ffn_down(hid_p, hid_s, wd, hp, hs, tm, _pick(d_model, 512))

        h_p, h_s = ffn(h_p, h_s, g_ffn1[l], w1g, w1u, w1d)
        (q_p, q_s), (k_p, k_s), (v_p, v_s), (glu_p, glu_s) = _in_proj(
            h_p, h_s, row2(g_mix[l]), w_in16, qk_cols, attn_width, conv_width, tm, tn_proj)
        conv_args = (w_dw[l], row2(b_dw[l]), row2(g_conv_norm[l]), row2(b_conv_norm[l]))

        attn_p = _prompt_attn(q_p, k_p, v_p, g_sub, scal, batch, seq, n_heads, post_scale, tq)
        zero_hist = jnp.zeros((batch, CONV_KERNEL - 1, conv_width), F32)
        conv_p, hist_p = _conv_module(glu_p.reshape(batch, seq, conv_width), zero_hist, *conv_args)
        outs["kp"].append(k_p.reshape(batch, seq, n_heads, 2 * HEAD_DIM_QK))
        outs["vp"].append(v_p.reshape(batch, seq, n_heads, HEAD_DIM_V))
        outs["cp"].append(hist_p)

        k_new = k_s.reshape(dec_batch, dec_seq, n_heads, 2 * HEAD_DIM_QK)
        v_new = v_s.reshape(dec_batch, dec_seq, n_heads, HEAD_DIM_V)
        q5 = q_s.reshape(dec_batch, dec_seq, n_heads, 2, HEAD_DIM_QK)
        q2 = jnp.einsum("bthmd,mn->bhmtnd", q5, jnp.eye(2, dtype=BF16))
        q2 = q2.reshape(dec_batch, 2 * n_heads * dec_seq, 2 * HEAD_DIM_QK)
        attn_s = _decode_attn(q2, cache_k, cache_v, l, k_new, v_new, g_sub, page_table, scal, post_scale)
        conv_s, hist_s = _conv_module(glu_s.reshape(dec_batch, dec_seq, conv_width), state_conv[l], *conv_args)
        outs["ks"].append(k_new)
        outs["vs"].append(v_new)
        outs["cs"].append(hist_s)

        h_p, h_s = _out_proj(attn_p, conv_p.reshape(n_p, conv_width), attn_s.reshape(n_s, attn_width),
                             conv_s.reshape(n_s, conv_width), w_out16, h_p, h_s, tm, _pick(d_model, 1024))
        h_p, h_s = ffn(h_p, h_s, g_ffn2[l], w2g, w2u, w2d)
        h_p, h_s = _ple(h_p, p_prompt[l].reshape(n_p, -1), h_s, p_sample[l].reshape(n_s, -1), row2(g_ple[l]),
                        w_pg16, w_pp16, gf, last, min(tm, 512))

    y_prompt = h_p.reshape(batch, seq, d_model)
    y_sample = h_s.reshape(dec_batch, dec_seq, d_model)
    return (y_prompt, y_sample, jnp.stack(outs["kp"]), jnp.stack(outs["vp"]), jnp.stack(outs["cp"]),
            jnp.stack(outs["ks"]), jnp.stack(outs["vs"]), jnp.stack(outs["cs"]))
```

```python
import functools
import math

import jax
import jax.numpy as jnp
from jax import lax
from jax.experimental import pallas as pl
from jax.experimental.pallas import tpu as pltpu

F32 = jnp.float32
BF16 = jnp.bfloat16

HEAD_DIM_QK = 64
HEAD_DIM_V = 2 * HEAD_DIM_QK
CONV_GROUPS = 8
CONV_KERNEL = 31
PAGE_SIZE = 128
RMS_EPS = 1e-6
LN_EPS = 1e-5
NEG_INF = -1e30
LOG2E = math.log2(math.e)

V7X_VMEM_BYTES = 64 * 1024 * 1024
LANES = 128
SUBLANES = 8
CONV_PAD = 32
CONV_PARTIALS = 4
CONV_CHUNK_ROWS = 64
CONV_CHUNKS_PER_ITER = 4
DECODE_PAGES_PER_STEP = 16
ATTN_Q_TILE = 1024
ATTN_COL_GROUPS = 4
TOKEN_TILE = 1024


def _vmem_limit(nbytes):
    return int(min(V7X_VMEM_BYTES - 4 * 1024 * 1024, nbytes + 16 * 1024 * 1024))


def _params(sem, nbytes):
    return pltpu.CompilerParams(dimension_semantics=sem, vmem_limit_bytes=_vmem_limit(nbytes))


def _rms(x, g):
    ms = jnp.mean(x * x, axis=-1, keepdims=True)
    return x * lax.rsqrt(ms + RMS_EPS) * g


def _sigmoid(x):
    return 1.0 / (1.0 + jnp.exp(-x))


def _pick(n, pref):
    t = min(n, pref)
    assert n % t == 0, (n, t)
    return t


def _row_spec(tm, cols):
    return pl.BlockSpec((tm, cols), lambda i, j: (i, 0))


def _col_spec(rows, tn, col0=0):
    off = col0 // tn
    return pl.BlockSpec((rows, tn), lambda i, j: (0, j + off))


def _tile_spec(tm, tn):
    return pl.BlockSpec((tm, tn), lambda i, j: (i, j))


def _const_spec(rows, cols):
    return pl.BlockSpec((rows, cols), lambda i, j: (0, 0))


def _side_tile_spec(rows, tn, n_col_tiles):
    return pl.BlockSpec((rows, tn), lambda i, j: (0, jnp.where(i == 0, j, n_col_tiles - 1)))


def _stage_norm(x_ref, xs_ref, g_ref, n_ref, ns_ref):
    i, j = pl.program_id(0), pl.program_id(1)

    @pl.when(j == 0)
    def _():
        n_ref[...] = _rms(x_ref[...], g_ref[...]).astype(BF16)

    @pl.when(jnp.logical_and(i == 0, j == 0))
    def _():
        ns_ref[...] = _rms(xs_ref[...], g_ref[...]).astype(BF16)


def _norm_scratch(tm, ms, d):
    return [pltpu.VMEM((tm, d), BF16), pltpu.VMEM((ms, d), BF16)]


def _ffn_up_kernel(x_ref, xs_ref, g_ref, wg_ref, wu_ref, o_ref, os_ref, n_ref, ns_ref):
    _stage_norm(x_ref, xs_ref, g_ref, n_ref, ns_ref)

    wg = wg_ref[...].astype(BF16)
    wu = wu_ref[...].astype(BF16)

    def act(n):
        a = jnp.dot(n, wg, preferred_element_type=F32)
        u = jnp.dot(n, wu, preferred_element_type=F32)
        return (a * _sigmoid(a) * u).astype(BF16)

    o_ref[...] = act(n_ref[...])

    @pl.when(pl.program_id(0) == 0)
    def _():
        os_ref[...] = act(ns_ref[...])


def _ffn_up(x, xs, g, wg, wu, tm, tf):
    m, d = x.shape
    ms = xs.shape[0]
    f = wg.shape[1]
    nj = f // tf
    wb = wg.dtype.itemsize
    nbytes = 2 * tm * d * 4 + tm * d * 2 + 2 * 2 * d * tf * wb + 2 * d * tf * 2 + 2 * tm * tf * 2 + 3 * tm * tf * 4
    return pl.pallas_call(
        _ffn_up_kernel,
        grid=(m // tm, nj),
        in_specs=[_row_spec(tm, d), _const_spec(ms, d), _const_spec(1, d), _col_spec(d, tf), _col_spec(d, tf)],
        out_specs=[_tile_spec(tm, tf), _side_tile_spec(ms, tf, nj)],
        out_shape=[jax.ShapeDtypeStruct((m, f), BF16), jax.ShapeDtypeStruct((ms, f), BF16)],
        scratch_shapes=_norm_scratch(tm, ms, d),
        compiler_params=_params(("arbitrary", "arbitrary"), nbytes),
        name="ffn_up",
    )(x, xs, g, wg, wu)


def _ffn_down_kernel(h_ref, hs_ref, w_ref, r_ref, rs_ref, o_ref, os_ref):
    w = w_ref[...].astype(BF16)
    o_ref[...] = r_ref[...] + 0.5 * jnp.dot(h_ref[...], w, preferred_element_type=F32)

    @pl.when(pl.program_id(0) == 0)
    def _():
        os_ref[...] = rs_ref[...] + 0.5 * jnp.dot(hs_ref[...], w, preferred_element_type=F32)


def _ffn_down(hid, hid_s, wd, res, res_s, tm, tn):
    m, f = hid.shape
    ms = hid_s.shape[0]
    d = wd.shape[1]
    nj = d // tn
    nbytes = 2 * tm * f * 2 + 2 * f * tn * wd.dtype.itemsize + f * tn * 2 + 4 * tm * tn * 4 + tm * tn * 4
    return pl.pallas_call(
        _ffn_down_kernel,
        grid=(m // tm, nj),
        in_specs=[_row_spec(tm, f), _const_spec(ms, f), _col_spec(f, tn), _tile_spec(tm, tn),
                  _side_tile_spec(ms, tn, nj)],
        out_specs=[_tile_spec(tm, tn), _side_tile_spec(ms, tn, nj)],
        out_shape=[jax.ShapeDtypeStruct((m, d), F32), jax.ShapeDtypeStruct((ms, d), F32)],
        compiler_params=_params(("arbitrary", "arbitrary"), nbytes),
        name="ffn_down",
    )(hid, hid_s, wd, res, res_s)


def _qkv_proj_kernel(q_scale, x_ref, xs_ref, g_ref, w_ref, q_ref, k_ref, v_ref, qs_ref, ks_ref, vs_ref, n_ref, ns_ref):
    _stage_norm(x_ref, xs_ref, g_ref, n_ref, ns_ref)
    i, j = pl.program_id(0), pl.program_id(1)
    w = w_ref[...].astype(BF16)

    def emit(n, outs):
        u = jnp.dot(n, w, preferred_element_type=F32)
        for jj, o_ref in enumerate(outs):
            @pl.when(j == jj)
            def _():
                o_ref[...] = (u * q_scale if jj == 0 else u).astype(o_ref.dtype)

    emit(n_ref[...], (q_ref, k_ref, v_ref))

    @pl.when(i == 0)
    def _():
        emit(ns_ref[...], (qs_ref, ks_ref, vs_ref))


def _glu_proj_kernel(x_ref, xs_ref, g_ref, wa_ref, wg_ref, o_ref, os_ref, n_ref, ns_ref):
    _stage_norm(x_ref, xs_ref, g_ref, n_ref, ns_ref)

    wa = wa_ref[...].astype(BF16)
    wg = wg_ref[...].astype(BF16)

    def glu(n):
        a = jnp.dot(n, wa, preferred_element_type=F32)
        gt = jnp.dot(n, wg, preferred_element_type=F32)
        return a * _sigmoid(gt)

    o_ref[...] = glu(n_ref[...])

    @pl.when(pl.program_id(0) == 0)
    def _():
        os_ref[...] = glu(ns_ref[...])


def _in_proj(x, xs, g, w_in, qk_cols, attn_width, conv_width, tm, tn):
    m, d = x.shape
    ms = xs.shape[0]
    q_scale = HEAD_DIM_QK ** -0.5 * LOG2E
    nbytes = 2 * tm * d * 4 + tm * d * 2 + 4 * d * tn * w_in.dtype.itemsize + 2 * d * tn * 2 + 6 * tm * tn * 4

    def call(kernel, col0s, cols, dtype, name):
        nj = cols // tn
        return pl.pallas_call(
            kernel,
            grid=(m // tm, nj),
            in_specs=[_row_spec(tm, d), _const_spec(ms, d), _const_spec(1, d)] + [_col_spec(d, tn, c) for c in col0s],
            out_specs=[_tile_spec(tm, tn), _side_tile_spec(ms, tn, nj)],
            out_shape=[jax.ShapeDtypeStruct((m, cols), dtype), jax.ShapeDtypeStruct((ms, cols), dtype)],
            scratch_shapes=_norm_scratch(tm, ms, d),
            compiler_params=_params(("arbitrary", "arbitrary"), nbytes),
            name=name,
        )(x, xs, g, *([w_in] * len(col0s)))

    assert qk_cols == attn_width
    width = qk_cols
    whole = pl.BlockSpec((tm, width), lambda i, j: (i, 0))
    whole_side = _const_spec(ms, width)
    dtypes = (BF16, F32, F32)
    qkv_bytes = 2 * tm * d * 4 + tm * d * 2 + 2 * d * width * w_in.dtype.itemsize + (2 * 10 + 4) * tm * width
    qkv = pl.pallas_call(
        functools.partial(_qkv_proj_kernel, q_scale),
        grid=(m // tm, 3),
        in_specs=[_row_spec(tm, d), _const_spec(ms, d), _const_spec(1, d), _col_spec(d, width)],
        out_specs=[whole] * 3 + [whole_side] * 3,
        out_shape=[jax.ShapeDtypeStruct((m, width), t) for t in dtypes]
        + [jax.ShapeDtypeStruct((ms, width), t) for t in dtypes],
        scratch_shapes=_norm_scratch(tm, ms, d),
        compiler_params=_params(("arbitrary", "arbitrary"), qkv_bytes),
        name="qkv_proj",
    )(x, xs, g, w_in)
    c0 = 2 * qk_cols + attn_width
    glu = call(_glu_proj_kernel, [c0, c0 + conv_width], conv_width, F32, "glu_proj")
    return (qkv[0], qkv[3]), (qkv[1], qkv[4]), (qkv[2], qkv[5]), glu


def _split_maps(q):
    lane = lax.broadcasted_iota(jnp.int32, q.shape, 1)
    zero = jnp.zeros_like(q)
    return jnp.concatenate([jnp.where(lane < HEAD_DIM_QK, q, zero), jnp.where(lane >= HEAD_DIM_QK, q, zero)], axis=0)


def _subln(o, g, post_scale):
    ms = jnp.mean(o * o, axis=-1, keepdims=True)
    return o * lax.rsqrt(ms + RMS_EPS) * g * post_scale


def _prompt_attn_kernel(post_scale, tq, n_heads, n_groups, sc_ref, q_ref, k_ref, v_ref, g_ref, o_ref, qq_ref, k16_ref,
                        vt_ref, *stat_refs):
    h = pl.program_id(1)
    qi = pl.program_id(2)
    slope = sc_ref[h]
    lam = sc_ref[n_heads]
    gw = 2 * tq // n_groups
    assert tq % gw == 0
    groups = [stat_refs[3 * g:3 * g + 3] for g in range(n_groups)]

    @pl.when(qi == 0)
    def _():
        k16_ref[...] = k_ref[...].astype(BF16)
        vt_ref[...] = v_ref[...].T.astype(BF16)

    qq_ref[...] = _split_maps(q_ref[...])
    for m_ref, l_ref, acc_ref in groups:
        m_ref[...] = jnp.full(m_ref.shape, NEG_INF, F32)
        l_ref[...] = jnp.zeros(l_ref.shape, F32)
        acc_ref[...] = jnp.zeros(acc_ref.shape, F32)

    def block(ki, masked):
        start = pl.multiple_of(ki * tq, tq)
        k = k16_ref[pl.ds(start, tq), :]
        vt = vt_ref[:, pl.ds(start, tq)]
        key = lax.broadcasted_iota(jnp.int32, (tq, LANES), 0)
        kbias = slope * ((ki - qi) * tq + key).astype(F32)
        kbias = jnp.concatenate([kbias] * (gw // LANES), axis=1)
        for g, (m_ref, l_ref, acc_ref) in enumerate(groups):
            qg = qq_ref[g * gw:(g + 1) * gw, :]
            s = lax.dot_general(k, qg, (((1,), (1,)), ((), ())), preferred_element_type=F32) + kbias
            if masked:
                keyf = lax.broadcasted_iota(jnp.int32, (tq, gw), 0)
                query = (g * gw) % tq + lax.broadcasted_iota(jnp.int32, (tq, gw), 1)
                s = jnp.where(keyf <= query, s, NEG_INF)
            m_prev = m_ref[...]
            m_new = jnp.maximum(m_prev, jnp.max(s, axis=0, keepdims=True))
            alpha = jnp.exp2(m_prev - m_new)
            p = jnp.exp2(s - m_new)
            l_ref[...] = alpha * l_ref[...] + jnp.sum(p, axis=0, keepdims=True)
            acc_ref[...] = alpha * acc_ref[...] + jnp.dot(vt, p.astype(BF16), preferred_element_type=F32)
            m_ref[...] = m_new

    def body(ki, carry):
        block(ki, False)
        return carry

    lax.fori_loop(0, qi, body, 0)
    block(qi, True)
    o = jnp.concatenate([acc_ref[...] / l_ref[...] for _, l_ref, acc_ref in groups], axis=1)
    o = o[:, :tq] - lam * o[:, tq:]
    ms = jnp.mean(o * o, axis=0, keepdims=True)
    o = o * (lax.rsqrt(ms + RMS_EPS) * post_scale)
    o_ref[...] = (o.T * g_ref[...]).astype(o_ref.dtype)


def _prompt_attn(q16, k32, v32, g_subln, scal, batch, seq, n_heads, post_scale, tq):
    nq = seq // tq
    gw = 2 * tq // ATTN_COL_GROUPS
    grid_spec = pltpu.PrefetchScalarGridSpec(
        num_scalar_prefetch=1,
        grid=(batch, n_heads, nq),
        in_specs=[
            pl.BlockSpec((tq, HEAD_DIM_V), lambda b, h, i, sc: (b * nq + i, h)),
            pl.BlockSpec((seq, HEAD_DIM_V), lambda b, h, i, sc: (b, h)),
            pl.BlockSpec((seq, HEAD_DIM_V), lambda b, h, i, sc: (b, h)),
            pl.BlockSpec((1, HEAD_DIM_V), lambda b, h, i, sc: (0, 0)),
        ],
        out_specs=pl.BlockSpec((tq, HEAD_DIM_V), lambda b, h, i, sc: (b * nq + i, h)),
        scratch_shapes=[
            pltpu.VMEM((2 * tq, HEAD_DIM_V), BF16),
            pltpu.VMEM((seq, HEAD_DIM_V), BF16),
            pltpu.VMEM((HEAD_DIM_V, seq), BF16),
        ] + [pltpu.VMEM((1, gw), F32), pltpu.VMEM((1, gw), F32), pltpu.VMEM((HEAD_DIM_V, gw), F32)] * ATTN_COL_GROUPS,
    )
    nbytes = 4 * seq * HEAD_DIM_V * 4 + 2 * seq * HEAD_DIM_V * 2 + 8 * 2 * tq * tq * 4
    return pl.pallas_call(
        functools.partial(_prompt_attn_kernel, post_scale, tq, n_heads, ATTN_COL_GROUPS),
        grid_spec=grid_spec,
        out_shape=jax.ShapeDtypeStruct((batch * seq, n_heads * HEAD_DIM_V), BF16),
        compiler_params=_params(("parallel", "parallel", "arbitrary"), nbytes),
        name="prompt_attn",
    )(scal, q16, k32, v32, g_subln)


def _decode_attn_kernel(post_scale, n_heads, t_new, past_len, grp, pt_ref, sc_ref, q_ref, *refs):
    k_refs, v_refs = refs[:grp], refs[grp:2 * grp]
    kn_ref, vn_ref, g_ref, o_ref, bias_ref, slope_ref, m_ref, l_ref, acc_ref = refs[2 * grp:]
    p = pl.program_id(1)
    rows = 2 * n_heads * t_new
    rows_per_head = 2 * t_new
    cols = PAGE_SIZE * n_heads

    def head_geometry(ncols):
        row = lax.broadcasted_iota(jnp.int32, (rows, ncols), 0)
        col = lax.broadcasted_iota(jnp.int32, (rows, ncols), 1)
        rhead = row // rows_per_head
        slope = jnp.zeros((rows, ncols), F32)
        for hh in range(n_heads):
            slope = jnp.where(rhead == hh, sc_ref[hh], slope)
        return row, col, rhead == col % n_heads, col // n_heads, slope

    @pl.when(p == 0)
    def _():
        m_ref[...] = jnp.full(m_ref.shape, NEG_INF, F32)
        l_ref[...] = jnp.zeros(l_ref.shape, F32)
        acc_ref[...] = jnp.zeros(acc_ref.shape, F32)
        _, _, same_head, tok, slope = head_geometry(cols)
        bias_ref[...] = jnp.where(same_head, slope * tok.astype(F32), NEG_INF)
        slope_ref[...] = slope[:, 0:1]

    q = q_ref[...]

    def update(blocks):
        scores = [lax.dot_general(q, k2, (((1,), (1,)), ((), ())), preferred_element_type=F32) + bias
                  for k2, _, bias, _ in blocks]
        m_prev = m_ref[...]
        m_new = m_prev
        for s, (_, _, _, shift) in zip(scores, blocks):
            m_new = jnp.maximum(m_new, jnp.max(s, axis=-1, keepdims=True) + shift)
        alpha = jnp.exp2(m_prev - m_new)
        l_new = alpha * l_ref[...]
        acc = alpha * acc_ref[...]
        for s, (_, v2, _, shift) in zip(scores, blocks):
            pr = jnp.exp2(s - (m_new - shift))
            l_new = l_new + jnp.sum(pr, axis=-1, keepdims=True)
            acc = acc + jnp.dot(pr.astype(BF16), v2, preferred_element_type=F32)
        l_ref[...] = l_new
        acc_ref[...] = acc
        m_ref[...] = m_new

    blocks = []
    for g in range(grp):
        k2 = k_refs[g][...].reshape(cols, HEAD_DIM_V).astype(BF16)
        v2 = v_refs[g][...].reshape(cols, HEAD_DIM_V).astype(BF16)
        page_start = ((p * grp + g) * PAGE_SIZE - past_len).astype(F32)
        blocks.append((k2, v2, bias_ref[...], slope_ref[...] * page_start))
    update(blocks)

    @pl.when(p == pl.num_programs(1) - 1)
    def _():
        n_new = t_new * n_heads
        ncols = pl.cdiv(n_new, LANES) * LANES
        pad = jnp.zeros((ncols - n_new, HEAD_DIM_V), F32)
        kn = jnp.concatenate([kn_ref[...].reshape(n_new, HEAD_DIM_V), pad], axis=0).astype(BF16)
        vn = jnp.concatenate([vn_ref[...].reshape(n_new, HEAD_DIM_V), pad], axis=0).astype(BF16)
        row, col, same_head, tok, slope = head_geometry(ncols)
        visible = jnp.where(same_head, tok, ncols) <= row % t_new
        update([(kn, vn, jnp.where(visible, slope * tok.astype(F32), NEG_INF), 0.0)])
        o = acc_ref[...] / l_ref[...]
        lam = sc_ref[n_heads]
        gain = g_ref[...]
        outs = []
        for hh in range(n_heads):
            r0 = hh * rows_per_head
            oh = o[r0:r0 + t_new] - lam * o[r0 + t_new:r0 + 2 * t_new]
            outs.append(_subln(oh, gain, post_scale))
        o_ref[...] = jnp.concatenate(outs, axis=1).astype(o_ref.dtype)


def _decode_attn(q2, cache_k, cache_v, layer, k_new, v_new, g_subln, page_table, scal, post_scale):
    batch, n_pages = page_table.shape
    _, t_new, n_heads, _ = k_new.shape
    rows = 2 * n_heads * t_new
    width = n_heads * HEAD_DIM_V
    past_len = n_pages * PAGE_SIZE
    grp = _pick(n_pages, DECODE_PAGES_PER_STEP)
    page_block = (None, None, PAGE_SIZE, n_heads, HEAD_DIM_V)

    def page_spec(g):
        return pl.BlockSpec(page_block, lambda b, p, pt, sc: (layer, pt[b, p * grp + g], 0, 0, 0))

    new_spec = pl.BlockSpec((None, t_new, n_heads, HEAD_DIM_V), lambda b, p, pt, sc: (b, 0, 0, 0))
    grid_spec = pltpu.PrefetchScalarGridSpec(
        num_scalar_prefetch=2,
        grid=(batch, n_pages // grp),
        in_specs=[pl.BlockSpec((None, rows, HEAD_DIM_V), lambda b, p, pt, sc: (b, 0, 0))]
        + [page_spec(g) for g in range(grp)] * 2
        + [new_spec, new_spec, pl.BlockSpec((1, HEAD_DIM_V), lambda b, p, pt, sc: (0, 0))],
        out_specs=pl.BlockSpec((None, t_new, width), lambda b, p, pt, sc: (b, 0, 0)),
        scratch_shapes=[
            pltpu.VMEM((rows, PAGE_SIZE * n_heads), F32),
            pltpu.VMEM((rows, 1), F32),
            pltpu.VMEM((rows, 1), F32),
            pltpu.VMEM((rows, 1), F32),
            pltpu.VMEM((rows, HEAD_DIM_V), F32),
        ],
    )
    page_bytes = PAGE_SIZE * width * 4
    nbytes = 4 * grp * page_bytes + 6 * rows * PAGE_SIZE * n_heads * 4
    return pl.pallas_call(
        functools.partial(_decode_attn_kernel, post_scale, n_heads, t_new, past_len, grp),
        grid_spec=grid_spec,
        out_shape=jax.ShapeDtypeStruct((batch, t_new, width), BF16),
        compiler_params=_params(("parallel", "arbitrary"), nbytes),
        name="decode_attn",
    )(page_table, scal, q2, *([cache_k] * grp), *([cache_v] * grp), k_new, v_new, g_subln)


def _conv_kernel(seq, chunk, u_ref, prev_ref, w_ref, bdw_ref, gcn_ref, bcn_ref, o_ref, new_ref, ext_ref):
    hist = CONV_KERNEL - 1
    lead = CONV_PAD - hist
    cw = u_ref.shape[-1]
    ext_ref[0:lead, :] = jnp.zeros((lead, cw), F32)
    ext_ref[lead:CONV_PAD, :] = prev_ref[0]
    ext_ref[CONV_PAD:CONV_PAD + seq, :] = u_ref[0]
    new_ref[0] = ext_ref[lead + seq:CONV_PAD + seq, :]
    bdw = bdw_ref[...]
    gcn = gcn_ref[...]
    bcn = bcn_ref[...]

    def body(c, _):
        r0 = c * chunk if isinstance(c, int) else pl.multiple_of(c * chunk, chunk)
        parts = [None] * CONV_PARTIALS
        for j in range(CONV_KERNEL):
            term = w_ref[j:j + 1, :] * ext_ref[pl.ds(r0 + (lead + j), chunk), :]
            slot = j % CONV_PARTIALS
            parts[slot] = term if parts[slot] is None else parts[slot] + term
        y = functools.reduce(lambda a, b: a + b, parts) + bdw
        outs = []
        for c0 in range(0, cw, LANES):
            grp = slice(c0, c0 + LANES)
            mu = jnp.mean(y[:, grp], axis=-1, keepdims=True)
            yc = y[:, grp] - mu
            var = jnp.mean(yc * yc, axis=-1, keepdims=True)
            yn = yc * lax.rsqrt(var + LN_EPS) * gcn[:, grp] + bcn[:, grp]
            outs.append(yn * _sigmoid(yn))
        o_ref[0, pl.ds(r0, chunk), :] = jnp.concatenate(outs, axis=1).astype(o_ref.dtype)
        return 0

    n_chunks = seq // chunk
    if n_chunks <= CONV_CHUNKS_PER_ITER:
        for c in range(n_chunks):
            body(c, 0)
    else:
        lax.fori_loop(0, n_chunks, body, 0, unroll=math.gcd(n_chunks, CONV_CHUNKS_PER_ITER))


def _conv_module(u, prev, w_dw, b_dw, g_cn, b_cn):
    batch, seq, width = u.shape
    assert width // CONV_GROUPS == LANES
    chunk = _pick(seq, CONV_CHUNK_ROWS)
    cw = min(width, LANES * (CONV_CHUNK_ROWS // chunk))
    assert width % cw == 0
    hist = CONV_KERNEL - 1
    vec = pl.BlockSpec((1, cw), lambda b, c: (0, c))
    return pl.pallas_call(
        functools.partial(_conv_kernel, seq, chunk),
        grid=(batch, width // cw),
        in_specs=[
            pl.BlockSpec((1, seq, cw), lambda b, c: (b, 0, c)),
            pl.BlockSpec((1, hist, cw), lambda b, c: (b, 0, c)),
            pl.BlockSpec((CONV_KERNEL, cw), lambda b, c: (0, c)),
            vec, vec, vec,
        ],
        out_specs=[
            pl.BlockSpec((1, seq, cw), lambda b, c: (b, 0, c)),
            pl.BlockSpec((1, hist, cw), lambda b, c: (b, 0, c)),
        ],
        out_shape=[jax.ShapeDtypeStruct((batch, seq, width), BF16), jax.ShapeDtypeStruct((batch, hist, width), F32)],
        scratch_shapes=[pltpu.VMEM((CONV_PAD + seq, cw), F32)],
        compiler_params=_params(("parallel", "parallel"), 8 * (seq + CONV_PAD) * cw * 4),
        name="conv_module",
    )(u, prev, w_dw, b_dw, g_cn, b_cn)


def _out_proj_kernel(ka, a_ref, b_ref, as_ref, bs_ref, w_ref, r_ref, rs_ref, o_ref, os_ref):
    wa = w_ref[0:ka, :].astype(BF16)
    wb = w_ref[ka:, :].astype(BF16)

    def proj(a, b):
        return jnp.dot(a, wa, preferred_element_type=F32) + jnp.dot(b, wb, preferred_element_type=F32)

    o_ref[...] = r_ref[...] + proj(a_ref[...], b_ref[...])

    @pl.when(pl.program_id(0) == 0)
    def _():
        os_ref[...] = rs_ref[...] + proj(as_ref[...], bs_ref[...])


def _out_proj(a, b, a_s, b_s, w, res, res_s, tm, tn):
    m, ka = a.shape
    ms = a_s.shape[0]
    kb = b.shape[1]
    d = w.shape[1]
    nj = d // tn
    nbytes = 2 * tm * (ka + kb) * 2 + 2 * (ka + kb) * tn * w.dtype.itemsize + (ka + kb) * tn * 2 + 5 * tm * tn * 4
    return pl.pallas_call(
        functools.partial(_out_proj_kernel, ka),
        grid=(m // tm, nj),
        in_specs=[_row_spec(tm, ka), _row_spec(tm, kb), _const_spec(ms, ka), _const_spec(ms, kb),
                  _col_spec(ka + kb, tn), _tile_spec(tm, tn), _side_tile_spec(ms, tn, nj)],
        out_specs=[_tile_spec(tm, tn), _side_tile_spec(ms, tn, nj)],
        out_shape=[jax.ShapeDtypeStruct((m, d), F32), jax.ShapeDtypeStruct((ms, d), F32)],
        compiler_params=_params(("arbitrary", "arbitrary"), nbytes),
        name="out_proj",
    )(a, b, a_s, b_s, w, res, res_s)


def _ple_kernel(final_norm, h_ref, p_ref, hs_ref, ps_ref, g_ref, wg_ref, wp_ref, gf_ref, o_ref, os_ref):
    def ple(h, p):
        n = _rms(h, g_ref[...]).astype(BF16)
        gate = _sigmoid(jnp.dot(n, wg_ref[...], preferred_element_type=F32))
        proj = jnp.dot(p.astype(BF16), wp_ref[...], preferred_element_type=F32)
        h = h + proj * gate
        return _rms(h, gf_ref[...]) if final_norm else h

    o_ref[...] = ple(h_ref[...], p_ref[...])

    @pl.when(pl.program_id(0) == 0)
    def _():
        os_ref[...] = ple(hs_ref[...], ps_ref[...])


def _ple(h, p, h_s, p_s, g, wg, wp, g_final, final_norm, tm):
    m, d = h.shape
    ms = h_s.shape[0]
    dp = p.shape[1]
    nbytes = 4 * tm * d * 4 + 2 * d * d * 2 + 2 * dp * d * 2 + 2 * tm * dp * 4 + 4 * tm * d * 4
    row = lambda cols: pl.BlockSpec((tm, cols), lambda i: (i, 0))
    const = lambda rows, cols: pl.BlockSpec((rows, cols), lambda i: (0, 0))
    return pl.pallas_call(
        functools.partial(_ple_kernel, final_norm),
        grid=(m // tm,),
        in_specs=[row(d), row(dp), const(ms, d), const(ms, dp), const(1, d), const(d, d), const(dp, d), const(1, d)],
        out_specs=[row(d), const(ms, d)],
        out_shape=[jax.ShapeDtypeStruct((m, d), F32), jax.ShapeDtypeStruct((ms, d), F32)],
        compiler_params=_params(("arbitrary",), nbytes),
        name="ple",
    )(h, p, h_s, p_s, g, wg, wp, g_final)


def _lambda_init(layer_idx):
    return 0.8 - 0.6 * math.exp(-0.3 * layer_idx)


def kernel(x_prompt, x_sample, cache_k, cache_v, state_conv, page_table, p_prompt, p_sample, g_ffn1, w_ffn1_gate, w_ffn1_up, w_ffn1_down, g_mix, w_in, lambda_q1, lambda_k1, lambda_q2, lambda_k2, g_subln, w_dw, b_dw, g_conv_norm, b_conv_norm, w_out, g_ffn2, w_ffn2_gate, w_ffn2_up, w_ffn2_down, g_ple, w_ple_gate, w_ple_proj, g_final):
    depth = w_in.shape[0]
    batch, seq, d_model = x_prompt.shape
    dec_batch, dec_seq, _ = x_sample.shape
    n_heads = cache_k.shape[3]
    assert cache_k.shape[4] == 2 * HEAD_DIM_QK and cache_v.shape[4] == HEAD_DIM_V and cache_k.shape[2] == PAGE_SIZE
    qk_cols = n_heads * 2 * HEAD_DIM_QK
    attn_width = n_heads * HEAD_DIM_V
    conv_width = d_model - attn_width
    n_p, n_s = batch * seq, dec_batch * dec_seq
    h_p = x_prompt.reshape(n_p, d_model)
    h_s = x_sample.reshape(n_s, d_model)
    tm = _pick(n_p, TOKEN_TILE)
    tq = _pick(seq, ATTN_Q_TILE)
    slopes = jnp.asarray([LOG2E * 2.0 ** (-8.0 * (h + 1) / n_heads) for h in range(n_heads)], F32)
    row2 = lambda a: a.reshape(1, -1)
    gf = row2(g_final)

    outs = {name: [] for name in ("kp", "vp", "cp", "ks", "vs", "cs")}
    for l in range(depth):
        last = l == depth - 1
        lam_init = _lambda_init(l)
        lam = (jnp.exp(jnp.sum(lambda_q1[l] * lambda_k1[l])) - jnp.exp(jnp.sum(lambda_q2[l] * lambda_k2[l])) + lam_init)
        scal = jnp.concatenate([slopes, lam.reshape(1).astype(F32)])
        post_scale = 1.0 - lam_init
        w1g, w1u, w1d = w_ffn1_gate[l], w_ffn1_up[l], w_ffn1_down[l].astype(BF16)
        w2g, w2u, w2d = w_ffn2_gate[l], w_ffn2_up[l], w_ffn2_down[l].astype(BF16)
        w_in16 = w_in[l].astype(BF16)
        w_out16 = w_out[l].astype(BF16)
        w_pg16 = w_ple_gate[l].astype(BF16)
        w_pp16 = w_ple_proj[l].astype(BF16)
        tf = _pick(w1g.shape[1], 512)
        tn_proj = math.gcd(math.gcd(qk_cols, attn_width), math.gcd(conv_width, 1024))
        g_sub = row2(g_subln[l])

        def ffn(hp, hs, g, wg, wu, wd):
            hid_p, hid_s = _ffn_up(hp, hs, row2(g), wg, wu, tm, tf)
            return _ffn_down(hid_p, hid_s, wd, hp, hs, tm, _pick(d_model, 512))

        h_p, h_s = ffn(h_p, h_s, g_ffn1[l], w1g, w1u, w1d)
        (q_p, q_s), (k_p, k_s), (v_p, v_s), (glu_p, glu_s) = _in_proj(
            h_p, h_s, row2(g_mix[l]), w_in16, qk_cols, attn_width, conv_width, tm, tn_proj)
        conv_args = (w_dw[l], row2(b_dw[l]), row2(g_conv_norm[l]), row2(b_conv_norm[l]))

        attn_p = _prompt_attn(q_p, k_p, v_p, g_sub, scal, batch, seq, n_heads, post_scale, tq)
        zero_hist = jnp.zeros((batch, CONV_KERNEL - 1, conv_width), F32)
        conv_p, hist_p = _conv_module(glu_p.reshape(batch, seq, conv_width), zero_hist, *conv_args)
        outs["kp"].append(k_p.reshape(batch, seq, n_heads, 2 * HEAD_DIM_QK))
        outs["vp"].append(v_p.reshape(batch, seq, n_heads, HEAD_DIM_V))
        outs["cp"].append(hist_p)

        k_new = k_s.reshape(dec_batch, dec_seq, n_heads, 2 * HEAD_DIM_QK)
        v_new = v_s.reshape(dec_batch, dec_seq, n_heads, HEAD_DIM_V)
        q5 = q_s.reshape(dec_batch, dec_seq, n_heads, 2, HEAD_DIM_QK)
        q2 = jnp.einsum("bthmd,mn->bhmtnd", q5, jnp.eye(2, dtype=BF16))
        q2 = q2.reshape(dec_batch, 2 * n_heads * dec_seq, 2 * HEAD_DIM_QK)
        attn_s = _decode_attn(q2, cache_k, cache_v, l, k_new, v_new, g_sub, page_table, scal, post_scale)
        conv_s, hist_s = _conv_module(glu_s.reshape(dec_batch, dec_seq, conv_width), state_conv[l], *conv_args)
        outs["ks"].append(k_new)
        outs["vs"].append(v_new)
        outs["cs"].append(hist_s)

        h_p, h_s = _out_proj(attn_p, conv_p.reshape(n_p, conv_width), attn_s.reshape(n_s, attn_width),
                             conv_s.reshape(n_s, conv_width), w_out16, h_p, h_s, tm, _pick(d_model, 1024))
        h_p, h_s = ffn(h_p, h_s, g_ffn2[l], w2g, w2u, w2d)
        h_p, h_s = _ple(h_p, p_prompt[l].reshape(n_p, -1), h_s, p_sample[l].reshape(n_s, -1), row2(g_ple[l]),
                        w_pg16, w_pp16, gf, last, min(tm, 512))

    y_prompt = h_p.reshape(batch, seq, d_model)
    y_sample = h_s.reshape(dec_batch, dec_seq, d_model)
    return (y_prompt, y_sample, jnp.stack(outs["kp"]), jnp.stack(outs["vp"]), jnp.stack(outs["cp"]),
            jnp.stack(outs["ks"]), jnp.stack(outs["vs"]), jnp.stack(outs["cs"]))
```

```python
import functools
import math

import jax
import jax.numpy as jnp
from jax import lax
from jax.experimental import pallas as pl
from jax.experimental.pallas import tpu as pltpu

F32 = jnp.float32
BF16 = jnp.bfloat16

HEAD_DIM_QK = 64
HEAD_DIM_V = 2 * HEAD_DIM_QK
CONV_GROUPS = 8
CONV_KERNEL = 31
PAGE_SIZE = 128
RMS_EPS = 1e-6
LN_EPS = 1e-5
NEG_INF = -1e30
LOG2E = math.log2(math.e)

V7X_VMEM_BYTES = 64 * 1024 * 1024
LANES = 128
SUBLANES = 8
CONV_PAD = 32
CONV_PARTIALS = 4
CONV_CHUNK_ROWS = 64
CONV_CHUNKS_PER_ITER = 4
DECODE_PAGES_PER_STEP = 16
ATTN_Q_TILE = 1024
ATTN_COL_GROUPS = 4
TOKEN_TILE = 1024


def _vmem_limit(nbytes):
    return int(min(V7X_VMEM_BYTES - 4 * 1024 * 1024, nbytes + 16 * 1024 * 1024))


def _params(sem, nbytes):
    return pltpu.CompilerParams(dimension_semantics=sem, vmem_limit_bytes=_vmem_limit(nbytes))


def _rms(x, g):
    ms = jnp.mean(x * x, axis=-1, keepdims=True)
    return x * lax.rsqrt(ms + RMS_EPS) * g


def _sigmoid(x):
    return 1.0 / (1.0 + jnp.exp(-x))


def _pick(n, pref):
    t = min(n, pref)
    assert n % t == 0, (n, t)
    return t


def _row_spec(tm, cols):
    return pl.BlockSpec((tm, cols), lambda i, j: (i, 0))


def _col_spec(rows, tn, col0=0):
    off = col0 // tn
    return pl.BlockSpec((rows, tn), lambda i, j: (0, j + off))


def _tile_spec(tm, tn):
    return pl.BlockSpec((tm, tn), lambda i, j: (i, j))


def _const_spec(rows, cols):
    return pl.BlockSpec((rows, cols), lambda i, j: (0, 0))


def _side_tile_spec(rows, tn, n_col_tiles):
    return pl.BlockSpec((rows, tn), lambda i, j: (0, jnp.where(i == 0, j, n_col_tiles - 1)))


def _stage_norm(x_ref, xs_ref, g_ref, n_ref, ns_ref):
    i, j = pl.program_id(0), pl.program_id(1)

    @pl.when(j == 0)
    def _():
        n_ref[...] = _rms(x_ref[...], g_ref[...]).astype(BF16)

    @pl.when(jnp.logical_and(i == 0, j == 0))
    def _():
        ns_ref[...] = _rms(xs_ref[...], g_ref[...]).astype(BF16)


def _norm_scratch(tm, ms, d):
    return [pltpu.VMEM((tm, d), BF16), pltpu.VMEM((ms, d), BF16)]


def _ffn_up_kernel(x_ref, xs_ref, g_ref, wg_ref, wu_ref, o_ref, os_ref, n_ref, ns_ref):
    _stage_norm(x_ref, xs_ref, g_ref, n_ref, ns_ref)

    wg = wg_ref[...].astype(BF16)
    wu = wu_ref[...].astype(BF16)

    def act(n):
        a = jnp.dot(n, wg, preferred_element_type=F32)
        u = jnp.dot(n, wu, preferred_element_type=F32)
        return (a * _sigmoid(a) * u).astype(BF16)

    o_ref[...] = act(n_ref[...])

    @pl.when(pl.program_id(0) == 0)
    def _():
        os_ref[...] = act(ns_ref[...])


def _ffn_up(x, xs, g, wg, wu, tm, tf):
    m, d = x.shape
    ms = xs.shape[0]
    f = wg.shape[1]
    nj = f // tf
    wb = wg.dtype.itemsize
    nbytes = 2 * tm * d * 4 + tm * d * 2 + 2 * 2 * d * tf * wb + 2 * d * tf * 2 + 2 * tm * tf * 2 + 3 * tm * tf * 4
    return pl.pallas_call(
        _ffn_up_kernel,
        grid=(m // tm, nj),
        in_specs=[_row_spec(tm, d), _const_spec(ms, d), _const_spec(1, d), _col_spec(d, tf), _col_spec(d, tf)],
        out_specs=[_tile_spec(tm, tf), _side_tile_spec(ms, tf, nj)],
        out_shape=[jax.ShapeDtypeStruct((m, f), BF16), jax.ShapeDtypeStruct((ms, f), BF16)],
        scratch_shapes=_norm_scratch(tm, ms, d),
        compiler_params=_params(("arbitrary", "arbitrary"), nbytes),
        name="ffn_up",
    )(x, xs, g, wg, wu)


def _ffn_down_kernel(h_ref, hs_ref, w_ref, r_ref, rs_ref, o_ref, os_ref):
    w = w_ref[...].astype(BF16)
    o_ref[...] = r_ref[...] + 0.5 * jnp.dot(h_ref[...], w, preferred_element_type=F32)

    @pl.when(pl.program_id(0) == 0)
    def _():
        os_ref[...] = rs_ref[...] + 0.5 * jnp.dot(hs_ref[...], w, preferred_element_type=F32)


def _ffn_down(hid, hid_s, wd, res, res_s, tm, tn):
    m, f = hid.shape
    ms = hid_s.shape[0]
    d = wd.shape[1]
    nj = d // tn
    nbytes = 2 * tm * f * 2 + 2 * f * tn * wd.dtype.itemsize + f * tn * 2 + 4 * tm * tn * 4 + tm * tn * 4
    return pl.pallas_call(
        _ffn_down_kernel,
        grid=(m // tm, nj),
        in_specs=[_row_spec(tm, f), _const_spec(ms, f), _col_spec(f, tn), _tile_spec(tm, tn),
                  _side_tile_spec(ms, tn, nj)],
        out_specs=[_tile_spec(tm, tn), _side_tile_spec(ms, tn, nj)],
        out_shape=[jax.ShapeDtypeStruct((m, d), F32), jax.ShapeDtypeStruct((ms, d), F32)],
        compiler_params=_params(("arbitrary", "arbitrary"), nbytes),
        name="ffn_down",
    )(hid, hid_s, wd, res, res_s)


def _qkv_proj_kernel(q_scale, x_ref, xs_ref, g_ref, w_ref, q_ref, k_ref, v_ref, qs_ref, ks_ref, vs_ref, n_ref, ns_ref):
    _stage_norm(x_ref, xs_ref, g_ref, n_ref, ns_ref)
    i, j = pl.program_id(0), pl.program_id(1)
    w = w_ref[...].astype(BF16)

    def emit(n, outs):
        u = jnp.dot(n, w, preferred_element_type=F32)
        for jj, o_ref in enumerate(outs):
            @pl.when(j == jj)
            def _():
                o_ref[...] = (u * q_scale if jj == 0 else u).astype(o_ref.dtype)

    emit(n_ref[...], (q_ref, k_ref, v_ref))

    @pl.when(i == 0)
    def _():
        emit(ns_ref[...], (qs_ref, ks_ref, vs_ref))


def _glu_proj_kernel(x_ref, xs_ref, g_ref, wa_ref, wg_ref, o_ref, os_ref, n_ref, ns_ref):
    _stage_norm(x_ref, xs_ref, g_ref, n_ref, ns_ref)

    wa = wa_ref[...].astype(BF16)
    wg = wg_ref[...].astype(BF16)

    def glu(n):
        a = jnp.dot(n, wa, preferred_element_type=F32)
        gt = jnp.dot(n, wg, preferred_element_type=F32)
        return a * _sigmoid(gt)

    o_ref[...] = glu(n_ref[...])

    @pl.when(pl.program_id(0) == 0)
    def _():
        os_ref[...] = glu(ns_ref[...])


def _in_proj(x, xs, g, w_in, qk_cols, attn_width, conv_width, tm, tn):
    m, d = x.shape
    ms = xs.shape[0]
    q_scale = HEAD_DIM_QK ** -0.5 * LOG2E
    nbytes = 2 * tm * d * 4 + tm * d * 2 + 4 * d * tn * w_in.dtype.itemsize + 2 * d * tn * 2 + 6 * tm * tn * 4

    def call(kernel, col0s, cols, dtype, name):
        nj = cols // tn
        return pl.pallas_call(
            kernel,
            grid=(m // tm, nj),
            in_specs=[_row_spec(tm, d), _const_spec(ms, d), _const_spec(1, d)] + [_col_spec(d, tn, c) for c in col0s],
            out_specs=[_tile_spec(tm, tn), _side_tile_spec(ms, tn, nj)],
            out_shape=[jax.ShapeDtypeStruct((m, cols), dtype), jax.ShapeDtypeStruct((ms, cols), dtype)],
            scratch_shapes=_norm_scratch(tm, ms, d),
            compiler_params=_params(("arbitrary", "arbitrary"), nbytes),
            name=name,
        )(x, xs, g, *([w_in] * len(col0s)))

    assert qk_cols == attn_width
    width = qk_cols
    whole = pl.BlockSpec((tm, width), lambda i, j: (i, 0))
    whole_side = _const_spec(ms, width)
    dtypes = (BF16, F32, F32)
    qkv_bytes = 2 * tm * d * 4 + tm * d * 2 + 2 * d * width * w_in.dtype.itemsize + (2 * 10 + 4) * tm * width
    qkv = pl.pallas_call(
        functools.partial(_qkv_proj_kernel, q_scale),
        grid=(m // tm, 3),
        in_specs=[_row_spec(tm, d), _const_spec(ms, d), _const_spec(1, d), _col_spec(d, width)],
        out_specs=[whole] * 3 + [whole_side] * 3,
        out_shape=[jax.ShapeDtypeStruct((m, width), t) for t in dtypes]
        + [jax.ShapeDtypeStruct((ms, width), t) for t in dtypes],
        scratch_shapes=_norm_scratch(tm, ms, d),
        compiler_params=_params(("arbitrary", "arbitrary"), qkv_bytes),
        name="qkv_proj",
    )(x, xs, g, w_in)
    c0 = 2 * qk_cols + attn_width
    glu = call(_glu_proj_kernel, [c0, c0 + conv_width], conv_width, F32, "glu_proj")
    return (qkv[0], qkv[3]), (qkv[1], qkv[4]), (qkv[2], qkv[5]), glu


def _split_maps(q):
    lane = lax.broadcasted_iota(jnp.int32, q.shape, 1)
    zero = jnp.zeros_like(q)
    return jnp.concatenate([jnp.where(lane < HEAD_DIM_QK, q, zero), jnp.where(lane >= HEAD_DIM_QK, q, zero)], axis=0)


def _subln(o, g, post_scale):
    ms = jnp.mean(o * o, axis=-1, keepdims=True)
    return o * lax.rsqrt(ms + RMS_EPS) * g * post_scale


def _prompt_attn_kernel(post_scale, tq, n_heads, n_groups, sc_ref, q_ref, k_ref, v_ref, g_ref, o_ref, qq_ref, k16_ref,
                        vt_ref, *stat_refs):
    h = pl.program_id(1)
    qi = pl.program_id(2)
    slope = sc_ref[h]
    lam = sc_ref[n_heads]
    gw = 2 * tq // n_groups
    assert tq % gw == 0
    groups = [stat_refs[3 * g:3 * g + 3] for g in range(n_groups)]

    @pl.when(qi == 0)
    def _():
        k16_ref[...] = k_ref[...].astype(BF16)
        vt_ref[...] = v_ref[...].T.astype(BF16)

    qq_ref[...] = _split_maps(q_ref[...])
    for m_ref, l_ref, acc_ref in groups:
        m_ref[...] = jnp.full(m_ref.shape, NEG_INF, F32)
        l_ref[...] = jnp.zeros(l_ref.shape, F32)
        acc_ref[...] = jnp.zeros(acc_ref.shape, F32)

    def block(ki, masked):
        start = pl.multiple_of(ki * tq, tq)
        key = lax.broadcasted_iota(jnp.int32, (tq, LANES), 0)
        kbias = slope * ((ki - qi) * tq + key).astype(F32)
        kbias = jnp.concatenate([kbias] * (gw // LANES), axis=1)
        for g, (m_ref, l_ref, acc_ref) in enumerate(groups):
            q0 = (g * gw) % tq
            nk = q0 + gw if masked else tq
            k = k16_ref[pl.ds(start, nk), :]
            vt = vt_ref[:, pl.ds(start, nk)]
            qg = qq_ref[g * gw:(g + 1) * gw, :]
            s = lax.dot_general(k, qg, (((1,), (1,)), ((), ())), preferred_element_type=F32) + kbias[:nk]
            if masked:
                keyf = lax.broadcasted_iota(jnp.int32, (nk, gw), 0)
                query = q0 + lax.broadcasted_iota(jnp.int32, (nk, gw), 1)
                s = jnp.where(keyf <= query, s, NEG_INF)
            m_prev = m_ref[...]
            m_new = jnp.maximum(m_prev, jnp.max(s, axis=0, keepdims=True))
            alpha = jnp.exp2(m_prev - m_new)
            p = jnp.exp2(s - m_new)
            l_ref[...] = alpha * l_ref[...] + jnp.sum(p, axis=0, keepdims=True)
            acc_ref[...] = alpha * acc_ref[...] + jnp.dot(vt, p.astype(BF16), preferred_element_type=F32)
            m_ref[...] = m_new

    def body(ki, carry):
        block(ki, False)
        return carry

    lax.fori_loop(0, qi, body, 0)
    block(qi, True)
    o = jnp.concatenate([acc_ref[...] / l_ref[...] for _, l_ref, acc_ref in groups], axis=1)
    o = o[:, :tq] - lam * o[:, tq:]
    ms = jnp.mean(o * o, axis=0, keepdims=True)
    o = o * (lax.rsqrt(ms + RMS_EPS) * post_scale)
    o_ref[...] = (o.T * g_ref[...]).astype(o_ref.dtype)


def _prompt_attn(q16, k32, v32, g_subln, scal, batch, seq, n_heads, post_scale, tq):
    nq = seq // tq
    gw = 2 * tq // ATTN_COL_GROUPS
    grid_spec = pltpu.PrefetchScalarGridSpec(
        num_scalar_prefetch=1,
        grid=(batch, n_heads, nq),
        in_specs=[
            pl.BlockSpec((tq, HEAD_DIM_V), lambda b, h, i, sc: (b * nq + i, h)),
            pl.BlockSpec((seq, HEAD_DIM_V), lambda b, h, i, sc: (b, h)),
            pl.BlockSpec((seq, HEAD_DIM_V), lambda b, h, i, sc: (b, h)),
            pl.BlockSpec((1, HEAD_DIM_V), lambda b, h, i, sc: (0, 0)),
        ],
        out_specs=pl.BlockSpec((tq, HEAD_DIM_V), lambda b, h, i, sc: (b * nq + i, h)),
        scratch_shapes=[
            pltpu.VMEM((2 * tq, HEAD_DIM_V), BF16),
            pltpu.VMEM((seq, HEAD_DIM_V), BF16),
            pltpu.VMEM((HEAD_DIM_V, seq), BF16),
        ] + [pltpu.VMEM((1, gw), F32), pltpu.VMEM((1, gw), F32), pltpu.VMEM((HEAD_DIM_V, gw), F32)] * ATTN_COL_GROUPS,
    )
    nbytes = 4 * seq * HEAD_DIM_V * 4 + 2 * seq * HEAD_DIM_V * 2 + 8 * 2 * tq * tq * 4
    return pl.pallas_call(
        functools.partial(_prompt_attn_kernel, post_scale, tq, n_heads, ATTN_COL_GROUPS),
        grid_spec=grid_spec,
        out_shape=jax.ShapeDtypeStruct((batch * seq, n_heads * HEAD_DIM_V), BF16),
        compiler_params=_params(("parallel", "parallel", "arbitrary"), nbytes),
        name="prompt_attn",
    )(scal, q16, k32, v32, g_subln)


def _decode_attn_kernel(post_scale, n_heads, t_new, past_len, grp, pt_ref, sc_ref, q_ref, *refs):
    k_refs, v_refs = refs[:grp], refs[grp:2 * grp]
    kn_ref, vn_ref, g_ref, o_ref, bias_ref, slope_ref, m_ref, l_ref, acc_ref = refs[2 * grp:]
    p = pl.program_id(1)
    rows = 2 * n_heads * t_new
    rows_per_head = 2 * t_new
    cols = PAGE_SIZE * n_heads

    def head_geometry(ncols):
        row = lax.broadcasted_iota(jnp.int32, (rows, ncols), 0)
        col = lax.broadcasted_iota(jnp.int32, (rows, ncols), 1)
        rhead = row // rows_per_head
        slope = jnp.zeros((rows, ncols), F32)
        for hh in range(n_heads):
            slope = jnp.where(rhead == hh, sc_ref[hh], slope)
        return row, col, rhead == col % n_heads, col // n_heads, slope

    @pl.when(p == 0)
    def _():
        m_ref[...] = jnp.full(m_ref.shape, NEG_INF, F32)
        l_ref[...] = jnp.zeros(l_ref.shape, F32)
        acc_ref[...] = jnp.zeros(acc_ref.shape, F32)
        _, _, same_head, tok, slope = head_geometry(cols)
        bias_ref[...] = jnp.where(same_head, slope * tok.astype(F32), NEG_INF)
        slope_ref[...] = slope[:, 0:1]

    q = q_ref[...]

    def update(blocks):
        scores = [lax.dot_general(q, k2, (((1,), (1,)), ((), ())), preferred_element_type=F32) + bias
                  for k2, _, bias, _ in blocks]
        m_prev = m_ref[...]
        m_new = m_prev
        for s, (_, _, _, shift) in zip(scores, blocks):
            m_new = jnp.maximum(m_new, jnp.max(s, axis=-1, keepdims=True) + shift)
        alpha = jnp.exp2(m_prev - m_new)
        l_new = alpha * l_ref[...]
        acc = alpha * acc_ref[...]
        for s, (_, v2, _, shift) in zip(scores, blocks):
            pr = jnp.exp2(s - (m_new - shift))
            l_new = l_new + jnp.sum(pr, axis=-1, keepdims=True)
            acc = acc + jnp.dot(pr.astype(BF16), v2, preferred_element_type=F32)
        l_ref[...] = l_new
        acc_ref[...] = acc
        m_ref[...] = m_new

    blocks = []
    for g in range(grp):
        k2 = k_refs[g][...].reshape(cols, HEAD_DIM_V).astype(BF16)
        v2 = v_refs[g][...].reshape(cols, HEAD_DIM_V).astype(BF16)
        page_start = ((p * grp + g) * PAGE_SIZE - past_len).astype(F32)
        blocks.append((k2, v2, bias_ref[...], slope_ref[...] * page_start))
    update(blocks)

    @pl.when(p == pl.num_programs(1) - 1)
    def _():
        n_new = t_new * n_heads
        ncols = pl.cdiv(n_new, LANES) * LANES
        pad = jnp.zeros((ncols - n_new, HEAD_DIM_V), F32)
        kn = jnp.concatenate([kn_ref[...].reshape(n_new, HEAD_DIM_V), pad], axis=0).astype(BF16)
        vn = jnp.concatenate([vn_ref[...].reshape(n_new, HEAD_DIM_V), pad], axis=0).astype(BF16)
        row, col, same_head, tok, slope = head_geometry(ncols)
        visible = jnp.where(same_head, tok, ncols) <= row % t_new
        update([(kn, vn, jnp.where(visible, slope * tok.astype(F32), NEG_INF), 0.0)])
        o = acc_ref[...] / l_ref[...]
        lam = sc_ref[n_heads]
        gain = g_ref[...]
        outs = []
        for hh in range(n_heads):
            r0 = hh * rows_per_head
            oh = o[r0:r0 + t_new] - lam * o[r0 + t_new:r0 + 2 * t_new]
            outs.append(_subln(oh, gain, post_scale))
        o_ref[...] = jnp.concatenate(outs, axis=1).astype(o_ref.dtype)


def _decode_attn(q2, cache_k, cache_v, layer, k_new, v_new, g_subln, page_table, scal, post_scale):
    batch, n_pages = page_table.shape
    _, t_new, n_heads, _ = k_new.shape
    rows = 2 * n_heads * t_new
    width = n_heads * HEAD_DIM_V
    past_len = n_pages * PAGE_SIZE
    grp = _pick(n_pages, DECODE_PAGES_PER_STEP)
    page_block = (None, None, PAGE_SIZE, n_heads, HEAD_DIM_V)

    def page_spec(g):
        return pl.BlockSpec(page_block, lambda b, p, pt, sc: (layer, pt[b, p * grp + g], 0, 0, 0))

    new_spec = pl.BlockSpec((None, t_new, n_heads, HEAD_DIM_V), lambda b, p, pt, sc: (b, 0, 0, 0))
    grid_spec = pltpu.PrefetchScalarGridSpec(
        num_scalar_prefetch=2,
        grid=(batch, n_pages // grp),
        in_specs=[pl.BlockSpec((None, rows, HEAD_DIM_V), lambda b, p, pt, sc: (b, 0, 0))]
        + [page_spec(g) for g in range(grp)] * 2
        + [new_spec, new_spec, pl.BlockSpec((1, HEAD_DIM_V), lambda b, p, pt, sc: (0, 0))],
        out_specs=pl.BlockSpec((None, t_new, width), lambda b, p, pt, sc: (b, 0, 0)),
        scratch_shapes=[
            pltpu.VMEM((rows, PAGE_SIZE * n_heads), F32),
            pltpu.VMEM((rows, 1), F32),
            pltpu.VMEM((rows, 1), F32),
            pltpu.VMEM((rows, 1), F32),
            pltpu.VMEM((rows, HEAD_DIM_V), F32),
        ],
    )
    page_bytes = PAGE_SIZE * width * 4
    nbytes = 4 * grp * page_bytes + 6 * rows * PAGE_SIZE * n_heads * 4
    return pl.pallas_call(
        functools.partial(_decode_attn_kernel, post_scale, n_heads, t_new, past_len, grp),
        grid_spec=grid_spec,
        out_shape=jax.ShapeDtypeStruct((batch, t_new, width), BF16),
        compiler_params=_params(("parallel", "arbitrary"), nbytes),
        name="decode_attn",
    )(page_table, scal, q2, *([cache_k] * grp), *([cache_v] * grp), k_new, v_new, g_subln)


def _conv_kernel(seq, chunk, u_ref, prev_ref, w_ref, bdw_ref, gcn_ref, bcn_ref, o_ref, new_ref, ext_ref):
    hist = CONV_KERNEL - 1
    lead = CONV_PAD - hist
    cw = u_ref.shape[-1]
    ext_ref[0:lead, :] = jnp.zeros((lead, cw), F32)
    ext_ref[lead:CONV_PAD, :] = prev_ref[0]
    ext_ref[CONV_PAD:CONV_PAD + seq, :] = u_ref[0]
    new_ref[0] = ext_ref[lead + seq:CONV_PAD + seq, :]
    bdw = bdw_ref[...]
    gcn = gcn_ref[...]
    bcn = bcn_ref[...]

    def body(c, _):
        r0 = c * chunk if isinstance(c, int) else pl.multiple_of(c * chunk, chunk)
        parts = [None] * CONV_PARTIALS
        for j in range(CONV_KERNEL):
            term = w_ref[j:j + 1, :] * ext_ref[pl.ds(r0 + (lead + j), chunk), :]
            slot = j % CONV_PARTIALS
            parts[slot] = term if parts[slot] is None else parts[slot] + term
        y = functools.reduce(lambda a, b: a + b, parts) + bdw
        outs = []
        for c0 in range(0, cw, LANES):
            grp = slice(c0, c0 + LANES)
            mu = jnp.mean(y[:, grp], axis=-1, keepdims=True)
            yc = y[:, grp] - mu
            var = jnp.mean(yc * yc, axis=-1, keepdims=True)
            yn = yc * lax.rsqrt(var + LN_EPS) * gcn[:, grp] + bcn[:, grp]
            outs.append(yn * _sigmoid(yn))
        o_ref[0, pl.ds(r0, chunk), :] = jnp.concatenate(outs, axis=1).astype(o_ref.dtype)
        return 0

    n_chunks = seq // chunk
    if n_chunks <= CONV_CHUNKS_PER_ITER:
        for c in range(n_chunks):
            body(c, 0)
    else:
        lax.fori_loop(0, n_chunks, body, 0, unroll=math.gcd(n_chunks, CONV_CHUNKS_PER_ITER))


def _conv_module(u, prev, w_dw, b_dw, g_cn, b_cn):
    batch, seq, width = u.shape
    assert width // CONV_GROUPS == LANES
    chunk = _pick(seq, CONV_CHUNK_ROWS)
    cw = min(width, LANES * (CONV_CHUNK_ROWS // chunk))
    assert width % cw == 0
    hist = CONV_KERNEL - 1
    vec = pl.BlockSpec((1, cw), lambda b, c: (0, c))
    return pl.pallas_call(
        functools.partial(_conv_kernel, seq, chunk),
        grid=(batch, width // cw),
        in_specs=[
            pl.BlockSpec((1, seq, cw), lambda b, c: (b, 0, c)),
            pl.BlockSpec((1, hist, cw), lambda b, c: (b, 0, c)),
            pl.BlockSpec((CONV_KERNEL, cw), lambda b, c: (0, c)),
            vec, vec, vec,
        ],
        out_specs=[
            pl.BlockSpec((1, seq, cw), lambda b, c: (b, 0, c)),
            pl.BlockSpec((1, hist, cw), lambda b, c: (b, 0, c)),
        ],
        out_shape=[jax.ShapeDtypeStruct((batch, seq, width), BF16), jax.ShapeDtypeStruct((batch, hist, width), F32)],
        scratch_shapes=[pltpu.VMEM((CONV_PAD + seq, cw), F32)],
        compiler_params=_params(("parallel", "parallel"), 8 * (seq + CONV_PAD) * cw * 4),
        name="conv_module",
    )(u, prev, w_dw, b_dw, g_cn, b_cn)


def _out_proj_kernel(ka, a_ref, b_ref, as_ref, bs_ref, w_ref, r_ref, rs_ref, o_ref, os_ref):
    wa = w_ref[0:ka, :].astype(BF16)
    wb = w_ref[ka:, :].astype(BF16)

    def proj(a, b):
        return jnp.dot(a, wa, preferred_element_type=F32) + jnp.dot(b, wb, preferred_element_type=F32)

    o_ref[...] = r_ref[...] + proj(a_ref[...], b_ref[...])

    @pl.when(pl.program_id(0) == 0)
    def _():
        os_ref[...] = rs_ref[...] + proj(as_ref[...], bs_ref[...])


def _out_proj(a, b, a_s, b_s, w, res, res_s, tm, tn):
    m, ka = a.shape
    ms = a_s.shape[0]
    kb = b.shape[1]
    d = w.shape[1]
    nj = d // tn
    nbytes = 2 * tm * (ka + kb) * 2 + 2 * (ka + kb) * tn * w.dtype.itemsize + (ka + kb) * tn * 2 + 5 * tm * tn * 4
    return pl.pallas_call(
        functools.partial(_out_proj_kernel, ka),
        grid=(m // tm, nj),
        in_specs=[_row_spec(tm, ka), _row_spec(tm, kb), _const_spec(ms, ka), _const_spec(ms, kb),
                  _col_spec(ka + kb, tn), _tile_spec(tm, tn), _side_tile_spec(ms, tn, nj)],
        out_specs=[_tile_spec(tm, tn), _side_tile_spec(ms, tn, nj)],
        out_shape=[jax.ShapeDtypeStruct((m, d), F32), jax.ShapeDtypeStruct((ms, d), F32)],
        compiler_params=_params(("arbitrary", "arbitrary"), nbytes),
        name="out_proj",
    )(a, b, a_s, b_s, w, res, res_s)


def _ple_kernel(final_norm, h_ref, p_ref, hs_ref, ps_ref, g_ref, wg_ref, wp_ref, gf_ref, o_ref, os_ref):
    def ple(h, p):
        n = _rms(h, g_ref[...]).astype(BF16)
        gate = _sigmoid(jnp.dot(n, wg_ref[...], preferred_element_type=F32))
        proj = jnp.dot(p.astype(BF16), wp_ref[...], preferred_element_type=F32)
        h = h + proj * gate
        return _rms(h, gf_ref[...]) if final_norm else h

    o_ref[...] = ple(h_ref[...], p_ref[...])

    @pl.when(pl.program_id(0) == 0)
    def _():
        os_ref[...] = ple(hs_ref[...], ps_ref[...])


def _ple(h, p, h_s, p_s, g, wg, wp, g_final, final_norm, tm):
    m, d = h.shape
    ms = h_s.shape[0]
    dp = p.shape[1]
    nbytes = 4 * tm * d * 4 + 2 * d * d * 2 + 2 * dp * d * 2 + 2 * tm * dp * 4 + 4 * tm * d * 4
    row = lambda cols: pl.BlockSpec((tm, cols), lambda i: (i, 0))
    const = lambda rows, cols: pl.BlockSpec((rows, cols), lambda i: (0, 0))
    return pl.pallas_call(
        functools.partial(_ple_kernel, final_norm),
        grid=(m // tm,),
        in_specs=[row(d), row(dp), const(ms, d), const(ms, dp), const(1, d), const(d, d), const(dp, d), const(1, d)],
        out_specs=[row(d), const(ms, d)],
        out_shape=[jax.ShapeDtypeStruct((m, d), F32), jax.ShapeDtypeStruct((ms, d), F32)],
        compiler_params=_params(("arbitrary",), nbytes),
        name="ple",
    )(h, p, h_s, p_s, g, wg, wp, g_final)


def _lambda_init(layer_idx):
    return 0.8 - 0.6 * math.exp(-0.3 * layer_idx)


def kernel(x_prompt, x_sample, cache_k, cache_v, state_conv, page_table, p_prompt, p_sample, g_ffn1, w_ffn1_gate, w_ffn1_up, w_ffn1_down, g_mix, w_in, lambda_q1, lambda_k1, lambda_q2, lambda_k2, g_subln, w_dw, b_dw, g_conv_norm, b_conv_norm, w_out, g_ffn2, w_ffn2_gate, w_ffn2_up, w_ffn2_down, g_ple, w_ple_gate, w_ple_proj, g_final):
    depth = w_in.shape[0]
    batch, seq, d_model = x_prompt.shape
    dec_batch, dec_seq, _ = x_sample.shape
    n_heads = cache_k.shape[3]
    assert cache_k.shape[4] == 2 * HEAD_DIM_QK and cache_v.shape[4] == HEAD_DIM_V and cache_k.shape[2] == PAGE_SIZE
    qk_cols = n_heads * 2 * HEAD_DIM_QK
    attn_width = n_heads * HEAD_DIM_V
    conv_width = d_model - attn_width
    n_p, n_s = batch * seq, dec_batch * dec_seq
    h_p = x_prompt.reshape(n_p, d_model)
    h_s = x_sample.reshape(n_s, d_model)
    tm = _pick(n_p, TOKEN_TILE)
    tq = _pick(seq, ATTN_Q_TILE)
    slopes = jnp.asarray([LOG2E * 2.0 ** (-8.0 * (h + 1) / n_heads) for h in range(n_heads)], F32)
    row2 = lambda a: a.reshape(1, -1)
    gf = row2(g_final)

    outs = {name: [] for name in ("kp", "vp", "cp", "ks", "vs", "cs")}
    for l in range(depth):
        last = l == depth - 1
        lam_init = _lambda_init(l)
        lam = (jnp.exp(jnp.sum(lambda_q1[l] * lambda_k1[l])) - jnp.exp(jnp.sum(lambda_q2[l] * lambda_k2[l])) + lam_init)
        scal = jnp.concatenate([slopes, lam.reshape(1).astype(F32)])
        post_scale = 1.0 - lam_init
        w1g, w1u, w1d = w_ffn1_gate[l], w_ffn1_up[l], w_ffn1_down[l].astype(BF16)
        w2g, w2u, w2d = w_ffn2_gate[l], w_ffn2_up[l], w_ffn2_down[l].astype(BF16)
        w_in16 = w_in[l].astype(BF16)
        w_out16 = w_out[l].astype(BF16)
        w_pg16 = w_ple_gate[l].astype(BF16)
        w_pp16 = w_ple_proj[l].astype(BF16)
        tf = _pick(w1g.shape[1], 512)
        tn_proj = math.gcd(math.gcd(qk_cols, attn_width), math.gcd(conv_width, 1024))
        g_sub = row2(g_subln[l])

        def ffn(hp, hs, g, wg, wu, wd):
            hid_p, hid_s = _ffn_up(hp, hs, row2(g), wg, wu, tm, tf)
            return _ffn_down(hid_p, hid_s, wd, hp, hs, tm, _pick(d_model, 512))

        h_p, h_s = ffn(h_p, h_s, g_ffn1[l], w1g, w1u, w1d)
        (q_p, q_s), (k_p, k_s), (v_p, v_s), (glu_p, glu_s) = _in_proj(
            h_p, h_s, row2(g_mix[l]), w_in16, qk_cols, attn_width, conv_width, tm, tn_proj)
        conv_args = (w_dw[l], row2(b_dw[l]), row2(g_conv_norm[l]), row2(b_conv_norm[l]))

        attn_p = _prompt_attn(q_p, k_p, v_p, g_sub, scal, batch, seq, n_heads, post_scale, tq)
        zero_hist = jnp.zeros((batch, CONV_KERNEL - 1, conv_width), F32)
        conv_p, hist_p = _conv_module(glu_p.reshape(batch, seq, conv_width), zero_hist, *conv_args)
        outs["kp"].append(k_p.reshape(batch, seq, n_heads, 2 * HEAD_DIM_QK))
        outs["vp"].append(v_p.reshape(batch, seq, n_heads, HEAD_DIM_V))
        outs["cp"].append(hist_p)

        k_new = k_s.reshape(dec_batch, dec_seq, n_heads, 2 * HEAD_DIM_QK)
        v_new = v_s.reshape(dec_batch, dec_seq, n_heads, HEAD_DIM_V)
        q5 = q_s.reshape(dec_batch, dec_seq, n_heads, 2, HEAD_DIM_QK)
        q2 = jnp.einsum("bthmd,mn->bhmtnd", q5, jnp.eye(2, dtype=BF16))
        q2 = q2.reshape(dec_batch, 2 * n_heads * dec_seq, 2 * HEAD_DIM_QK)
        attn_s = _decode_attn(q2, cache_k, cache_v, l, k_new, v_new, g_sub, page_table, scal, post_scale)
        conv_s, hist_s = _conv_module(glu_s.reshape(dec_batch, dec_seq, conv_width), state_conv[l], *conv_args)
        outs["ks"].append(k_new)
        outs["vs"].append(v_new)
        outs["cs"].append(hist_s)

        h_p, h_s = _out_proj(attn_p, conv_p.reshape(n_p, conv_width), attn_s.reshape(n_s, attn_width),
                             conv_s.reshape(n_s, conv_width), w_out16, h_p, h_s, tm, _pick(d_model, 1024))
        h_p, h_s = ffn(h_p, h_s, g_ffn2[l], w2g, w2u, w2d)
        h_p, h_s = _ple(h_p, p_prompt[l].reshape(n_p, -1), h_s, p_sample[l].reshape(n_s, -1), row2(g_ple[l]),
                        w_pg16, w_pp16, gf, last, min(tm, 512))

    y_prompt = h_p.reshape(batch, seq, d_model)
    y_sample = h_s.reshape(dec_batch, dec_seq, d_model)
    return (y_prompt, y_sample, jnp.stack(outs["kp"]), jnp.stack(outs["vp"]), jnp.stack(outs["cp"]),
            jnp.stack(outs["ks"]), jnp.stack(outs["vs"]), jnp.stack(outs["cs"]))
```

```python
import functools
import math

import jax
import jax.numpy as jnp
from jax import lax
from jax.experimental import pallas as pl
from jax.experimental.pallas import tpu as pltpu

F32 = jnp.float32
BF16 = jnp.bfloat16

HEAD_DIM_QK = 64
HEAD_DIM_V = 2 * HEAD_DIM_QK
CONV_GROUPS = 8
CONV_KERNEL = 31
PAGE_SIZE = 128
RMS_EPS = 1e-6
LN_EPS = 1e-5
NEG_INF = -1e30
LOG2E = math.log2(math.e)

V7X_VMEM_BYTES = 64 * 1024 * 1024
LANES = 128
SUBLANES = 8
CONV_PAD = 32
CONV_PARTIALS = 4
CONV_CHUNK_ROWS = 64
CONV_CHUNKS_PER_ITER = 4
DECODE_PAGES_PER_STEP = 16
ATTN_Q_TILE = 1024
ATTN_COL_GROUPS = 4
TOKEN_TILE = 1024


def _vmem_limit(nbytes):
    return int(min(V7X_VMEM_BYTES - 4 * 1024 * 1024, nbytes + 16 * 1024 * 1024))


def _params(sem, nbytes):
    return pltpu.CompilerParams(dimension_semantics=sem, vmem_limit_bytes=_vmem_limit(nbytes))


def _rms(x, g):
    ms = jnp.mean(x * x, axis=-1, keepdims=True)
    return x * lax.rsqrt(ms + RMS_EPS) * g


def _sigmoid(x):
    return 1.0 / (1.0 + jnp.exp(-x))


def _pick(n, pref):
    t = min(n, pref)
    assert n % t == 0, (n, t)
    return t


def _row_spec(tm, cols):
    return pl.BlockSpec((tm, cols), lambda i, j: (i, 0))


def _col_spec(rows, tn, col0=0):
    off = col0 // tn
    return pl.BlockSpec((rows, tn), lambda i, j: (0, j + off))


def _tile_spec(tm, tn):
    return pl.BlockSpec((tm, tn), lambda i, j: (i, j))


def _const_spec(rows, cols):
    return pl.BlockSpec((rows, cols), lambda i, j: (0, 0))


def _side_tile_spec(rows, tn, n_col_tiles):
    return pl.BlockSpec((rows, tn), lambda i, j: (0, jnp.where(i == 0, j, n_col_tiles - 1)))


def _stage_norm(x_ref, xs_ref, g_ref, n_ref):
    i, j = pl.program_id(0), pl.program_id(1)
    tm = x_ref.shape[0]

    @pl.when(j == 0)
    def _():
        n_ref[0:tm, :] = _rms(x_ref[...], g_ref[...]).astype(BF16)

    @pl.when(jnp.logical_and(i == 0, j == 0))
    def _():
        n_ref[tm:, :] = _rms(xs_ref[...], g_ref[...]).astype(BF16)


def _norm_scratch(tm, ms, d):
    return [pltpu.VMEM((tm + ms, d), BF16)]


def _store_to(ref):
    def store(v):
        ref[...] = v
    return store


def _rows_with_side(n_ref, tm, compute, store_main, store_side):
    i = pl.program_id(0)

    @pl.when(i != 0)
    def _():
        store_main(compute(n_ref[0:tm, :]))

    @pl.when(i == 0)
    def _():
        r = compute(n_ref[...])
        store_main(r[:tm])
        store_side(r[tm:])


def _ffn_up_kernel(x_ref, xs_ref, g_ref, wg_ref, wu_ref, o_ref, os_ref, n_ref):
    _stage_norm(x_ref, xs_ref, g_ref, n_ref)

    wg = wg_ref[...].astype(BF16)
    wu = wu_ref[...].astype(BF16)

    def act(n):
        a = jnp.dot(n, wg, preferred_element_type=F32)
        u = jnp.dot(n, wu, preferred_element_type=F32)
        return (a * _sigmoid(a) * u).astype(BF16)

    _rows_with_side(n_ref, x_ref.shape[0], act, _store_to(o_ref), _store_to(os_ref))


def _ffn_up(x, xs, g, wg, wu, tm, tf):
    m, d = x.shape
    ms = xs.shape[0]
    f = wg.shape[1]
    nj = f // tf
    wb = wg.dtype.itemsize
    nbytes = 2 * tm * d * 4 + tm * d * 2 + 2 * 2 * d * tf * wb + 2 * d * tf * 2 + 2 * tm * tf * 2 + 3 * tm * tf * 4
    return pl.pallas_call(
        _ffn_up_kernel,
        grid=(m // tm, nj),
        in_specs=[_row_spec(tm, d), _const_spec(ms, d), _const_spec(1, d), _col_spec(d, tf), _col_spec(d, tf)],
        out_specs=[_tile_spec(tm, tf), _side_tile_spec(ms, tf, nj)],
        out_shape=[jax.ShapeDtypeStruct((m, f), BF16), jax.ShapeDtypeStruct((ms, f), BF16)],
        scratch_shapes=_norm_scratch(tm, ms, d),
        compiler_params=_params(("arbitrary", "arbitrary"), nbytes),
        name="ffn_up",
    )(x, xs, g, wg, wu)


def _ffn_down_kernel(h_ref, hs_ref, w_ref, r_ref, rs_ref, o_ref, os_ref):
    w = w_ref[...].astype(BF16)
    o_ref[...] = r_ref[...] + 0.5 * jnp.dot(h_ref[...], w, preferred_element_type=F32)

    @pl.when(pl.program_id(0) == 0)
    def _():
        os_ref[...] = rs_ref[...] + 0.5 * jnp.dot(hs_ref[...], w, preferred_element_type=F32)


def _ffn_down(hid, hid_s, wd, res, res_s, tm, tn):
    m, f = hid.shape
    ms = hid_s.shape[0]
    d = wd.shape[1]
    nj = d // tn
    nbytes = 2 * tm * f * 2 + 2 * f * tn * wd.dtype.itemsize + f * tn * 2 + 4 * tm * tn * 4 + tm * tn * 4
    return pl.pallas_call(
        _ffn_down_kernel,
        grid=(m // tm, nj),
        in_specs=[_row_spec(tm, f), _const_spec(ms, f), _col_spec(f, tn), _tile_spec(tm, tn),
                  _side_tile_spec(ms, tn, nj)],
        out_specs=[_tile_spec(tm, tn), _side_tile_spec(ms, tn, nj)],
        out_shape=[jax.ShapeDtypeStruct((m, d), F32), jax.ShapeDtypeStruct((ms, d), F32)],
        compiler_params=_params(("arbitrary", "arbitrary"), nbytes),
        name="ffn_down",
    )(hid, hid_s, wd, res, res_s)


def _qkv_proj_kernel(q_scale, x_ref, xs_ref, g_ref, w_ref, q_ref, k_ref, v_ref, qs_ref, ks_ref, vs_ref, n_ref):
    _stage_norm(x_ref, xs_ref, g_ref, n_ref)
    j = pl.program_id(1)
    w = w_ref[...].astype(BF16)

    def proj(n):
        return jnp.dot(n, w, preferred_element_type=F32)

    def store(outs):
        def _store(u):
            for jj, o_ref in enumerate(outs):
                @pl.when(j == jj)
                def _():
                    o_ref[...] = (u * q_scale if jj == 0 else u).astype(o_ref.dtype)
        return _store

    _rows_with_side(n_ref, x_ref.shape[0], proj, store((q_ref, k_ref, v_ref)), store((qs_ref, ks_ref, vs_ref)))


def _glu_proj_kernel(x_ref, xs_ref, g_ref, wa_ref, wg_ref, o_ref, os_ref, n_ref):
    _stage_norm(x_ref, xs_ref, g_ref, n_ref)

    wa = wa_ref[...].astype(BF16)
    wg = wg_ref[...].astype(BF16)

    def glu(n):
        a = jnp.dot(n, wa, preferred_element_type=F32)
        gt = jnp.dot(n, wg, preferred_element_type=F32)
        return a * _sigmoid(gt)

    _rows_with_side(n_ref, x_ref.shape[0], glu, _store_to(o_ref), _store_to(os_ref))


def _in_proj(x, xs, g, w_in, qk_cols, attn_width, conv_width, tm, tn):
    m, d = x.shape
    ms = xs.shape[0]
    q_scale = HEAD_DIM_QK ** -0.5 * LOG2E
    nbytes = 2 * tm * d * 4 + tm * d * 2 + 4 * d * tn * w_in.dtype.itemsize + 2 * d * tn * 2 + 6 * tm * tn * 4

    def call(kernel, col0s, cols, dtype, name):
        nj = cols // tn
        return pl.pallas_call(
            kernel,
            grid=(m // tm, nj),
            in_specs=[_row_spec(tm, d), _const_spec(ms, d), _const_spec(1, d)] + [_col_spec(d, tn, c) for c in col0s],
            out_specs=[_tile_spec(tm, tn), _side_tile_spec(ms, tn, nj)],
            out_shape=[jax.ShapeDtypeStruct((m, cols), dtype), jax.ShapeDtypeStruct((ms, cols), dtype)],
            scratch_shapes=_norm_scratch(tm, ms, d),
            compiler_params=_params(("arbitrary", "arbitrary"), nbytes),
            name=name,
        )(x, xs, g, *([w_in] * len(col0s)))

    assert qk_cols == attn_width
    width = qk_cols
    whole = pl.BlockSpec((tm, width), lambda i, j: (i, 0))
    whole_side = _const_spec(ms, width)
    dtypes = (BF16, F32, F32)
    qkv_bytes = 2 * tm * d * 4 + tm * d * 2 + 2 * d * width * w_in.dtype.itemsize + (2 * 10 + 4) * tm * width
    qkv = pl.pallas_call(
        functools.partial(_qkv_proj_kernel, q_scale),
        grid=(m // tm, 3),
        in_specs=[_row_spec(tm, d), _const_spec(ms, d), _const_spec(1, d), _col_spec(d, width)],
        out_specs=[whole] * 3 + [whole_side] * 3,
        out_shape=[jax.ShapeDtypeStruct((m, width), t) for t in dtypes]
        + [jax.ShapeDtypeStruct((ms, width), t) for t in dtypes],
        scratch_shapes=_norm_scratch(tm, ms, d),
        compiler_params=_params(("arbitrary", "arbitrary"), qkv_bytes),
        name="qkv_proj",
    )(x, xs, g, w_in)
    c0 = 2 * qk_cols + attn_width
    glu = call(_glu_proj_kernel, [c0, c0 + conv_width], conv_width, F32, "glu_proj")
    return (qkv[0], qkv[3]), (qkv[1], qkv[4]), (qkv[2], qkv[5]), glu


def _split_maps(q):
    lane = lax.broadcasted_iota(jnp.int32, q.shape, 1)
    zero = jnp.zeros_like(q)
    return jnp.concatenate([jnp.where(lane < HEAD_DIM_QK, q, zero), jnp.where(lane >= HEAD_DIM_QK, q, zero)], axis=0)


def _subln(o, g, post_scale):
    ms = jnp.mean(o * o, axis=-1, keepdims=True)
    return o * lax.rsqrt(ms + RMS_EPS) * g * post_scale


def _prompt_attn_kernel(post_scale, tq, n_heads, n_groups, sc_ref, q_ref, k_ref, v_ref, g_ref, o_ref, qq_ref, k16_ref,
                        vt_ref, *stat_refs):
    h = pl.program_id(1)
    qi = pl.program_id(2)
    slope = sc_ref[h]
    lam = sc_ref[n_heads]
    gw = 2 * tq // n_groups
    assert tq % gw == 0
    groups = [stat_refs[3 * g:3 * g + 3] for g in range(n_groups)]

    @pl.when(qi == 0)
    def _():
        k16_ref[...] = k_ref[...].astype(BF16)
        vt_ref[...] = v_ref[...].T.astype(BF16)

    qq_ref[...] = _split_maps(q_ref[...])
    for m_ref, l_ref, acc_ref in groups:
        m_ref[...] = jnp.full(m_ref.shape, NEG_INF, F32)
        l_ref[...] = jnp.zeros(l_ref.shape, F32)
        acc_ref[...] = jnp.zeros(acc_ref.shape, F32)

    def block(ki, masked):
        start = pl.multiple_of(ki * tq, tq)
        key = lax.broadcasted_iota(jnp.int32, (tq, LANES), 0)
        kbias = slope * ((ki - qi) * tq + key).astype(F32)
        kbias = jnp.concatenate([kbias] * (gw // LANES), axis=1)
        for g, (m_ref, l_ref, acc_ref) in enumerate(groups):
            q0 = (g * gw) % tq
            nk = q0 + gw if masked else tq
            k = k16_ref[pl.ds(start, nk), :]
            vt = vt_ref[:, pl.ds(start, nk)]
            qg = qq_ref[g * gw:(g + 1) * gw, :]
            s = lax.dot_general(k, qg, (((1,), (1,)), ((), ())), preferred_element_type=F32) + kbias[:nk]
            if masked:
                keyf = lax.broadcasted_iota(jnp.int32, (nk, gw), 0)
                query = q0 + lax.broadcasted_iota(jnp.int32, (nk, gw), 1)
                s = jnp.where(keyf <= query, s, NEG_INF)
            m_prev = m_ref[...]
            m_new = jnp.maximum(m_prev, jnp.max(s, axis=0, keepdims=True))
            alpha = jnp.exp2(m_prev - m_new)
            p = jnp.exp2(s - m_new)
            l_ref[...] = alpha * l_ref[...] + jnp.sum(p, axis=0, keepdims=True)
            acc_ref[...] = alpha * acc_ref[...] + jnp.dot(vt, p.astype(BF16), preferred_element_type=F32)
            m_ref[...] = m_new

    def body(ki, carry):
        block(ki, False)
        return carry

    lax.fori_loop(0, qi, body, 0)
    block(qi, True)
    o = jnp.concatenate([acc_ref[...] / l_ref[...] for _, l_ref, acc_ref in groups], axis=1)
    o = o[:, :tq] - lam * o[:, tq:]
    ms = jnp.mean(o * o, axis=0, keepdims=True)
    o = o * (lax.rsqrt(ms + RMS_EPS) * post_scale)
    o_ref[...] = (o.T * g_ref[...]).astype(o_ref.dtype)


def _prompt_attn(q16, k32, v32, g_subln, scal, batch, seq, n_heads, post_scale, tq):
    nq = seq // tq
    gw = 2 * tq // ATTN_COL_GROUPS
    grid_spec = pltpu.PrefetchScalarGridSpec(
        num_scalar_prefetch=1,
        grid=(batch, n_heads, nq),
        in_specs=[
            pl.BlockSpec((tq, HEAD_DIM_V), lambda b, h, i, sc: (b * nq + i, h)),
            pl.BlockSpec((seq, HEAD_DIM_V), lambda b, h, i, sc: (b, h)),
            pl.BlockSpec((seq, HEAD_DIM_V), lambda b, h, i, sc: (b, h)),
            pl.BlockSpec((1, HEAD_DIM_V), lambda b, h, i, sc: (0, 0)),
        ],
        out_specs=pl.BlockSpec((tq, HEAD_DIM_V), lambda b, h, i, sc: (b * nq + i, h)),
        scratch_shapes=[
            pltpu.VMEM((2 * tq, HEAD_DIM_V), BF16),
            pltpu.VMEM((seq, HEAD_DIM_V), BF16),
            pltpu.VMEM((HEAD_DIM_V, seq), BF16),
        ] + [pltpu.VMEM((1, gw), F32), pltpu.VMEM((1, gw), F32), pltpu.VMEM((HEAD_DIM_V, gw), F32)] * ATTN_COL_GROUPS,
    )
    nbytes = 4 * seq * HEAD_DIM_V * 4 + 2 * seq * HEAD_DIM_V * 2 + 8 * 2 * tq * tq * 4
    return pl.pallas_call(
        functools.partial(_prompt_attn_kernel, post_scale, tq, n_heads, ATTN_COL_GROUPS),
        grid_spec=grid_spec,
        out_shape=jax.ShapeDtypeStruct((batch * seq, n_heads * HEAD_DIM_V), BF16),
        compiler_params=_params(("parallel", "parallel", "arbitrary"), nbytes),
        name="prompt_attn",
    )(scal, q16, k32, v32, g_subln)


def _decode_attn_kernel(post_scale, n_heads, t_new, past_len, grp, pt_ref, sc_ref, q_ref, *refs):
    k_refs, v_refs = refs[:grp], refs[grp:2 * grp]
    kn_ref, vn_ref, g_ref, o_ref, bias_ref, slope_ref, m_ref, l_ref, acc_ref = refs[2 * grp:]
    p = pl.program_id(1)
    rows = 2 * n_heads * t_new
    rows_per_head = 2 * t_new
    cols = PAGE_SIZE * n_heads

    def head_geometry(ncols):
        row = lax.broadcasted_iota(jnp.int32, (rows, ncols), 0)
        col = lax.broadcasted_iota(jnp.int32, (rows, ncols), 1)
        rhead = row // rows_per_head
        slope = jnp.zeros((rows, ncols), F32)
        for hh in range(n_heads):
            slope = jnp.where(rhead == hh, sc_ref[hh], slope)
        return row, col, rhead == col % n_heads, col // n_heads, slope

    @pl.when(p == 0)
    def _():
        m_ref[...] = jnp.full(m_ref.shape, NEG_INF, F32)
        l_ref[...] = jnp.zeros(l_ref.shape, F32)
        acc_ref[...] = jnp.zeros(acc_ref.shape, F32)
        _, _, same_head, tok, slope = head_geometry(cols)
        bias_ref[...] = jnp.where(same_head, slope * tok.astype(F32), NEG_INF)
        slope_ref[...] = slope[:, 0:1]

    q = q_ref[...]

    def update(blocks):
        scores = [lax.dot_general(q, k2, (((1,), (1,)), ((), ())), preferred_element_type=F32) + bias
                  for k2, _, bias, _ in blocks]
        m_prev = m_ref[...]
        m_new = m_prev
        for s, (_, _, _, shift) in zip(scores, blocks):
            m_new = jnp.maximum(m_new, jnp.max(s, axis=-1, keepdims=True) + shift)
        alpha = jnp.exp2(m_prev - m_new)
        l_new = alpha * l_ref[...]
        acc = alpha * acc_ref[...]
        for s, (_, v2, _, shift) in zip(scores, blocks):
            pr = jnp.exp2(s - (m_new - shift))
            l_new = l_new + jnp.sum(pr, axis=-1, keepdims=True)
            acc = acc + jnp.dot(pr.astype(BF16), v2, preferred_element_type=F32)
        l_ref[...] = l_new
        acc_ref[...] = acc
        m_ref[...] = m_new

    blocks = []
    for g in range(grp):
        k2 = k_refs[g][...].reshape(cols, HEAD_DIM_V).astype(BF16)
        v2 = v_refs[g][...].reshape(cols, HEAD_DIM_V).astype(BF16)
        page_start = ((p * grp + g) * PAGE_SIZE - past_len).astype(F32)
        blocks.append((k2, v2, bias_ref[...], slope_ref[...] * page_start))
    update(blocks)

    @pl.when(p == pl.num_programs(1) - 1)
    def _():
        n_new = t_new * n_heads
        ncols = pl.cdiv(n_new, LANES) * LANES
        pad = jnp.zeros((ncols - n_new, HEAD_DIM_V), F32)
        kn = jnp.concatenate([kn_ref[...].reshape(n_new, HEAD_DIM_V), pad], axis=0).astype(BF16)
        vn = jnp.concatenate([vn_ref[...].reshape(n_new, HEAD_DIM_V), pad], axis=0).astype(BF16)
        row, col, same_head, tok, slope = head_geometry(ncols)
        visible = jnp.where(same_head, tok, ncols) <= row % t_new
        update([(kn, vn, jnp.where(visible, slope * tok.astype(F32), NEG_INF), 0.0)])
        o = acc_ref[...] / l_ref[...]
        lam = sc_ref[n_heads]
        gain = g_ref[...]
        outs = []
        for hh in range(n_heads):
            r0 = hh * rows_per_head
            oh = o[r0:r0 + t_new] - lam * o[r0 + t_new:r0 + 2 * t_new]
            outs.append(_subln(oh, gain, post_scale))
        o_ref[...] = jnp.concatenate(outs, axis=1).astype(o_ref.dtype)


def _decode_attn(q2, cache_k, cache_v, layer, k_new, v_new, g_subln, page_table, scal, post_scale):
    batch, n_pages = page_table.shape
    _, t_new, n_heads, _ = k_new.shape
    rows = 2 * n_heads * t_new
    width = n_heads * HEAD_DIM_V
    past_len = n_pages * PAGE_SIZE
    grp = _pick(n_pages, DECODE_PAGES_PER_STEP)
    page_block = (None, None, PAGE_SIZE, n_heads, HEAD_DIM_V)

    def page_spec(g):
        return pl.BlockSpec(page_block, lambda b, p, pt, sc: (layer, pt[b, p * grp + g], 0, 0, 0))

    new_spec = pl.BlockSpec((None, t_new, n_heads, HEAD_DIM_V), lambda b, p, pt, sc: (b, 0, 0, 0))
    grid_spec = pltpu.PrefetchScalarGridSpec(
        num_scalar_prefetch=2,
        grid=(batch, n_pages // grp),
        in_specs=[pl.BlockSpec((None, rows, HEAD_DIM_V), lambda b, p, pt, sc: (b, 0, 0))]
        + [page_spec(g) for g in range(grp)] * 2
        + [new_spec, new_spec, pl.BlockSpec((1, HEAD_DIM_V), lambda b, p, pt, sc: (0, 0))],
        out_specs=pl.BlockSpec((None, t_new, width), lambda b, p, pt, sc: (b, 0, 0)),
        scratch_shapes=[
            pltpu.VMEM((rows, PAGE_SIZE * n_heads), F32),
            pltpu.VMEM((rows, 1), F32),
            pltpu.VMEM((rows, 1), F32),
            pltpu.VMEM((rows, 1), F32),
            pltpu.VMEM((rows, HEAD_DIM_V), F32),
        ],
    )
    page_bytes = PAGE_SIZE * width * 4
    nbytes = 4 * grp * page_bytes + 6 * rows * PAGE_SIZE * n_heads * 4
    return pl.pallas_call(
        functools.partial(_decode_attn_kernel, post_scale, n_heads, t_new, past_len, grp),
        grid_spec=grid_spec,
        out_shape=jax.ShapeDtypeStruct((batch, t_new, width), BF16),
        compiler_params=_params(("parallel", "arbitrary"), nbytes),
        name="decode_attn",
    )(page_table, scal, q2, *([cache_k] * grp), *([cache_v] * grp), k_new, v_new, g_subln)


def _conv_kernel(seq, chunk, u_ref, prev_ref, w_ref, bdw_ref, gcn_ref, bcn_ref, o_ref, new_ref, ext_ref):
    hist = CONV_KERNEL - 1
    lead = CONV_PAD - hist
    cw = u_ref.shape[-1]
    ext_ref[0:lead, :] = jnp.zeros((lead, cw), F32)
    ext_ref[lead:CONV_PAD, :] = prev_ref[0]
    ext_ref[CONV_PAD:CONV_PAD + seq, :] = u_ref[0]
    new_ref[0] = ext_ref[lead + seq:CONV_PAD + seq, :]
    bdw = bdw_ref[...]
    gcn = gcn_ref[...]
    bcn = bcn_ref[...]

    def body(c, _):
        r0 = c * chunk if isinstance(c, int) else pl.multiple_of(c * chunk, chunk)
        parts = [None] * CONV_PARTIALS
        for j in range(CONV_KERNEL):
            term = w_ref[j:j + 1, :] * ext_ref[pl.ds(r0 + (lead + j), chunk), :]
            slot = j % CONV_PARTIALS
            parts[slot] = term if parts[slot] is None else parts[slot] + term
        y = functools.reduce(lambda a, b: a + b, parts) + bdw
        outs = []
        for c0 in range(0, cw, LANES):
            grp = slice(c0, c0 + LANES)
            mu = jnp.mean(y[:, grp], axis=-1, keepdims=True)
            yc = y[:, grp] - mu
            var = jnp.mean(yc * yc, axis=-1, keepdims=True)
            yn = yc * lax.rsqrt(var + LN_EPS) * gcn[:, grp] + bcn[:, grp]
            outs.append(yn * _sigmoid(yn))
        o_ref[0, pl.ds(r0, chunk), :] = jnp.concatenate(outs, axis=1).astype(o_ref.dtype)
        return 0

    n_chunks = seq // chunk
    if n_chunks <= CONV_CHUNKS_PER_ITER:
        for c in range(n_chunks):
            body(c, 0)
    else:
        lax.fori_loop(0, n_chunks, body, 0, unroll=math.gcd(n_chunks, CONV_CHUNKS_PER_ITER))


def _conv_module(u, prev, w_dw, b_dw, g_cn, b_cn):
    batch, seq, width = u.shape
    assert width // CONV_GROUPS == LANES
    chunk = _pick(seq, CONV_CHUNK_ROWS)
    cw = min(width, LANES * (CONV_CHUNK_ROWS // chunk))
    assert width % cw == 0
    hist = CONV_KERNEL - 1
    vec = pl.BlockSpec((1, cw), lambda b, c: (0, c))
    return pl.pallas_call(
        functools.partial(_conv_kernel, seq, chunk),
        grid=(batch, width // cw),
        in_specs=[
            pl.BlockSpec((1, seq, cw), lambda b, c: (b, 0, c)),
            pl.BlockSpec((1, hist, cw), lambda b, c: (b, 0, c)),
            pl.BlockSpec((CONV_KERNEL, cw), lambda b, c: (0, c)),
            vec, vec, vec,
        ],
        out_specs=[
            pl.BlockSpec((1, seq, cw), lambda b, c: (b, 0, c)),
            pl.BlockSpec((1, hist, cw), lambda b, c: (b, 0, c)),
        ],
        out_shape=[jax.ShapeDtypeStruct((batch, seq, width), BF16), jax.ShapeDtypeStruct((batch, hist, width), F32)],
        scratch_shapes=[pltpu.VMEM((CONV_PAD + seq, cw), F32)],
        compiler_params=_params(("parallel", "parallel"), 8 * (seq + CONV_PAD) * cw * 4),
        name="conv_module",
    )(u, prev, w_dw, b_dw, g_cn, b_cn)


def _out_proj_kernel(ka, a_ref, b_ref, as_ref, bs_ref, w_ref, r_ref, rs_ref, o_ref, os_ref):
    wa = w_ref[0:ka, :].astype(BF16)
    wb = w_ref[ka:, :].astype(BF16)

    def proj(a, b):
        return jnp.dot(a, wa, preferred_element_type=F32) + jnp.dot(b, wb, preferred_element_type=F32)

    o_ref[...] = r_ref[...] + proj(a_ref[...], b_ref[...])

    @pl.when(pl.program_id(0) == 0)
    def _():
        os_ref[...] = rs_ref[...] + proj(as_ref[...], bs_ref[...])


def _out_proj(a, b, a_s, b_s, w, res, res_s, tm, tn):
    m, ka = a.shape
    ms = a_s.shape[0]
    kb = b.shape[1]
    d = w.shape[1]
    nj = d // tn
    nbytes = 2 * tm * (ka + kb) * 2 + 2 * (ka + kb) * tn * w.dtype.itemsize + (ka + kb) * tn * 2 + 5 * tm * tn * 4
    return pl.pallas_call(
        functools.partial(_out_proj_kernel, ka),
        grid=(m // tm, nj),
        in_specs=[_row_spec(tm, ka), _row_spec(tm, kb), _const_spec(ms, ka), _const_spec(ms, kb),
                  _col_spec(ka + kb, tn), _tile_spec(tm, tn), _side_tile_spec(ms, tn, nj)],
        out_specs=[_tile_spec(tm, tn), _side_tile_spec(ms, tn, nj)],
        out_shape=[jax.ShapeDtypeStruct((m, d), F32), jax.ShapeDtypeStruct((ms, d), F32)],
        compiler_params=_params(("arbitrary", "arbitrary"), nbytes),
        name="out_proj",
    )(a, b, a_s, b_s, w, res, res_s)


def _ple_kernel(final_norm, h_ref, p_ref, hs_ref, ps_ref, g_ref, wg_ref, wp_ref, gf_ref, o_ref, os_ref):
    def ple(h, p):
        n = _rms(h, g_ref[...]).astype(BF16)
        gate = _sigmoid(jnp.dot(n, wg_ref[...], preferred_element_type=F32))
        proj = jnp.dot(p.astype(BF16), wp_ref[...], preferred_element_type=F32)
        h = h + proj * gate
        return _rms(h, gf_ref[...]) if final_norm else h

    o_ref[...] = ple(h_ref[...], p_ref[...])

    @pl.when(pl.program_id(0) == 0)
    def _():
        os_ref[...] = ple(hs_ref[...], ps_ref[...])


def _ple(h, p, h_s, p_s, g, wg, wp, g_final, final_norm, tm):
    m, d = h.shape
    ms = h_s.shape[0]
    dp = p.shape[1]
    nbytes = 4 * tm * d * 4 + 2 * d * d * 2 + 2 * dp * d * 2 + 2 * tm * dp * 4 + 4 * tm * d * 4
    row = lambda cols: pl.BlockSpec((tm, cols), lambda i: (i, 0))
    const = lambda rows, cols: pl.BlockSpec((rows, cols), lambda i: (0, 0))
    return pl.pallas_call(
        functools.partial(_ple_kernel, final_norm),
        grid=(m // tm,),
        in_specs=[row(d), row(dp), const(ms, d), const(ms, dp), const(1, d), const(d, d), const(dp, d), const(1, d)],
        out_specs=[row(d), const(ms, d)],
        out_shape=[jax.ShapeDtypeStruct((m, d), F32), jax.ShapeDtypeStruct((ms, d), F32)],
        compiler_params=_params(("arbitrary",), nbytes),
        name="ple",
    )(h, p, h_s, p_s, g, wg, wp, g_final)


def _lambda_init(layer_idx):
    return 0.8 - 0.6 * math.exp(-0.3 * layer_idx)


def kernel(x_prompt, x_sample, cache_k, cache_v, state_conv, page_table, p_prompt, p_sample, g_ffn1, w_ffn1_gate, w_ffn1_up, w_ffn1_down, g_mix, w_in, lambda_q1, lambda_k1, lambda_q2, lambda_k2, g_subln, w_dw, b_dw, g_conv_norm, b_conv_norm, w_out, g_ffn2, w_ffn2_gate, w_ffn2_up, w_ffn2_down, g_ple, w_ple_gate, w_ple_proj, g_final):
    depth = w_in.shape[0]
    batch, seq, d_model = x_prompt.shape
    dec_batch, dec_seq, _ = x_sample.shape
    n_heads = cache_k.shape[3]
    assert cache_k.shape[4] == 2 * HEAD_DIM_QK and cache_v.shape[4] == HEAD_DIM_V and cache_k.shape[2] == PAGE_SIZE
    qk_cols = n_heads * 2 * HEAD_DIM_QK
    attn_width = n_heads * HEAD_DIM_V
    conv_width = d_model - attn_width
    n_p, n_s = batch * seq, dec_batch * dec_seq
    h_p = x_prompt.reshape(n_p, d_model)
    h_s = x_sample.reshape(n_s, d_model)
    tm = _pick(n_p, TOKEN_TILE)
    tq = _pick(seq, ATTN_Q_TILE)
    slopes = jnp.asarray([LOG2E * 2.0 ** (-8.0 * (h + 1) / n_heads) for h in range(n_heads)], F32)
    row2 = lambda a: a.reshape(1, -1)
    gf = row2(g_final)

    outs = {name: [] for name in ("kp", "vp", "cp", "ks", "vs", "cs")}
    for l in range(depth):
        last = l == depth - 1
        lam_init = _lambda_init(l)
        lam = (jnp.exp(jnp.sum(lambda_q1[l] * lambda_k1[l])) - jnp.exp(jnp.sum(lambda_q2[l] * lambda_k2[l])) + lam_init)
        scal = jnp.concatenate([slopes, lam.reshape(1).astype(F32)])
        post_scale = 1.0 - lam_init
        w1g, w1u, w1d = w_ffn1_gate[l], w_ffn1_up[l], w_ffn1_down[l].astype(BF16)
        w2g, w2u, w2d = w_ffn2_gate[l], w_ffn2_up[l], w_ffn2_down[l].astype(BF16)
        w_in16 = w_in[l].astype(BF16)
        w_out16 = w_out[l].astype(BF16)
        w_pg16 = w_ple_gate[l].astype(BF16)
        w_pp16 = w_ple_proj[l].astype(BF16)
        tf = _pick(w1g.shape[1], 512)
        tn_proj = math.gcd(math.gcd(qk_cols, attn_width), math.gcd(conv_width, 1024))
        g_sub = row2(g_subln[l])

        def ffn(hp, hs, g, wg, wu, wd):
            hid_p, hid_s = _ffn_up(hp, hs, row2(g), wg, wu, tm, tf)
            return _ffn_down(hid_p, hid_s, wd, hp, hs, tm, _pick(d_model, 512))

        h_p, h_s = ffn(h_p, h_s, g_ffn1[l], w1g, w1u, w1d)
        (q_p, q_s), (k_p, k_s), (v_p, v_s), (glu_p, glu_s) = _in_proj(
            h_p, h_s, row2(g_mix[l]), w_in16, qk_cols, attn_width, conv_width, tm, tn_proj)
        conv_args = (w_dw[l], row2(b_dw[l]), row2(g_conv_norm[l]), row2(b_conv_norm[l]))

        attn_p = _prompt_attn(q_p, k_p, v_p, g_sub, scal, batch, seq, n_heads, post_scale, tq)
        zero_hist = jnp.zeros((batch, CONV_KERNEL - 1, conv_width), F32)
        conv_p, hist_p = _conv_module(glu_p.reshape(batch, seq, conv_width), zero_hist, *conv_args)
        outs["kp"].append(k_p.reshape(batch, seq, n_heads, 2 * HEAD_DIM_QK))
        outs["vp"].append(v_p.reshape(batch, seq, n_heads, HEAD_DIM_V))
        outs["cp"].append(hist_p)

        k_new = k_s.reshape(dec_batch, dec_seq, n_heads, 2 * HEAD_DIM_QK)
        v_new = v_s.reshape(dec_batch, dec_seq, n_heads, HEAD_DIM_V)
        q5 = q_s.reshape(dec_batch, dec_seq, n_heads, 2, HEAD_DIM_QK)
        q2 = jnp.einsum("bthmd,mn->bhmtnd", q5, jnp.eye(2, dtype=BF16))
        q2 = q2.reshape(dec_batch, 2 * n_heads * dec_seq, 2 * HEAD_DIM_QK)
        attn_s = _decode_attn(q2, cache_k, cache_v, l, k_new, v_new, g_sub, page_table, scal, post_scale)
        conv_s, hist_s = _conv_module(glu_s.reshape(dec_batch, dec_seq, conv_width), state_conv[l], *conv_args)
        outs["ks"].append(k_new)
        outs["vs"].append(v_new)
        outs["cs"].append(hist_s)

        h_p, h_s = _out_proj(attn_p, conv_p.reshape(n_p, conv_width), attn_s.reshape(n_s, attn_width),
                             conv_s.reshape(n_s, conv_width), w_out16, h_p, h_s, min(tm, 512), d_model)
        h_p, h_s = ffn(h_p, h_s, g_ffn2[l], w2g, w2u, w2d)
        h_p, h_s = _ple(h_p, p_prompt[l].reshape(n_p, -1), h_s, p_sample[l].reshape(n_s, -1), row2(g_ple[l]),
                        w_pg16, w_pp16, gf, last, min(tm, 512))

    y_prompt = h_p.reshape(batch, seq, d_model)
    y_sample = h_s.reshape(dec_batch, dec_seq, d_model)
    return (y_prompt, y_sample, jnp.stack(outs["kp"]), jnp.stack(outs["vp"]), jnp.stack(outs["cp"]),
            jnp.stack(outs["ks"]), jnp.stack(outs["vs"]), jnp.stack(outs["cs"]))
```

```python
import functools
import math

import jax
import jax.numpy as jnp
from jax import lax
from jax.experimental import pallas as pl
from jax.experimental.pallas import tpu as pltpu

F32 = jnp.float32
BF16 = jnp.bfloat16

HEAD_DIM_QK = 64
HEAD_DIM_V = 2 * HEAD_DIM_QK
CONV_GROUPS = 8
CONV_KERNEL = 31
PAGE_SIZE = 128
RMS_EPS = 1e-6
LN_EPS = 1e-5
NEG_INF = -1e30
LOG2E = math.log2(math.e)

V7X_VMEM_BYTES = 64 * 1024 * 1024
LANES = 128
SUBLANES = 8
CONV_PAD = 32
CONV_PARTIALS = 4
CONV_CHUNK_ROWS = 64
CONV_CHUNKS_PER_ITER = 4
DECODE_PAGES_PER_STEP = 16
ATTN_Q_TILE = 1024
ATTN_COL_GROUPS = 4
TOKEN_TILE = 1024


def _vmem_limit(nbytes):
    return int(min(V7X_VMEM_BYTES - 4 * 1024 * 1024, nbytes + 16 * 1024 * 1024))


def _params(sem, nbytes):
    return pltpu.CompilerParams(dimension_semantics=sem, vmem_limit_bytes=_vmem_limit(nbytes))


def _rms(x, g):
    ms = jnp.mean(x * x, axis=-1, keepdims=True)
    return x * lax.rsqrt(ms + RMS_EPS) * g


def _sigmoid(x):
    return 1.0 / (1.0 + jnp.exp(-x))


def _pick(n, pref):
    t = min(n, pref)
    assert n % t == 0, (n, t)
    return t


def _row_spec(tm, cols):
    return pl.BlockSpec((tm, cols), lambda i, j: (i, 0))


def _col_spec(rows, tn, col0=0):
    off = col0 // tn
    return pl.BlockSpec((rows, tn), lambda i, j: (0, j + off))


def _tile_spec(tm, tn):
    return pl.BlockSpec((tm, tn), lambda i, j: (i, j))


def _const_spec(rows, cols):
    return pl.BlockSpec((rows, cols), lambda i, j: (0, 0))


def _side_tile_spec(rows, tn, n_col_tiles):
    return pl.BlockSpec((rows, tn), lambda i, j: (0, jnp.where(i == 0, j, n_col_tiles - 1)))


def _stage_norm(x_ref, xs_ref, g_ref, n_ref, ns_ref):
    i, j = pl.program_id(0), pl.program_id(1)

    @pl.when(j == 0)
    def _():
        n_ref[...] = _rms(x_ref[...], g_ref[...]).astype(BF16)

    @pl.when(jnp.logical_and(i == 0, j == 0))
    def _():
        ns_ref[...] = _rms(xs_ref[...], g_ref[...]).astype(BF16)


def _norm_scratch(tm, ms, d):
    return [pltpu.VMEM((tm, d), BF16), pltpu.VMEM((ms, d), BF16)]


def _ffn_up_kernel(x_ref, xs_ref, g_ref, wg_ref, wu_ref, o_ref, os_ref, n_ref, ns_ref):
    _stage_norm(x_ref, xs_ref, g_ref, n_ref, ns_ref)

    wg = wg_ref[...].astype(BF16)
    wu = wu_ref[...].astype(BF16)

    def act(n):
        a = jnp.dot(n, wg, preferred_element_type=F32)
        u = jnp.dot(n, wu, preferred_element_type=F32)
        return (a * _sigmoid(a) * u).astype(BF16)

    o_ref[...] = act(n_ref[...])

    @pl.when(pl.program_id(0) == 0)
    def _():
        os_ref[...] = act(ns_ref[...])


def _ffn_up(x, xs, g, wg, wu, tm, tf):
    m, d = x.shape
    ms = xs.shape[0]
    f = wg.shape[1]
    nj = f // tf
    wb = wg.dtype.itemsize
    nbytes = 2 * tm * d * 4 + tm * d * 2 + 2 * 2 * d * tf * wb + 2 * d * tf * 2 + 2 * tm * tf * 2 + 3 * tm * tf * 4
    return pl.pallas_call(
        _ffn_up_kernel,
        grid=(m // tm, nj),
        in_specs=[_row_spec(tm, d), _const_spec(ms, d), _const_spec(1, d), _col_spec(d, tf), _col_spec(d, tf)],
        out_specs=[_tile_spec(tm, tf), _side_tile_spec(ms, tf, nj)],
        out_shape=[jax.ShapeDtypeStruct((m, f), BF16), jax.ShapeDtypeStruct((ms, f), BF16)],
        scratch_shapes=_norm_scratch(tm, ms, d),
        compiler_params=_params(("arbitrary", "arbitrary"), nbytes),
        name="ffn_up",
    )(x, xs, g, wg, wu)


def _ffn_down_kernel(h_ref, hs_ref, w_ref, r_ref, rs_ref, o_ref, os_ref):
    w = w_ref[...].astype(BF16)
    o_ref[...] = r_ref[...] + 0.5 * jnp.dot(h_ref[...], w, preferred_element_type=F32)

    @pl.when(pl.program_id(0) == 0)
    def _():
        os_ref[...] = rs_ref[...] + 0.5 * jnp.dot(hs_ref[...], w, preferred_element_type=F32)


def _ffn_down(hid, hid_s, wd, res, res_s, tm, tn):
    m, f = hid.shape
    ms = hid_s.shape[0]
    d = wd.shape[1]
    nj = d // tn
    nbytes = 2 * tm * f * 2 + 2 * f * tn * wd.dtype.itemsize + f * tn * 2 + 4 * tm * tn * 4 + tm * tn * 4
    return pl.pallas_call(
        _ffn_down_kernel,
        grid=(m // tm, nj),
        in_specs=[_row_spec(tm, f), _const_spec(ms, f), _col_spec(f, tn), _tile_spec(tm, tn),
                  _side_tile_spec(ms, tn, nj)],
        out_specs=[_tile_spec(tm, tn), _side_tile_spec(ms, tn, nj)],
        out_shape=[jax.ShapeDtypeStruct((m, d), F32), jax.ShapeDtypeStruct((ms, d), F32)],
        compiler_params=_params(("arbitrary", "arbitrary"), nbytes),
        name="ffn_down",
    )(hid, hid_s, wd, res, res_s)


def _qkv_proj_kernel(q_scale, x_ref, xs_ref, g_ref, w_ref, q_ref, k_ref, v_ref, qs_ref, ks_ref, vs_ref, n_ref, ns_ref):
    _stage_norm(x_ref, xs_ref, g_ref, n_ref, ns_ref)
    i, j = pl.program_id(0), pl.program_id(1)
    w = w_ref[...].astype(BF16)

    def emit(n, outs):
        u = jnp.dot(n, w, preferred_element_type=F32)
        for jj, o_ref in enumerate(outs):
            @pl.when(j == jj)
            def _():
                o_ref[...] = (u * q_scale if jj == 0 else u).astype(o_ref.dtype)

    emit(n_ref[...], (q_ref, k_ref, v_ref))

    @pl.when(i == 0)
    def _():
        emit(ns_ref[...], (qs_ref, ks_ref, vs_ref))


def _glu_proj_kernel(x_ref, xs_ref, g_ref, wa_ref, wg_ref, o_ref, os_ref, n_ref, ns_ref):
    _stage_norm(x_ref, xs_ref, g_ref, n_ref, ns_ref)

    wa = wa_ref[...].astype(BF16)
    wg = wg_ref[...].astype(BF16)

    def glu(n):
        a = jnp.dot(n, wa, preferred_element_type=F32)
        gt = jnp.dot(n, wg, preferred_element_type=F32)
        return a * _sigmoid(gt)

    o_ref[...] = glu(n_ref[...])

    @pl.when(pl.program_id(0) == 0)
    def _():
        os_ref[...] = glu(ns_ref[...])


def _in_proj(x, xs, g, w_in, qk_cols, attn_width, conv_width, tm, tn):
    m, d = x.shape
    ms = xs.shape[0]
    q_scale = HEAD_DIM_QK ** -0.5 * LOG2E
    nbytes = 2 * tm * d * 4 + tm * d * 2 + 4 * d * tn * w_in.dtype.itemsize + 2 * d * tn * 2 + 6 * tm * tn * 4

    def call(kernel, col0s, cols, dtype, name):
        nj = cols // tn
        return pl.pallas_call(
            kernel,
            grid=(m // tm, nj),
            in_specs=[_row_spec(tm, d), _const_spec(ms, d), _const_spec(1, d)] + [_col_spec(d, tn, c) for c in col0s],
            out_specs=[_tile_spec(tm, tn), _side_tile_spec(ms, tn, nj)],
            out_shape=[jax.ShapeDtypeStruct((m, cols), dtype), jax.ShapeDtypeStruct((ms, cols), dtype)],
            scratch_shapes=_norm_scratch(tm, ms, d),
            compiler_params=_params(("arbitrary", "arbitrary"), nbytes),
            name=name,
        )(x, xs, g, *([w_in] * len(col0s)))

    assert qk_cols == attn_width
    width = qk_cols
    whole = pl.BlockSpec((tm, width), lambda i, j: (i, 0))
    whole_side = _const_spec(ms, width)
    dtypes = (BF16, F32, F32)
    qkv_bytes = 2 * tm * d * 4 + tm * d * 2 + 2 * d * width * w_in.dtype.itemsize + (2 * 10 + 4) * tm * width
    qkv = pl.pallas_call(
        functools.partial(_qkv_proj_kernel, q_scale),
        grid=(m // tm, 3),
        in_specs=[_row_spec(tm, d), _const_spec(ms, d), _const_spec(1, d), _col_spec(d, width)],
        out_specs=[whole] * 3 + [whole_side] * 3,
        out_shape=[jax.ShapeDtypeStruct((m, width), t) for t in dtypes]
        + [jax.ShapeDtypeStruct((ms, width), t) for t in dtypes],
        scratch_shapes=_norm_scratch(tm, ms, d),
        compiler_params=_params(("arbitrary", "arbitrary"), qkv_bytes),
        name="qkv_proj",
    )(x, xs, g, w_in)
    c0 = 2 * qk_cols + attn_width
    glu = call(_glu_proj_kernel, [c0, c0 + conv_width], conv_width, F32, "glu_proj")
    return (qkv[0], qkv[3]), (qkv[1], qkv[4]), (qkv[2], qkv[5]), glu


def _split_maps(q):
    lane = lax.broadcasted_iota(jnp.int32, q.shape, 1)
    zero = jnp.zeros_like(q)
    return jnp.concatenate([jnp.where(lane < HEAD_DIM_QK, q, zero), jnp.where(lane >= HEAD_DIM_QK, q, zero)], axis=0)


def _subln(o, g, post_scale):
    ms = jnp.mean(o * o, axis=-1, keepdims=True)
    return o * lax.rsqrt(ms + RMS_EPS) * g * post_scale


def _prompt_attn_kernel(post_scale, tq, n_heads, n_groups, sc_ref, q_ref, k_ref, v_ref, g_ref, o_ref, qq_ref, k16_ref,
                        vt_ref, *stat_refs):
    h = pl.program_id(1)
    qi = pl.program_id(2)
    slope = sc_ref[h]
    lam = sc_ref[n_heads]
    gw = 2 * tq // n_groups
    assert tq % gw == 0
    groups = [stat_refs[3 * g:3 * g + 3] for g in range(n_groups)]

    @pl.when(qi == 0)
    def _():
        k16_ref[...] = k_ref[...].astype(BF16)
        vt_ref[...] = v_ref[...].T.astype(BF16)

    qq_ref[...] = _split_maps(q_ref[...])
    for m_ref, l_ref, acc_ref in groups:
        m_ref[...] = jnp.full(m_ref.shape, NEG_INF, F32)
        l_ref[...] = jnp.zeros(l_ref.shape, F32)
        acc_ref[...] = jnp.zeros(acc_ref.shape, F32)

    def block(ki, masked):
        start = pl.multiple_of(ki * tq, tq)
        key = lax.broadcasted_iota(jnp.int32, (tq, LANES), 0)
        kbias = slope * ((ki - qi) * tq + key).astype(F32)
        kbias = jnp.concatenate([kbias] * (gw // LANES), axis=1)
        for g, (m_ref, l_ref, acc_ref) in enumerate(groups):
            q0 = (g * gw) % tq
            nk = q0 + gw if masked else tq
            k = k16_ref[pl.ds(start, nk), :]
            vt = vt_ref[:, pl.ds(start, nk)]
            qg = qq_ref[g * gw:(g + 1) * gw, :]
            s = lax.dot_general(k, qg, (((1,), (1,)), ((), ())), preferred_element_type=F32) + kbias[:nk]
            if masked:
                keyf = lax.broadcasted_iota(jnp.int32, (nk, gw), 0)
                query = q0 + lax.broadcasted_iota(jnp.int32, (nk, gw), 1)
                s = jnp.where(keyf <= query, s, NEG_INF)
            m_prev = m_ref[...]
            m_new = jnp.maximum(m_prev, jnp.max(s, axis=0, keepdims=True))
            alpha = jnp.exp2(m_prev - m_new)
            p = jnp.exp2(s - m_new)
            l_ref[...] = alpha * l_ref[...] + jnp.sum(p, axis=0, keepdims=True)
            acc_ref[...] = alpha * acc_ref[...] + jnp.dot(vt, p.astype(BF16), preferred_element_type=F32)
            m_ref[...] = m_new

    def body(ki, carry):
        block(ki, False)
        return carry

    lax.fori_loop(0, qi, body, 0)
    block(qi, True)
    o = jnp.concatenate([acc_ref[...] / l_ref[...] for _, l_ref, acc_ref in groups], axis=1)
    o = o[:, :tq] - lam * o[:, tq:]
    ms = jnp.mean(o * o, axis=0, keepdims=True)
    o = o * (lax.rsqrt(ms + RMS_EPS) * post_scale)
    o_ref[...] = (o.T * g_ref[...]).astype(o_ref.dtype)


def _prompt_attn(q16, k32, v32, g_subln, scal, batch, seq, n_heads, post_scale, tq):
    nq = seq // tq
    gw = 2 * tq // ATTN_COL_GROUPS
    grid_spec = pltpu.PrefetchScalarGridSpec(
        num_scalar_prefetch=1,
        grid=(batch, n_heads, nq),
        in_specs=[
            pl.BlockSpec((tq, HEAD_DIM_V), lambda b, h, i, sc: (b * nq + i, h)),
            pl.BlockSpec((seq, HEAD_DIM_V), lambda b, h, i, sc: (b, h)),
            pl.BlockSpec((seq, HEAD_DIM_V), lambda b, h, i, sc: (b, h)),
            pl.BlockSpec((1, HEAD_DIM_V), lambda b, h, i, sc: (0, 0)),
        ],
        out_specs=pl.BlockSpec((tq, HEAD_DIM_V), lambda b, h, i, sc: (b * nq + i, h)),
        scratch_shapes=[
            pltpu.VMEM((2 * tq, HEAD_DIM_V), BF16),
            pltpu.VMEM((seq, HEAD_DIM_V), BF16),
            pltpu.VMEM((HEAD_DIM_V, seq), BF16),
        ] + [pltpu.VMEM((1, gw), F32), pltpu.VMEM((1, gw), F32), pltpu.VMEM((HEAD_DIM_V, gw), F32)] * ATTN_COL_GROUPS,
    )
    nbytes = 4 * seq * HEAD_DIM_V * 4 + 2 * seq * HEAD_DIM_V * 2 + 8 * 2 * tq * tq * 4
    return pl.pallas_call(
        functools.partial(_prompt_attn_kernel, post_scale, tq, n_heads, ATTN_COL_GROUPS),
        grid_spec=grid_spec,
        out_shape=jax.ShapeDtypeStruct((batch * seq, n_heads * HEAD_DIM_V), BF16),
        compiler_params=_params(("parallel", "parallel", "arbitrary"), nbytes),
        name="prompt_attn",
    )(scal, q16, k32, v32, g_subln)


def _decode_attn_kernel(post_scale, n_heads, t_new, past_len, grp, pt_ref, sc_ref, q_ref, *refs):
    k_refs, v_refs = refs[:grp], refs[grp:2 * grp]
    kn_ref, vn_ref, g_ref, o_ref, bias_ref, slope_ref, m_ref, l_ref, acc_ref = refs[2 * grp:]
    p = pl.program_id(1)
    rows = 2 * n_heads * t_new
    rows_per_head = 2 * t_new
    cols = PAGE_SIZE * n_heads

    def head_geometry(ncols):
        row = lax.broadcasted_iota(jnp.int32, (rows, ncols), 0)
        col = lax.broadcasted_iota(jnp.int32, (rows, ncols), 1)
        rhead = row // rows_per_head
        slope = jnp.zeros((rows, ncols), F32)
        for hh in range(n_heads):
            slope = jnp.where(rhead == hh, sc_ref[hh], slope)
        return row, col, rhead == col % n_heads, col // n_heads, slope

    @pl.when(p == 0)
    def _():
        m_ref[...] = jnp.full(m_ref.shape, NEG_INF, F32)
        l_ref[...] = jnp.zeros(l_ref.shape, F32)
        acc_ref[...] = jnp.zeros(acc_ref.shape, F32)
        _, _, same_head, tok, slope = head_geometry(cols)
        bias_ref[...] = jnp.where(same_head, slope * tok.astype(F32), NEG_INF)
        slope_ref[...] = slope[:, 0:1]

    q = q_ref[...]

    def update(blocks):
        scores = [lax.dot_general(q, k2, (((1,), (1,)), ((), ())), preferred_element_type=F32) + bias
                  for k2, _, bias, _ in blocks]
        m_prev = m_ref[...]
        m_new = m_prev
        for s, (_, _, _, shift) in zip(scores, blocks):
            m_new = jnp.maximum(m_new, jnp.max(s, axis=-1, keepdims=True) + shift)
        alpha = jnp.exp2(m_prev - m_new)
        l_new = alpha * l_ref[...]
        acc = alpha * acc_ref[...]
        for s, (_, v2, _, shift) in zip(scores, blocks):
            pr = jnp.exp2(s - (m_new - shift))
            l_new = l_new + jnp.sum(pr, axis=-1, keepdims=True)
            acc = acc + jnp.dot(pr.astype(BF16), v2, preferred_element_type=F32)
        l_ref[...] = l_new
        acc_ref[...] = acc
        m_ref[...] = m_new

    blocks = []
    for g in range(grp):
        k2 = k_refs[g][...].reshape(cols, HEAD_DIM_V).astype(BF16)
        v2 = v_refs[g][...].reshape(cols, HEAD_DIM_V).astype(BF16)
        page_start = ((p * grp + g) * PAGE_SIZE - past_len).astype(F32)
        blocks.append((k2, v2, bias_ref[...], slope_ref[...] * page_start))
    update(blocks)

    @pl.when(p == pl.num_programs(1) - 1)
    def _():
        n_new = t_new * n_heads
        ncols = pl.cdiv(n_new, LANES) * LANES
        pad = jnp.zeros((ncols - n_new, HEAD_DIM_V), F32)
        kn = jnp.concatenate([kn_ref[...].reshape(n_new, HEAD_DIM_V), pad], axis=0).astype(BF16)
        vn = jnp.concatenate([vn_ref[...].reshape(n_new, HEAD_DIM_V), pad], axis=0).astype(BF16)
        row, col, same_head, tok, slope = head_geometry(ncols)
        visible = jnp.where(same_head, tok, ncols) <= row % t_new
        update([(kn, vn, jnp.where(visible, slope * tok.astype(F32), NEG_INF), 0.0)])
        o = acc_ref[...] / l_ref[...]
        lam = sc_ref[n_heads]
        gain = g_ref[...]
        outs = []
        for hh in range(n_heads):
            r0 = hh * rows_per_head
            oh = o[r0:r0 + t_new] - lam * o[r0 + t_new:r0 + 2 * t_new]
            outs.append(_subln(oh, gain, post_scale))
        o_ref[...] = jnp.concatenate(outs, axis=1).astype(o_ref.dtype)


def _decode_attn(q2, cache_k, cache_v, layer, k_new, v_new, g_subln, page_table, scal, post_scale):
    batch, n_pages = page_table.shape
    _, t_new, n_heads, _ = k_new.shape
    rows = 2 * n_heads * t_new
    width = n_heads * HEAD_DIM_V
    past_len = n_pages * PAGE_SIZE
    grp = _pick(n_pages, DECODE_PAGES_PER_STEP)
    page_block = (None, None, PAGE_SIZE, n_heads, HEAD_DIM_V)

    def page_spec(g):
        return pl.BlockSpec(page_block, lambda b, p, pt, sc: (layer, pt[b, p * grp + g], 0, 0, 0))

    new_spec = pl.BlockSpec((None, t_new, n_heads, HEAD_DIM_V), lambda b, p, pt, sc: (b, 0, 0, 0))
    grid_spec = pltpu.PrefetchScalarGridSpec(
        num_scalar_prefetch=2,
        grid=(batch, n_pages // grp),
        in_specs=[pl.BlockSpec((None, rows, HEAD_DIM_V), lambda b, p, pt, sc: (b, 0, 0))]
        + [page_spec(g) for g in range(grp)] * 2
        + [new_spec, new_spec, pl.BlockSpec((1, HEAD_DIM_V), lambda b, p, pt, sc: (0, 0))],
        out_specs=pl.BlockSpec((None, t_new, width), lambda b, p, pt, sc: (b, 0, 0)),
        scratch_shapes=[
            pltpu.VMEM((rows, PAGE_SIZE * n_heads), F32),
            pltpu.VMEM((rows, 1), F32),
            pltpu.VMEM((rows, 1), F32),
            pltpu.VMEM((rows, 1), F32),
            pltpu.VMEM((rows, HEAD_DIM_V), F32),
        ],
    )
    page_bytes = PAGE_SIZE * width * 4
    nbytes = 4 * grp * page_bytes + 6 * rows * PAGE_SIZE * n_heads * 4
    return pl.pallas_call(
        functools.partial(_decode_attn_kernel, post_scale, n_heads, t_new, past_len, grp),
        grid_spec=grid_spec,
        out_shape=jax.ShapeDtypeStruct((batch, t_new, width), BF16),
        compiler_params=_params(("parallel", "arbitrary"), nbytes),
        name="decode_attn",
    )(page_table, scal, q2, *([cache_k] * grp), *([cache_v] * grp), k_new, v_new, g_subln)


def _conv_kernel(seq, chunk, u_ref, prev_ref, w_ref, bdw_ref, gcn_ref, bcn_ref, o_ref, new_ref, ext_ref):
    hist = CONV_KERNEL - 1
    lead = CONV_PAD - hist
    cw = u_ref.shape[-1]
    ext_ref[0:lead, :] = jnp.zeros((lead, cw), F32)
    ext_ref[lead:CONV_PAD, :] = prev_ref[0]
    ext_ref[CONV_PAD:CONV_PAD + seq, :] = u_ref[0]
    new_ref[0] = ext_ref[lead + seq:CONV_PAD + seq, :]
    bdw = bdw_ref[...]
    gcn = gcn_ref[...]
    bcn = bcn_ref[...]

    def body(c, _):
        r0 = c * chunk if isinstance(c, int) else pl.multiple_of(c * chunk, chunk)
        parts = [None] * CONV_PARTIALS
        for j in range(CONV_KERNEL):
            term = w_ref[j:j + 1, :] * ext_ref[pl.ds(r0 + (lead + j), chunk), :]
            slot = j % CONV_PARTIALS
            parts[slot] = term if parts[slot] is None else parts[slot] + term
        y = functools.reduce(lambda a, b: a + b, parts) + bdw
        outs = []
        for c0 in range(0, cw, LANES):
            grp = slice(c0, c0 + LANES)
            mu = jnp.mean(y[:, grp], axis=-1, keepdims=True)
            yc = y[:, grp] - mu
            var = jnp.mean(yc * yc, axis=-1, keepdims=True)
            yn = yc * lax.rsqrt(var + LN_EPS) * gcn[:, grp] + bcn[:, grp]
            outs.append(yn * _sigmoid(yn))
        o_ref[0, pl.ds(r0, chunk), :] = jnp.concatenate(outs, axis=1).astype(o_ref.dtype)
        return 0

    n_chunks = seq // chunk
    if n_chunks <= CONV_CHUNKS_PER_ITER:
        for c in range(n_chunks):
            body(c, 0)
    else:
        lax.fori_loop(0, n_chunks, body, 0, unroll=math.gcd(n_chunks, CONV_CHUNKS_PER_ITER))


def _conv_module(u, prev, w_dw, b_dw, g_cn, b_cn):
    batch, seq, width = u.shape
    assert width // CONV_GROUPS == LANES
    chunk = _pick(seq, CONV_CHUNK_ROWS)
    cw = min(width, LANES * (CONV_CHUNK_ROWS // chunk))
    assert width % cw == 0
    hist = CONV_KERNEL - 1
    vec = pl.BlockSpec((1, cw), lambda b, c: (0, c))
    return pl.pallas_call(
        functools.partial(_conv_kernel, seq, chunk),
        grid=(batch, width // cw),
        in_specs=[
            pl.BlockSpec((1, seq, cw), lambda b, c: (b, 0, c)),
            pl.BlockSpec((1, hist, cw), lambda b, c: (b, 0, c)),
            pl.BlockSpec((CONV_KERNEL, cw), lambda b, c: (0, c)),
            vec, vec, vec,
        ],
        out_specs=[
            pl.BlockSpec((1, seq, cw), lambda b, c: (b, 0, c)),
            pl.BlockSpec((1, hist, cw), lambda b, c: (b, 0, c)),
        ],
        out_shape=[jax.ShapeDtypeStruct((batch, seq, width), BF16), jax.ShapeDtypeStruct((batch, hist, width), F32)],
        scratch_shapes=[pltpu.VMEM((CONV_PAD + seq, cw), F32)],
        compiler_params=_params(("parallel", "parallel"), 8 * (seq + CONV_PAD) * cw * 4),
        name="conv_module",
    )(u, prev, w_dw, b_dw, g_cn, b_cn)


def _out_proj_kernel(ka, a_ref, b_ref, as_ref, bs_ref, w_ref, r_ref, rs_ref, o_ref, os_ref):
    wa = w_ref[0:ka, :].astype(BF16)
    wb = w_ref[ka:, :].astype(BF16)

    def proj(a, b):
        return jnp.dot(a, wa, preferred_element_type=F32) + jnp.dot(b, wb, preferred_element_type=F32)

    o_ref[...] = r_ref[...] + proj(a_ref[...], b_ref[...])

    @pl.when(pl.program_id(0) == 0)
    def _():
        os_ref[...] = rs_ref[...] + proj(as_ref[...], bs_ref[...])


def _out_proj(a, b, a_s, b_s, w, res, res_s, tm, tn):
    m, ka = a.shape
    ms = a_s.shape[0]
    kb = b.shape[1]
    d = w.shape[1]
    nj = d // tn
    nbytes = 2 * tm * (ka + kb) * 2 + 2 * (ka + kb) * tn * w.dtype.itemsize + (ka + kb) * tn * 2 + 5 * tm * tn * 4
    return pl.pallas_call(
        functools.partial(_out_proj_kernel, ka),
        grid=(m // tm, nj),
        in_specs=[_row_spec(tm, ka), _row_spec(tm, kb), _const_spec(ms, ka), _const_spec(ms, kb),
                  _col_spec(ka + kb, tn), _tile_spec(tm, tn), _side_tile_spec(ms, tn, nj)],
        out_specs=[_tile_spec(tm, tn), _side_tile_spec(ms, tn, nj)],
        out_shape=[jax.ShapeDtypeStruct((m, d), F32), jax.ShapeDtypeStruct((ms, d), F32)],
        compiler_params=_params(("arbitrary", "arbitrary"), nbytes),
        name="out_proj",
    )(a, b, a_s, b_s, w, res, res_s)


def _ple_kernel(final_norm, h_ref, p_ref, hs_ref, ps_ref, g_ref, wg_ref, wp_ref, gf_ref, o_ref, os_ref):
    def ple(h, p):
        n = _rms(h, g_ref[...]).astype(BF16)
        gate = _sigmoid(jnp.dot(n, wg_ref[...], preferred_element_type=F32))
        proj = jnp.dot(p.astype(BF16), wp_ref[...], preferred_element_type=F32)
        h = h + proj * gate
        return _rms(h, gf_ref[...]) if final_norm else h

    o_ref[...] = ple(h_ref[...], p_ref[...])

    @pl.when(pl.program_id(0) == 0)
    def _():
        os_ref[...] = ple(hs_ref[...], ps_ref[...])


def _ple(h, p, h_s, p_s, g, wg, wp, g_final, final_norm, tm):
    m, d = h.shape
    ms = h_s.shape[0]
    dp = p.shape[1]
    nbytes = 4 * tm * d * 4 + 2 * d * d * 2 + 2 * dp * d * 2 + 2 * tm * dp * 4 + 4 * tm * d * 4
    row = lambda cols: pl.BlockSpec((tm, cols), lambda i: (i, 0))
    const = lambda rows, cols: pl.BlockSpec((rows, cols), lambda i: (0, 0))
    return pl.pallas_call(
        functools.partial(_ple_kernel, final_norm),
        grid=(m // tm,),
        in_specs=[row(d), row(dp), const(ms, d), const(ms, dp), const(1, d), const(d, d), const(dp, d), const(1, d)],
        out_specs=[row(d), const(ms, d)],
        out_shape=[jax.ShapeDtypeStruct((m, d), F32), jax.ShapeDtypeStruct((ms, d), F32)],
        compiler_params=_params(("arbitrary",), nbytes),
        name="ple",
    )(h, p, h_s, p_s, g, wg, wp, g_final)


def _lambda_init(layer_idx):
    return 0.8 - 0.6 * math.exp(-0.3 * layer_idx)


def kernel(x_prompt, x_sample, cache_k, cache_v, state_conv, page_table, p_prompt, p_sample, g_ffn1, w_ffn1_gate, w_ffn1_up, w_ffn1_down, g_mix, w_in, lambda_q1, lambda_k1, lambda_q2, lambda_k2, g_subln, w_dw, b_dw, g_conv_norm, b_conv_norm, w_out, g_ffn2, w_ffn2_gate, w_ffn2_up, w_ffn2_down, g_ple, w_ple_gate, w_ple_proj, g_final):
    depth = w_in.shape[0]
    batch, seq, d_model = x_prompt.shape
    dec_batch, dec_seq, _ = x_sample.shape
    n_heads = cache_k.shape[3]
    assert cache_k.shape[4] == 2 * HEAD_DIM_QK and cache_v.shape[4] == HEAD_DIM_V and cache_k.shape[2] == PAGE_SIZE
    qk_cols = n_heads * 2 * HEAD_DIM_QK
    attn_width = n_heads * HEAD_DIM_V
    conv_width = d_model - attn_width
    n_p, n_s = batch * seq, dec_batch * dec_seq
    h_p = x_prompt.reshape(n_p, d_model)
    h_s = x_sample.reshape(n_s, d_model)
    tm = _pick(n_p, TOKEN_TILE)
    tq = _pick(seq, ATTN_Q_TILE)
    slopes = jnp.asarray([LOG2E * 2.0 ** (-8.0 * (h + 1) / n_heads) for h in range(n_heads)], F32)
    row2 = lambda a: a.reshape(1, -1)
    gf = row2(g_final)

    outs = {name: [] for name in ("kp", "vp", "cp", "ks", "vs", "cs")}
    for l in range(depth):
        last = l == depth - 1
        lam_init = _lambda_init(l)
        lam = (jnp.exp(jnp.sum(lambda_q1[l] * lambda_k1[l])) - jnp.exp(jnp.sum(lambda_q2[l] * lambda_k2[l])) + lam_init)
        scal = jnp.concatenate([slopes, lam.reshape(1).astype(F32)])
        post_scale = 1.0 - lam_init
        w1g, w1u, w1d = w_ffn1_gate[l], w_ffn1_up[l], w_ffn1_down[l].astype(BF16)
        w2g, w2u, w2d = w_ffn2_gate[l], w_ffn2_up[l], w_ffn2_down[l].astype(BF16)
        w_in16 = w_in[l].astype(BF16)
        w_out16 = w_out[l].astype(BF16)
        w_pg16 = w_ple_gate[l].astype(BF16)
        w_pp16 = w_ple_proj[l].astype(BF16)
        tf = _pick(w1g.shape[1], 512)
        tn_proj = math.gcd(math.gcd(qk_cols, attn_width), math.gcd(conv_width, 1024))
        g_sub = row2(g_subln[l])

        def ffn(hp, hs, g, wg, wu, wd):
            hid_p, hid_s = _ffn_up(hp, hs, row2(g), wg, wu, tm, tf)
            return _ffn_down(hid_p, hid_s, wd, hp, hs, tm, _pick(d_model, 512))

        h_p, h_s = ffn(h_p, h_s, g_ffn1[l], w1g, w1u, w1d)
        (q_p, q_s), (k_p, k_s), (v_p, v_s), (glu_p, glu_s) = _in_proj(
            h_p, h_s, row2(g_mix[l]), w_in16, qk_cols, attn_width, conv_width, tm, tn_proj)
        conv_args = (w_dw[l], row2(b_dw[l]), row2(g_conv_norm[l]), row2(b_conv_norm[l]))

        attn_p = _prompt_attn(q_p, k_p, v_p, g_sub, scal, batch, seq, n_heads, post_scale, tq)
        zero_hist = jnp.zeros((batch, CONV_KERNEL - 1, conv_width), F32)
        conv_p, hist_p = _conv_module(glu_p.reshape(batch, seq, conv_width), zero_hist, *conv_args)
        outs["kp"].append(k_p.reshape(batch, seq, n_heads, 2 * HEAD_DIM_QK))
        outs["vp"].append(v_p.reshape(batch, seq, n_heads, HEAD_DIM_V))
        outs["cp"].append(hist_p)

        k_new = k_s.reshape(dec_batch, dec_seq, n_heads, 2 * HEAD_DIM_QK)
        v_new = v_s.reshape(dec_batch, dec_seq, n_heads, HEAD_DIM_V)
        q5 = q_s.reshape(dec_batch, dec_seq, n_heads, 2, HEAD_DIM_QK)
        q2 = jnp.einsum("bthmd,mn->bhmtnd", q5, jnp.eye(2, dtype=BF16))
        q2 = q2.reshape(dec_batch, 2 * n_heads * dec_seq, 2 * HEAD_DIM_QK)
        attn_s = _decode_attn(q2, cache_k, cache_v, l, k_new, v_new, g_sub, page_table, scal, post_scale)
        conv_s, hist_s = _conv_module(glu_s.reshape(dec_batch, dec_seq, conv_width), state_conv[l], *conv_args)
        outs["ks"].append(k_new)
        outs["vs"].append(v_new)
        outs["cs"].append(hist_s)

        h_p, h_s = _out_proj(attn_p, conv_p.reshape(n_p, conv_width), attn_s.reshape(n_s, attn_width),
                             conv_s.reshape(n_s, conv_width), w_out16, h_p, h_s, min(tm, 512), d_model)
        h_p, h_s = ffn(h_p, h_s, g_ffn2[l], w2g, w2u, w2d)
        h_p, h_s = _ple(h_p, p_prompt[l].reshape(n_p, -1), h_s, p_sample[l].reshape(n_s, -1), row2(g_ple[l]),
                        w_pg16, w_pp16, gf, last, min(tm, 512))

    y_prompt = h_p.reshape(batch, seq, d_model)
    y_sample = h_s.reshape(dec_batch, dec_seq, d_model)
    return (y_prompt, y_sample, jnp.stack(outs["kp"]), jnp.stack(outs["vp"]), jnp.stack(outs["cp"]),
            jnp.stack(outs["ks"]), jnp.stack(outs["vs"]), jnp.stack(outs["cs"]))
```

```python
import functools
import math

import jax
import jax.numpy as jnp
from jax import lax
from jax.experimental import pallas as pl
from jax.experimental.pallas import tpu as pltpu

F32 = jnp.float32
BF16 = jnp.bfloat16

HEAD_DIM_QK = 64
HEAD_DIM_V = 2 * HEAD_DIM_QK
CONV_GROUPS = 8
CONV_KERNEL = 31
PAGE_SIZE = 128
RMS_EPS = 1e-6
LN_EPS = 1e-5
NEG_INF = -1e30
LOG2E = math.log2(math.e)

V7X_VMEM_BYTES = 64 * 1024 * 1024
LANES = 128
SUBLANES = 8
CONV_PAD = 32
CONV_PARTIALS = 4
CONV_CHUNK_ROWS = 64
CONV_CHUNKS_PER_ITER = 16
DECODE_PAGES_PER_STEP = 16
ATTN_Q_TILE = 1024
ATTN_COL_GROUPS = 4
TOKEN_TILE = 1024


def _vmem_limit(nbytes):
    return int(min(V7X_VMEM_BYTES - 4 * 1024 * 1024, nbytes + 16 * 1024 * 1024))


def _params(sem, nbytes):
    return pltpu.CompilerParams(dimension_semantics=sem, vmem_limit_bytes=_vmem_limit(nbytes))


def _rms(x, g):
    ms = jnp.mean(x * x, axis=-1, keepdims=True)
    return x * lax.rsqrt(ms + RMS_EPS) * g


def _sigmoid(x):
    return 1.0 / (1.0 + jnp.exp(-x))


def _pick(n, pref):
    t = min(n, pref)
    assert n % t == 0, (n, t)
    return t


def _row_spec(tm, cols):
    return pl.BlockSpec((tm, cols), lambda i, j: (i, 0))


def _col_spec(rows, tn, col0=0):
    off = col0 // tn
    return pl.BlockSpec((rows, tn), lambda i, j: (0, j + off))


def _tile_spec(tm, tn):
    return pl.BlockSpec((tm, tn), lambda i, j: (i, j))


def _const_spec(rows, cols):
    return pl.BlockSpec((rows, cols), lambda i, j: (0, 0))


def _side_tile_spec(rows, tn, n_col_tiles):
    return pl.BlockSpec((rows, tn), lambda i, j: (0, jnp.where(i == 0, j, n_col_tiles - 1)))


def _stage_norm(x_ref, xs_ref, g_ref, n_ref, ns_ref):
    i, j = pl.program_id(0), pl.program_id(1)

    @pl.when(j == 0)
    def _():
        n_ref[...] = _rms(x_ref[...], g_ref[...]).astype(BF16)

    @pl.when(jnp.logical_and(i == 0, j == 0))
    def _():
        ns_ref[...] = _rms(xs_ref[...], g_ref[...]).astype(BF16)


def _norm_scratch(tm, ms, d):
    return [pltpu.VMEM((tm, d), BF16), pltpu.VMEM((ms, d), BF16)]


def _ffn_up_kernel(x_ref, xs_ref, g_ref, wg_ref, wu_ref, o_ref, os_ref, n_ref, ns_ref):
    _stage_norm(x_ref, xs_ref, g_ref, n_ref, ns_ref)

    wg = wg_ref[...].astype(BF16)
    wu = wu_ref[...].astype(BF16)

    def act(n):
        a = jnp.dot(n, wg, preferred_element_type=F32)
        u = jnp.dot(n, wu, preferred_element_type=F32)
        return (a * _sigmoid(a) * u).astype(BF16)

    o_ref[...] = act(n_ref[...])

    @pl.when(pl.program_id(0) == 0)
    def _():
        os_ref[...] = act(ns_ref[...])


def _ffn_up(x, xs, g, wg, wu, tm, tf):
    m, d = x.shape
    ms = xs.shape[0]
    f = wg.shape[1]
    nj = f // tf
    wb = wg.dtype.itemsize
    nbytes = 2 * tm * d * 4 + tm * d * 2 + 2 * 2 * d * tf * wb + 2 * d * tf * 2 + 2 * tm * tf * 2 + 3 * tm * tf * 4
    return pl.pallas_call(
        _ffn_up_kernel,
        grid=(m // tm, nj),
        in_specs=[_row_spec(tm, d), _const_spec(ms, d), _const_spec(1, d), _col_spec(d, tf), _col_spec(d, tf)],
        out_specs=[_tile_spec(tm, tf), _side_tile_spec(ms, tf, nj)],
        out_shape=[jax.ShapeDtypeStruct((m, f), BF16), jax.ShapeDtypeStruct((ms, f), BF16)],
        scratch_shapes=_norm_scratch(tm, ms, d),
        compiler_params=_params(("arbitrary", "arbitrary"), nbytes),
        name="ffn_up",
    )(x, xs, g, wg, wu)


def _ffn_down_kernel(h_ref, hs_ref, w_ref, r_ref, rs_ref, o_ref, os_ref):
    w = w_ref[...].astype(BF16)
    o_ref[...] = r_ref[...] + 0.5 * jnp.dot(h_ref[...], w, preferred_element_type=F32)

    @pl.when(pl.program_id(0) == 0)
    def _():
        os_ref[...] = rs_ref[...] + 0.5 * jnp.dot(hs_ref[...], w, preferred_element_type=F32)


def _ffn_down(hid, hid_s, wd, res, res_s, tm, tn):
    m, f = hid.shape
    ms = hid_s.shape[0]
    d = wd.shape[1]
    nj = d // tn
    nbytes = 2 * tm * f * 2 + 2 * f * tn * wd.dtype.itemsize + f * tn * 2 + 4 * tm * tn * 4 + tm * tn * 4
    return pl.pallas_call(
        _ffn_down_kernel,
        grid=(m // tm, nj),
        in_specs=[_row_spec(tm, f), _const_spec(ms, f), _col_spec(f, tn), _tile_spec(tm, tn),
                  _side_tile_spec(ms, tn, nj)],
        out_specs=[_tile_spec(tm, tn), _side_tile_spec(ms, tn, nj)],
        out_shape=[jax.ShapeDtypeStruct((m, d), F32), jax.ShapeDtypeStruct((ms, d), F32)],
        compiler_params=_params(("arbitrary", "arbitrary"), nbytes),
        name="ffn_down",
    )(hid, hid_s, wd, res, res_s)


def _qkv_proj_kernel(q_scale, x_ref, xs_ref, g_ref, w_ref, q_ref, k_ref, v_ref, qs_ref, ks_ref, vs_ref, n_ref, ns_ref):
    _stage_norm(x_ref, xs_ref, g_ref, n_ref, ns_ref)
    i, j = pl.program_id(0), pl.program_id(1)
    w = w_ref[...].astype(BF16)

    def emit(n, outs):
        u = jnp.dot(n, w, preferred_element_type=F32)
        for jj, o_ref in enumerate(outs):
            @pl.when(j == jj)
            def _():
                o_ref[...] = (u * q_scale if jj == 0 else u).astype(o_ref.dtype)

    emit(n_ref[...], (q_ref, k_ref, v_ref))

    @pl.when(i == 0)
    def _():
        emit(ns_ref[...], (qs_ref, ks_ref, vs_ref))


def _glu_proj_kernel(x_ref, xs_ref, g_ref, wa_ref, wg_ref, o_ref, os_ref, n_ref, ns_ref):
    _stage_norm(x_ref, xs_ref, g_ref, n_ref, ns_ref)

    wa = wa_ref[...].astype(BF16)
    wg = wg_ref[...].astype(BF16)

    def glu(n):
        a = jnp.dot(n, wa, preferred_element_type=F32)
        gt = jnp.dot(n, wg, preferred_element_type=F32)
        return a * _sigmoid(gt)

    o_ref[...] = glu(n_ref[...])

    @pl.when(pl.program_id(0) == 0)
    def _():
        os_ref[...] = glu(ns_ref[...])


def _in_proj(x, xs, g, w_in, qk_cols, attn_width, conv_width, tm, tn):
    m, d = x.shape
    ms = xs.shape[0]
    q_scale = HEAD_DIM_QK ** -0.5 * LOG2E
    nbytes = 2 * tm * d * 4 + tm * d * 2 + 4 * d * tn * w_in.dtype.itemsize + 2 * d * tn * 2 + 6 * tm * tn * 4

    def call(kernel, col0s, cols, dtype, name):
        nj = cols // tn
        return pl.pallas_call(
            kernel,
            grid=(m // tm, nj),
            in_specs=[_row_spec(tm, d), _const_spec(ms, d), _const_spec(1, d)] + [_col_spec(d, tn, c) for c in col0s],
            out_specs=[_tile_spec(tm, tn), _side_tile_spec(ms, tn, nj)],
            out_shape=[jax.ShapeDtypeStruct((m, cols), dtype), jax.ShapeDtypeStruct((ms, cols), dtype)],
            scratch_shapes=_norm_scratch(tm, ms, d),
            compiler_params=_params(("arbitrary", "arbitrary"), nbytes),
            name=name,
        )(x, xs, g, *([w_in] * len(col0s)))

    assert qk_cols == attn_width
    width = qk_cols
    whole = pl.BlockSpec((tm, width), lambda i, j: (i, 0))
    whole_side = _const_spec(ms, width)
    dtypes = (BF16, F32, F32)
    qkv_bytes = 2 * tm * d * 4 + tm * d * 2 + 2 * d * width * w_in.dtype.itemsize + (2 * 10 + 4) * tm * width
    qkv = pl.pallas_call(
        functools.partial(_qkv_proj_kernel, q_scale),
        grid=(m // tm, 3),
        in_specs=[_row_spec(tm, d), _const_spec(ms, d), _const_spec(1, d), _col_spec(d, width)],
        out_specs=[whole] * 3 + [whole_side] * 3,
        out_shape=[jax.ShapeDtypeStruct((m, width), t) for t in dtypes]
        + [jax.ShapeDtypeStruct((ms, width), t) for t in dtypes],
        scratch_shapes=_norm_scratch(tm, ms, d),
        compiler_params=_params(("arbitrary", "arbitrary"), qkv_bytes),
        name="qkv_proj",
    )(x, xs, g, w_in)
    c0 = 2 * qk_cols + attn_width
    glu = call(_glu_proj_kernel, [c0, c0 + conv_width], conv_width, F32, "glu_proj")
    return (qkv[0], qkv[3]), (qkv[1], qkv[4]), (qkv[2], qkv[5]), glu


def _split_maps(q):
    lane = lax.broadcasted_iota(jnp.int32, q.shape, 1)
    zero = jnp.zeros_like(q)
    return jnp.concatenate([jnp.where(lane < HEAD_DIM_QK, q, zero), jnp.where(lane >= HEAD_DIM_QK, q, zero)], axis=0)


def _subln(o, g, post_scale):
    ms = jnp.mean(o * o, axis=-1, keepdims=True)
    return o * lax.rsqrt(ms + RMS_EPS) * g * post_scale


def _prompt_attn_kernel(post_scale, tq, n_heads, n_groups, sc_ref, q_ref, k_ref, v_ref, g_ref, o_ref, qq_ref, k16_ref,
                        vt_ref, *stat_refs):
    h = pl.program_id(1)
    qi = pl.program_id(2)
    slope = sc_ref[h]
    lam = sc_ref[n_heads]
    gw = 2 * tq // n_groups
    assert tq % gw == 0
    groups = [stat_refs[3 * g:3 * g + 3] for g in range(n_groups)]

    @pl.when(qi == 0)
    def _():
        k16_ref[...] = k_ref[...].astype(BF16)
        vt_ref[...] = v_ref[...].T.astype(BF16)

    qq_ref[...] = _split_maps(q_ref[...])
    for m_ref, l_ref, acc_ref in groups:
        m_ref[...] = jnp.full(m_ref.shape, NEG_INF, F32)
        l_ref[...] = jnp.zeros(l_ref.shape, F32)
        acc_ref[...] = jnp.zeros(acc_ref.shape, F32)

    def block(ki, masked):
        start = pl.multiple_of(ki * tq, tq)
        key = lax.broadcasted_iota(jnp.int32, (tq, LANES), 0)
        kbias = slope * ((ki - qi) * tq + key).astype(F32)
        kbias = jnp.concatenate([kbias] * (gw // LANES), axis=1)
        for g, (m_ref, l_ref, acc_ref) in enumerate(groups):
            q0 = (g * gw) % tq
            nk = q0 + gw if masked else tq
            k = k16_ref[pl.ds(start, nk), :]
            vt = vt_ref[:, pl.ds(start, nk)]
            qg = qq_ref[g * gw:(g + 1) * gw, :]
            s = lax.dot_general(k, qg, (((1,), (1,)), ((), ())), preferred_element_type=F32) + kbias[:nk]
            if masked:
                keyf = lax.broadcasted_iota(jnp.int32, (nk, gw), 0)
                query = q0 + lax.broadcasted_iota(jnp.int32, (nk, gw), 1)
                s = jnp.where(keyf <= query, s, NEG_INF)
            m_prev = m_ref[...]
            m_new = jnp.maximum(m_prev, jnp.max(s, axis=0, keepdims=True))
            alpha = jnp.exp2(m_prev - m_new)
            p = jnp.exp2(s - m_new)
            l_ref[...] = alpha * l_ref[...] + jnp.sum(p, axis=0, keepdims=True)
            acc_ref[...] = alpha * acc_ref[...] + jnp.dot(vt, p.astype(BF16), preferred_element_type=F32)
            m_ref[...] = m_new

    def body(ki, carry):
        block(ki, False)
        return carry

    lax.fori_loop(0, qi, body, 0)
    block(qi, True)
    o = jnp.concatenate([acc_ref[...] / l_ref[...] for _, l_ref, acc_ref in groups], axis=1)
    o = o[:, :tq] - lam * o[:, tq:]
    ms = jnp.mean(o * o, axis=0, keepdims=True)
    o = o * (lax.rsqrt(ms + RMS_EPS) * post_scale)
    o_ref[...] = (o.T * g_ref[...]).astype(o_ref.dtype)


def _prompt_attn(q16, k32, v32, g_subln, scal, batch, seq, n_heads, post_scale, tq):
    nq = seq // tq
    gw = 2 * tq // ATTN_COL_GROUPS
    grid_spec = pltpu.PrefetchScalarGridSpec(
        num_scalar_prefetch=1,
        grid=(batch, n_heads, nq),
        in_specs=[
            pl.BlockSpec((tq, HEAD_DIM_V), lambda b, h, i, sc: (b * nq + i, h)),
            pl.BlockSpec((seq, HEAD_DIM_V), lambda b, h, i, sc: (b, h)),
            pl.BlockSpec((seq, HEAD_DIM_V), lambda b, h, i, sc: (b, h)),
            pl.BlockSpec((1, HEAD_DIM_V), lambda b, h, i, sc: (0, 0)),
        ],
        out_specs=pl.BlockSpec((tq, HEAD_DIM_V), lambda b, h, i, sc: (b * nq + i, h)),
        scratch_shapes=[
            pltpu.VMEM((2 * tq, HEAD_DIM_V), BF16),
            pltpu.VMEM((seq, HEAD_DIM_V), BF16),
            pltpu.VMEM((HEAD_DIM_V, seq), BF16),
        ] + [pltpu.VMEM((1, gw), F32), pltpu.VMEM((1, gw), F32), pltpu.VMEM((HEAD_DIM_V, gw), F32)] * ATTN_COL_GROUPS,
    )
    nbytes = 4 * seq * HEAD_DIM_V * 4 + 2 * seq * HEAD_DIM_V * 2 + 8 * 2 * tq * tq * 4
    return pl.pallas_call(
        functools.partial(_prompt_attn_kernel, post_scale, tq, n_heads, ATTN_COL_GROUPS),
        grid_spec=grid_spec,
        out_shape=jax.ShapeDtypeStruct((batch * seq, n_heads * HEAD_DIM_V), BF16),
        compiler_params=_params(("parallel", "parallel", "arbitrary"), nbytes),
        name="prompt_attn",
    )(scal, q16, k32, v32, g_subln)


def _decode_attn_kernel(post_scale, n_heads, t_new, past_len, grp, pt_ref, sc_ref, q_ref, *refs):
    k_refs, v_refs = refs[:grp], refs[grp:2 * grp]
    kn_ref, vn_ref, g_ref, o_ref, bias_ref, slope_ref, m_ref, l_ref, acc_ref = refs[2 * grp:]
    p = pl.program_id(1)
    rows = 2 * n_heads * t_new
    rows_per_head = 2 * t_new
    cols = PAGE_SIZE * n_heads

    def head_geometry(ncols):
        row = lax.broadcasted_iota(jnp.int32, (rows, ncols), 0)
        col = lax.broadcasted_iota(jnp.int32, (rows, ncols), 1)
        rhead = row // rows_per_head
        slope = jnp.zeros((rows, ncols), F32)
        for hh in range(n_heads):
            slope = jnp.where(rhead == hh, sc_ref[hh], slope)
        return row, col, rhead == col % n_heads, col // n_heads, slope

    @pl.when(p == 0)
    def _():
        m_ref[...] = jnp.full(m_ref.shape, NEG_INF, F32)
        l_ref[...] = jnp.zeros(l_ref.shape, F32)
        acc_ref[...] = jnp.zeros(acc_ref.shape, F32)
        _, _, same_head, tok, slope = head_geometry(cols)
        bias_ref[...] = jnp.where(same_head, slope * tok.astype(F32), NEG_INF)
        slope_ref[...] = slope[:, 0:1]

    q = q_ref[...]

    def update(blocks):
        scores = [lax.dot_general(q, k2, (((1,), (1,)), ((), ())), preferred_element_type=F32) + bias
                  for k2, _, bias, _ in blocks]
        m_prev = m_ref[...]
        m_new = m_prev
        for s, (_, _, _, shift) in zip(scores, blocks):
            m_new = jnp.maximum(m_new, jnp.max(s, axis=-1, keepdims=True) + shift)
        alpha = jnp.exp2(m_prev - m_new)
        l_new = alpha * l_ref[...]
        acc = alpha * acc_ref[...]
        for s, (_, v2, _, shift) in zip(scores, blocks):
            pr = jnp.exp2(s - (m_new - shift))
            l_new = l_new + jnp.sum(pr, axis=-1, keepdims=True)
            acc = acc + jnp.dot(pr.astype(BF16), v2, preferred_element_type=F32)
        l_ref[...] = l_new
        acc_ref[...] = acc
        m_ref[...] = m_new

    blocks = []
    for g in range(grp):
        k2 = k_refs[g][...].reshape(cols, HEAD_DIM_V).astype(BF16)
        v2 = v_refs[g][...].reshape(cols, HEAD_DIM_V).astype(BF16)
        page_start = ((p * grp + g) * PAGE_SIZE - past_len).astype(F32)
        blocks.append((k2, v2, bias_ref[...], slope_ref[...] * page_start))
    update(blocks)

    @pl.when(p == pl.num_programs(1) - 1)
    def _():
        n_new = t_new * n_heads
        ncols = pl.cdiv(n_new, LANES) * LANES
        pad = jnp.zeros((ncols - n_new, HEAD_DIM_V), F32)
        kn = jnp.concatenate([kn_ref[...].reshape(n_new, HEAD_DIM_V), pad], axis=0).astype(BF16)
        vn = jnp.concatenate([vn_ref[...].reshape(n_new, HEAD_DIM_V), pad], axis=0).astype(BF16)
        row, col, same_head, tok, slope = head_geometry(ncols)
        visible = jnp.where(same_head, tok, ncols) <= row % t_new
        update([(kn, vn, jnp.where(visible, slope * tok.astype(F32), NEG_INF), 0.0)])
        o = acc_ref[...] / l_ref[...]
        lam = sc_ref[n_heads]
        gain = g_ref[...]
        outs = []
        for hh in range(n_heads):
            r0 = hh * rows_per_head
            oh = o[r0:r0 + t_new] - lam * o[r0 + t_new:r0 + 2 * t_new]
            outs.append(_subln(oh, gain, post_scale))
        o_ref[...] = jnp.concatenate(outs, axis=1).astype(o_ref.dtype)


def _decode_attn(q2, cache_k, cache_v, layer, k_new, v_new, g_subln, page_table, scal, post_scale):
    batch, n_pages = page_table.shape
    _, t_new, n_heads, _ = k_new.shape
    rows = 2 * n_heads * t_new
    width = n_heads * HEAD_DIM_V
    past_len = n_pages * PAGE_SIZE
    grp = _pick(n_pages, DECODE_PAGES_PER_STEP)
    page_block = (None, None, PAGE_SIZE, n_heads, HEAD_DIM_V)

    def page_spec(g):
        return pl.BlockSpec(page_block, lambda b, p, pt, sc: (layer, pt[b, p * grp + g], 0, 0, 0))

    new_spec = pl.BlockSpec((None, t_new, n_heads, HEAD_DIM_V), lambda b, p, pt, sc: (b, 0, 0, 0))
    grid_spec = pltpu.PrefetchScalarGridSpec(
        num_scalar_prefetch=2,
        grid=(batch, n_pages // grp),
        in_specs=[pl.BlockSpec((None, rows, HEAD_DIM_V), lambda b, p, pt, sc: (b, 0, 0))]
        + [page_spec(g) for g in range(grp)] * 2
        + [new_spec, new_spec, pl.BlockSpec((1, HEAD_DIM_V), lambda b, p, pt, sc: (0, 0))],
        out_specs=pl.BlockSpec((None, t_new, width), lambda b, p, pt, sc: (b, 0, 0)),
        scratch_shapes=[
            pltpu.VMEM((rows, PAGE_SIZE * n_heads), F32),
            pltpu.VMEM((rows, 1), F32),
            pltpu.VMEM((rows, 1), F32),
            pltpu.VMEM((rows, 1), F32),
            pltpu.VMEM((rows, HEAD_DIM_V), F32),
        ],
    )
    page_bytes = PAGE_SIZE * width * 4
    nbytes = 4 * grp * page_bytes + 6 * rows * PAGE_SIZE * n_heads * 4
    return pl.pallas_call(
        functools.partial(_decode_attn_kernel, post_scale, n_heads, t_new, past_len, grp),
        grid_spec=grid_spec,
        out_shape=jax.ShapeDtypeStruct((batch, t_new, width), BF16),
        compiler_params=_params(("parallel", "arbitrary"), nbytes),
        name="decode_attn",
    )(page_table, scal, q2, *([cache_k] * grp), *([cache_v] * grp), k_new, v_new, g_subln)


def _conv_kernel(seq, chunk, u_ref, prev_ref, w_ref, bdw_ref, gcn_ref, bcn_ref, o_ref, new_ref, ext_ref):
    hist = CONV_KERNEL - 1
    lead = CONV_PAD - hist
    cw = u_ref.shape[-1]
    ext_ref[0:lead, :] = jnp.zeros((lead, cw), F32)
    ext_ref[lead:CONV_PAD, :] = prev_ref[0]
    ext_ref[CONV_PAD:CONV_PAD + seq, :] = u_ref[0]
    new_ref[0] = ext_ref[lead + seq:CONV_PAD + seq, :]
    bdw = bdw_ref[...]
    gcn = gcn_ref[...]
    bcn = bcn_ref[...]

    def body(c, _):
        r0 = c * chunk if isinstance(c, int) else pl.multiple_of(c * chunk, chunk)
        parts = [None] * CONV_PARTIALS
        for j in range(CONV_KERNEL):
            term = w_ref[j:j + 1, :] * ext_ref[pl.ds(r0 + (lead + j), chunk), :]
            slot = j % CONV_PARTIALS
            parts[slot] = term if parts[slot] is None else parts[slot] + term
        y = functools.reduce(lambda a, b: a + b, parts) + bdw
        outs = []
        for c0 in range(0, cw, LANES):
            grp = slice(c0, c0 + LANES)
            mu = jnp.mean(y[:, grp], axis=-1, keepdims=True)
            yc = y[:, grp] - mu
            var = jnp.mean(yc * yc, axis=-1, keepdims=True)
            yn = yc * lax.rsqrt(var + LN_EPS) * gcn[:, grp] + bcn[:, grp]
            outs.append(yn * _sigmoid(yn))
        o_ref[0, pl.ds(r0, chunk), :] = jnp.concatenate(outs, axis=1).astype(o_ref.dtype)
        return 0

    n_chunks = seq // chunk
    if n_chunks <= CONV_CHUNKS_PER_ITER:
        for c in range(n_chunks):
            body(c, 0)
    else:
        lax.fori_loop(0, n_chunks, body, 0, unroll=math.gcd(n_chunks, CONV_CHUNKS_PER_ITER))


def _conv_module(u, prev, w_dw, b_dw, g_cn, b_cn):
    batch, seq, width = u.shape
    assert width // CONV_GROUPS == LANES
    chunk = _pick(seq, CONV_CHUNK_ROWS)
    cw = min(width, LANES * (CONV_CHUNK_ROWS // chunk))
    assert width % cw == 0
    hist = CONV_KERNEL - 1
    vec = pl.BlockSpec((1, cw), lambda b, c: (0, c))
    return pl.pallas_call(
        functools.partial(_conv_kernel, seq, chunk),
        grid=(batch, width // cw),
        in_specs=[
            pl.BlockSpec((1, seq, cw), lambda b, c: (b, 0, c)),
            pl.BlockSpec((1, hist, cw), lambda b, c: (b, 0, c)),
            pl.BlockSpec((CONV_KERNEL, cw), lambda b, c: (0, c)),
            vec, vec, vec,
        ],
        out_specs=[
            pl.BlockSpec((1, seq, cw), lambda b, c: (b, 0, c)),
            pl.BlockSpec((1, hist, cw), lambda b, c: (b, 0, c)),
        ],
        out_shape=[jax.ShapeDtypeStruct((batch, seq, width), BF16), jax.ShapeDtypeStruct((batch, hist, width), F32)],
        scratch_shapes=[pltpu.VMEM((CONV_PAD + seq, cw), F32)],
        compiler_params=_params(("parallel", "parallel"), 8 * (seq + CONV_PAD) * cw * 4),
        name="conv_module",
    )(u, prev, w_dw, b_dw, g_cn, b_cn)


def _out_proj_kernel(ka, a_ref, b_ref, as_ref, bs_ref, w_ref, r_ref, rs_ref, o_ref, os_ref):
    wa = w_ref[0:ka, :].astype(BF16)
    wb = w_ref[ka:, :].astype(BF16)

    def proj(a, b):
        return jnp.dot(a, wa, preferred_element_type=F32) + jnp.dot(b, wb, preferred_element_type=F32)

    o_ref[...] = r_ref[...] + proj(a_ref[...], b_ref[...])

    @pl.when(pl.program_id(0) == 0)
    def _():
        os_ref[...] = rs_ref[...] + proj(as_ref[...], bs_ref[...])


def _out_proj(a, b, a_s, b_s, w, res, res_s, tm, tn):
    m, ka = a.shape
    ms = a_s.shape[0]
    kb = b.shape[1]
    d = w.shape[1]
    nj = d // tn
    nbytes = 2 * tm * (ka + kb) * 2 + 2 * (ka + kb) * tn * w.dtype.itemsize + (ka + kb) * tn * 2 + 5 * tm * tn * 4
    return pl.pallas_call(
        functools.partial(_out_proj_kernel, ka),
        grid=(m // tm, nj),
        in_specs=[_row_spec(tm, ka), _row_spec(tm, kb), _const_spec(ms, ka), _const_spec(ms, kb),
                  _col_spec(ka + kb, tn), _tile_spec(tm, tn), _side_tile_spec(ms, tn, nj)],
        out_specs=[_tile_spec(tm, tn), _side_tile_spec(ms, tn, nj)],
        out_shape=[jax.ShapeDtypeStruct((m, d), F32), jax.ShapeDtypeStruct((ms, d), F32)],
        compiler_params=_params(("arbitrary", "arbitrary"), nbytes),
        name="out_proj",
    )(a, b, a_s, b_s, w, res, res_s)


def _ple_kernel(final_norm, h_ref, p_ref, hs_ref, ps_ref, g_ref, wg_ref, wp_ref, gf_ref, o_ref, os_ref):
    def ple(h, p):
        n = _rms(h, g_ref[...]).astype(BF16)
        gate = _sigmoid(jnp.dot(n, wg_ref[...], preferred_element_type=F32))
        proj = jnp.dot(p.astype(BF16), wp_ref[...], preferred_element_type=F32)
        h = h + proj * gate
        return _rms(h, gf_ref[...]) if final_norm else h

    o_ref[...] = ple(h_ref[...], p_ref[...])

    @pl.when(pl.program_id(0) == 0)
    def _():
        os_ref[...] = ple(hs_ref[...], ps_ref[...])


def _ple(h, p, h_s, p_s, g, wg, wp, g_final, final_norm, tm):
    m, d = h.shape
    ms = h_s.shape[0]
    dp = p.shape[1]
    nbytes = 4 * tm * d * 4 + 2 * d * d * 2 + 2 * dp * d * 2 + 2 * tm * dp * 4 + 4 * tm * d * 4
    row = lambda cols: pl.BlockSpec((tm, cols), lambda i: (i, 0))
    const = lambda rows, cols: pl.BlockSpec((rows, cols), lambda i: (0, 0))
    return pl.pallas_call(
        functools.partial(_ple_kernel, final_norm),
        grid=(m // tm,),
        in_specs=[row(d), row(dp), const(ms, d), const(ms, dp), const(1, d), const(d, d), const(dp, d), const(1, d)],
        out_specs=[row(d), const(ms, d)],
        out_shape=[jax.ShapeDtypeStruct((m, d), F32), jax.ShapeDtypeStruct((ms, d), F32)],
        compiler_params=_params(("arbitrary",), nbytes),
        name="ple",
    )(h, p, h_s, p_s, g, wg, wp, g_final)


def _lambda_init(layer_idx):
    return 0.8 - 0.6 * math.exp(-0.3 * layer_idx)


def kernel(x_prompt, x_sample, cache_k, cache_v, state_conv, page_table, p_prompt, p_sample, g_ffn1, w_ffn1_gate, w_ffn1_up, w_ffn1_down, g_mix, w_in, lambda_q1, lambda_k1, lambda_q2, lambda_k2, g_subln, w_dw, b_dw, g_conv_norm, b_conv_norm, w_out, g_ffn2, w_ffn2_gate, w_ffn2_up, w_ffn2_down, g_ple, w_ple_gate, w_ple_proj, g_final):
    depth = w_in.shape[0]
    batch, seq, d_model = x_prompt.shape
    dec_batch, dec_seq, _ = x_sample.shape
    n_heads = cache_k.shape[3]
    assert cache_k.shape[4] == 2 * HEAD_DIM_QK and cache_v.shape[4] == HEAD_DIM_V and cache_k.shape[2] == PAGE_SIZE
    qk_cols = n_heads * 2 * HEAD_DIM_QK
    attn_width = n_heads * HEAD_DIM_V
    conv_width = d_model - attn_width
    n_p, n_s = batch * seq, dec_batch * dec_seq
    h_p = x_prompt.reshape(n_p, d_model)
    h_s = x_sample.reshape(n_s, d_model)
    tm = _pick(n_p, TOKEN_TILE)
    tq = _pick(seq, ATTN_Q_TILE)
    slopes = jnp.asarray([LOG2E * 2.0 ** (-8.0 * (h + 1) / n_heads) for h in range(n_heads)], F32)
    row2 = lambda a: a.reshape(1, -1)
    gf = row2(g_final)

    outs = {name: [] for name in ("kp", "vp", "cp", "ks", "vs", "cs")}
    for l in range(depth):
        last = l == depth - 1
        lam_init = _lambda_init(l)
        lam = (jnp.exp(jnp.sum(lambda_q1[l] * lambda_k1[l])) - jnp.exp(jnp.sum(lambda_q2[l] * lambda_k2[l])) + lam_init)
        scal = jnp.concatenate([slopes, lam.reshape(1).astype(F32)])
        post_scale = 1.0 - lam_init
        w1g, w1u, w1d = w_ffn1_gate[l], w_ffn1_up[l], w_ffn1_down[l].astype(BF16)
        w2g, w2u, w2d = w_ffn2_gate[l], w_ffn2_up[l], w_ffn2_down[l].astype(BF16)
        w_in16 = w_in[l].astype(BF16)
        w_out16 = w_out[l].astype(BF16)
        w_pg16 = w_ple_gate[l].astype(BF16)
        w_pp16 = w_ple_proj[l].astype(BF16)
        tf = _pick(w1g.shape[1], 512)
        tn_proj = math.gcd(math.gcd(qk_cols, attn_width), math.gcd(conv_width, 1024))
        g_sub = row2(g_subln[l])

        def ffn(hp, hs, g, wg, wu, wd):
            hid_p, hid_s = _ffn_up(hp, hs, row2(g), wg, wu, tm, tf)
            return _ffn_down(hid_p, hid_s, wd, hp, hs, tm, _pick(d_model, 512))

        h_p, h_s = ffn(h_p, h_s, g_ffn1[l], w1g, w1u, w1d)
        (q_p, q_s), (k_p, k_s), (v_p, v_s), (glu_p, glu_s) = _in_proj(
            h_p, h_s, row2(g_mix[l]), w_in16, qk_cols, attn_width, conv_width, tm, tn_proj)
        conv_args = (w_dw[l], row2(b_dw[l]), row2(g_conv_norm[l]), row2(b_conv_norm[l]))

        attn_p = _prompt_attn(q_p, k_p, v_p, g_sub, scal, batch, seq, n_heads, post_scale, tq)
        zero_hist = jnp.zeros((batch, CONV_KERNEL - 1, conv_width), F32)
        conv_p, hist_p = _conv_module(glu_p.reshape(batch, seq, conv_width), zero_hist, *conv_args)
        outs["kp"].append(k_p.reshape(batch, seq, n_heads, 2 * HEAD_DIM_QK))
        outs["vp"].append(v_p.reshape(batch, seq, n_heads, HEAD_DIM_V))
        outs["cp"].append(hist_p)

        k_new = k_s.reshape(dec_batch, dec_seq, n_heads, 2 * HEAD_DIM_QK)
        v_new = v_s.reshape(dec_batch, dec_seq, n_heads, HEAD_DIM_V)
        q5 = q_s.reshape(dec_batch, dec_seq, n_heads, 2, HEAD_DIM_QK)
        q2 = jnp.einsum("bthmd,mn->bhmtnd", q5, jnp.eye(2, dtype=BF16))
        q2 = q2.reshape(dec_batch, 2 * n_heads * dec_seq, 2 * HEAD_DIM_QK)
        attn_s = _decode_attn(q2, cache_k, cache_v, l, k_new, v_new, g_sub, page_table, scal, post_scale)
        conv_s, hist_s = _conv_module(glu_s.reshape(dec_batch, dec_seq, conv_width), state_conv[l], *conv_args)
        outs["ks"].append(k_new)
        outs["vs"].append(v_new)
        outs["cs"].append(hist_s)

        h_p, h_s = _out_proj(attn_p, conv_p.reshape(n_p, conv_width), attn_s.reshape(n_s, attn_width),
                             conv_s.reshape(n_s, conv_width), w_out16, h_p, h_s, min(tm, 512), d_model)
        h_p, h_s = ffn(h_p, h_s, g_ffn2[l], w2g, w2u, w2d)
        h_p, h_s = _ple(h_p, p_prompt[l].reshape(n_p, -1), h_s, p_sample[l].reshape(n_s, -1), row2(g_ple[l]),
                        w_pg16, w_pp16, gf, last, min(tm, 512))

    y_prompt = h_p.reshape(batch, seq, d_model)
    y_sample = h_s.reshape(dec_batch, dec_seq, d_model)
    return (y_prompt, y_sample, jnp.stack(outs["kp"]), jnp.stack(outs["vp"]), jnp.stack(outs["cp"]),
            jnp.stack(outs["ks"]), jnp.stack(outs["vs"]), jnp.stack(outs["cs"]))
```

```python
import functools
import math

import jax
import jax.numpy as jnp
from jax import lax
from jax.experimental import pallas as pl
from jax.experimental.pallas import tpu as pltpu

F32 = jnp.float32
BF16 = jnp.bfloat16

HEAD_DIM_QK = 64
HEAD_DIM_V = 2 * HEAD_DIM_QK
CONV_GROUPS = 8
CONV_KERNEL = 31
PAGE_SIZE = 128
RMS_EPS = 1e-6
LN_EPS = 1e-5
NEG_INF = -1e30
LOG2E = math.log2(math.e)

V7X_VMEM_BYTES = 64 * 1024 * 1024
LANES = 128
SUBLANES = 8
CONV_PAD = 32
CONV_PARTIALS = 4
CONV_CHUNK_ROWS = 64
CONV_CHUNKS_PER_ITER = 16
DECODE_PAGES_PER_STEP = 16
ATTN_Q_TILE = 2048
ATTN_COL_GROUPS = 8
TOKEN_TILE = 1024


def _vmem_limit(nbytes):
    return int(min(V7X_VMEM_BYTES - 4 * 1024 * 1024, nbytes + 16 * 1024 * 1024))


def _params(sem, nbytes):
    return pltpu.CompilerParams(dimension_semantics=sem, vmem_limit_bytes=_vmem_limit(nbytes))


def _rms(x, g):
    ms = jnp.mean(x * x, axis=-1, keepdims=True)
    return x * lax.rsqrt(ms + RMS_EPS) * g


def _sigmoid(x):
    return 1.0 / (1.0 + jnp.exp(-x))


def _pick(n, pref):
    t = min(n, pref)
    assert n % t == 0, (n, t)
    return t


def _row_spec(tm, cols):
    return pl.BlockSpec((tm, cols), lambda i, j: (i, 0))


def _col_spec(rows, tn, col0=0):
    off = col0 // tn
    return pl.BlockSpec((rows, tn), lambda i, j: (0, j + off))


def _tile_spec(tm, tn):
    return pl.BlockSpec((tm, tn), lambda i, j: (i, j))


def _const_spec(rows, cols):
    return pl.BlockSpec((rows, cols), lambda i, j: (0, 0))


def _side_tile_spec(rows, tn, n_col_tiles):
    return pl.BlockSpec((rows, tn), lambda i, j: (0, jnp.where(i == 0, j, n_col_tiles - 1)))


def _stage_norm(x_ref, xs_ref, g_ref, n_ref, ns_ref):
    i, j = pl.program_id(0), pl.program_id(1)

    @pl.when(j == 0)
    def _():
        n_ref[...] = _rms(x_ref[...], g_ref[...]).astype(BF16)

    @pl.when(jnp.logical_and(i == 0, j == 0))
    def _():
        ns_ref[...] = _rms(xs_ref[...], g_ref[...]).astype(BF16)


def _norm_scratch(tm, ms, d):
    return [pltpu.VMEM((tm, d), BF16), pltpu.VMEM((ms, d), BF16)]


def _ffn_up_kernel(x_ref, xs_ref, g_ref, wg_ref, wu_ref, o_ref, os_ref, n_ref, ns_ref):
    _stage_norm(x_ref, xs_ref, g_ref, n_ref, ns_ref)

    wg = wg_ref[...].astype(BF16)
    wu = wu_ref[...].astype(BF16)

    def act(n):
        a = jnp.dot(n, wg, preferred_element_type=F32)
        u = jnp.dot(n, wu, preferred_element_type=F32)
        return (a * _sigmoid(a) * u).astype(BF16)

    o_ref[...] = act(n_ref[...])

    @pl.when(pl.program_id(0) == 0)
    def _():
        os_ref[...] = act(ns_ref[...])


def _ffn_up(x, xs, g, wg, wu, tm, tf):
    m, d = x.shape
    ms = xs.shape[0]
    f = wg.shape[1]
    nj = f // tf
    wb = wg.dtype.itemsize
    nbytes = 2 * tm * d * 4 + tm * d * 2 + 2 * 2 * d * tf * wb + 2 * d * tf * 2 + 2 * tm * tf * 2 + 3 * tm * tf * 4
    return pl.pallas_call(
        _ffn_up_kernel,
        grid=(m // tm, nj),
        in_specs=[_row_spec(tm, d), _const_spec(ms, d), _const_spec(1, d), _col_spec(d, tf), _col_spec(d, tf)],
        out_specs=[_tile_spec(tm, tf), _side_tile_spec(ms, tf, nj)],
        out_shape=[jax.ShapeDtypeStruct((m, f), BF16), jax.ShapeDtypeStruct((ms, f), BF16)],
        scratch_shapes=_norm_scratch(tm, ms, d),
        compiler_params=_params(("arbitrary", "arbitrary"), nbytes),
        name="ffn_up",
    )(x, xs, g, wg, wu)


def _ffn_down_kernel(h_ref, hs_ref, w_ref, r_ref, rs_ref, o_ref, os_ref):
    w = w_ref[...].astype(BF16)
    o_ref[...] = r_ref[...] + 0.5 * jnp.dot(h_ref[...], w, preferred_element_type=F32)

    @pl.when(pl.program_id(0) == 0)
    def _():
        os_ref[...] = rs_ref[...] + 0.5 * jnp.dot(hs_ref[...], w, preferred_element_type=F32)


def _ffn_down(hid, hid_s, wd, res, res_s, tm, tn):
    m, f = hid.shape
    ms = hid_s.shape[0]
    d = wd.shape[1]
    nj = d // tn
    nbytes = 2 * tm * f * 2 + 2 * f * tn * wd.dtype.itemsize + f * tn * 2 + 4 * tm * tn * 4 + tm * tn * 4
    return pl.pallas_call(
        _ffn_down_kernel,
        grid=(m // tm, nj),
        in_specs=[_row_spec(tm, f), _const_spec(ms, f), _col_spec(f, tn), _tile_spec(tm, tn),
                  _side_tile_spec(ms, tn, nj)],
        out_specs=[_tile_spec(tm, tn), _side_tile_spec(ms, tn, nj)],
        out_shape=[jax.ShapeDtypeStruct((m, d), F32), jax.ShapeDtypeStruct((ms, d), F32)],
        compiler_params=_params(("arbitrary", "arbitrary"), nbytes),
        name="ffn_down",
    )(hid, hid_s, wd, res, res_s)


def _qkv_proj_kernel(q_scale, x_ref, xs_ref, g_ref, w_ref, q_ref, k_ref, v_ref, qs_ref, ks_ref, vs_ref, n_ref, ns_ref):
    _stage_norm(x_ref, xs_ref, g_ref, n_ref, ns_ref)
    i, j = pl.program_id(0), pl.program_id(1)
    w = w_ref[...].astype(BF16)

    def emit(n, outs):
        u = jnp.dot(n, w, preferred_element_type=F32)
        for jj, o_ref in enumerate(outs):
            @pl.when(j == jj)
            def _():
                o_ref[...] = (u * q_scale if jj == 0 else u).astype(o_ref.dtype)

    emit(n_ref[...], (q_ref, k_ref, v_ref))

    @pl.when(i == 0)
    def _():
        emit(ns_ref[...], (qs_ref, ks_ref, vs_ref))


def _glu_proj_kernel(x_ref, xs_ref, g_ref, wa_ref, wg_ref, o_ref, os_ref, n_ref, ns_ref):
    _stage_norm(x_ref, xs_ref, g_ref, n_ref, ns_ref)

    wa = wa_ref[...].astype(BF16)
    wg = wg_ref[...].astype(BF16)

    def glu(n):
        a = jnp.dot(n, wa, preferred_element_type=F32)
        gt = jnp.dot(n, wg, preferred_element_type=F32)
        return a * _sigmoid(gt)

    o_ref[...] = glu(n_ref[...])

    @pl.when(pl.program_id(0) == 0)
    def _():
        os_ref[...] = glu(ns_ref[...])


def _in_proj(x, xs, g, w_in, qk_cols, attn_width, conv_width, tm, tn):
    m, d = x.shape
    ms = xs.shape[0]
    q_scale = HEAD_DIM_QK ** -0.5 * LOG2E
    nbytes = 2 * tm * d * 4 + tm * d * 2 + 4 * d * tn * w_in.dtype.itemsize + 2 * d * tn * 2 + 6 * tm * tn * 4

    def call(kernel, col0s, cols, dtype, name):
        nj = cols // tn
        return pl.pallas_call(
            kernel,
            grid=(m // tm, nj),
            in_specs=[_row_spec(tm, d), _const_spec(ms, d), _const_spec(1, d)] + [_col_spec(d, tn, c) for c in col0s],
            out_specs=[_tile_spec(tm, tn), _side_tile_spec(ms, tn, nj)],
            out_shape=[jax.ShapeDtypeStruct((m, cols), dtype), jax.ShapeDtypeStruct((ms, cols), dtype)],
            scratch_shapes=_norm_scratch(tm, ms, d),
            compiler_params=_params(("arbitrary", "arbitrary"), nbytes),
            name=name,
        )(x, xs, g, *([w_in] * len(col0s)))

    assert qk_cols == attn_width
    width = qk_cols
    whole = pl.BlockSpec((tm, width), lambda i, j: (i, 0))
    whole_side = _const_spec(ms, width)
    dtypes = (BF16, F32, F32)
    qkv_bytes = 2 * tm * d * 4 + tm * d * 2 + 2 * d * width * w_in.dtype.itemsize + (2 * 10 + 4) * tm * width
    qkv = pl.pallas_call(
        functools.partial(_qkv_proj_kernel, q_scale),
        grid=(m // tm, 3),
        in_specs=[_row_spec(tm, d), _const_spec(ms, d), _const_spec(1, d), _col_spec(d, width)],
        out_specs=[whole] * 3 + [whole_side] * 3,
        out_shape=[jax.ShapeDtypeStruct((m, width), t) for t in dtypes]
        + [jax.ShapeDtypeStruct((ms, width), t) for t in dtypes],
        scratch_shapes=_norm_scratch(tm, ms, d),
        compiler_params=_params(("arbitrary", "arbitrary"), qkv_bytes),
        name="qkv_proj",
    )(x, xs, g, w_in)
    c0 = 2 * qk_cols + attn_width
    glu = call(_glu_proj_kernel, [c0, c0 + conv_width], conv_width, F32, "glu_proj")
    return (qkv[0], qkv[3]), (qkv[1], qkv[4]), (qkv[2], qkv[5]), glu


def _split_maps(q):
    lane = lax.broadcasted_iota(jnp.int32, q.shape, 1)
    zero = jnp.zeros_like(q)
    return jnp.concatenate([jnp.where(lane < HEAD_DIM_QK, q, zero), jnp.where(lane >= HEAD_DIM_QK, q, zero)], axis=0)


def _subln(o, g, post_scale):
    ms = jnp.mean(o * o, axis=-1, keepdims=True)
    return o * lax.rsqrt(ms + RMS_EPS) * g * post_scale


def _prompt_attn_kernel(post_scale, tq, n_heads, n_groups, sc_ref, q_ref, k_ref, v_ref, g_ref, o_ref, qq_ref, k16_ref,
                        vt_ref, *stat_refs):
    h = pl.program_id(1)
    qi = pl.program_id(2)
    slope = sc_ref[h]
    lam = sc_ref[n_heads]
    gw = 2 * tq // n_groups
    assert tq % gw == 0
    groups = [stat_refs[3 * g:3 * g + 3] for g in range(n_groups)]

    @pl.when(qi == 0)
    def _():
        k16_ref[...] = k_ref[...].astype(BF16)
        vt_ref[...] = v_ref[...].T.astype(BF16)

    qq_ref[...] = _split_maps(q_ref[...])
    for m_ref, l_ref, acc_ref in groups:
        m_ref[...] = jnp.full(m_ref.shape, NEG_INF, F32)
        l_ref[...] = jnp.zeros(l_ref.shape, F32)
        acc_ref[...] = jnp.zeros(acc_ref.shape, F32)

    def block(ki, masked):
        start = pl.multiple_of(ki * tq, tq)
        key = lax.broadcasted_iota(jnp.int32, (tq, LANES), 0)
        kbias = slope * ((ki - qi) * tq + key).astype(F32)
        kbias = jnp.concatenate([kbias] * (gw // LANES), axis=1)
        for g, (m_ref, l_ref, acc_ref) in enumerate(groups):
            q0 = (g * gw) % tq
            nk = q0 + gw if masked else tq
            k = k16_ref[pl.ds(start, nk), :]
            vt = vt_ref[:, pl.ds(start, nk)]
            qg = qq_ref[g * gw:(g + 1) * gw, :]
            s = lax.dot_general(k, qg, (((1,), (1,)), ((), ())), preferred_element_type=F32) + kbias[:nk]
            if masked:
                keyf = lax.broadcasted_iota(jnp.int32, (nk, gw), 0)
                query = q0 + lax.broadcasted_iota(jnp.int32, (nk, gw), 1)
                s = jnp.where(keyf <= query, s, NEG_INF)
            m_prev = m_ref[...]
            m_new = jnp.maximum(m_prev, jnp.max(s, axis=0, keepdims=True))
            alpha = jnp.exp2(m_prev - m_new)
            p = jnp.exp2(s - m_new)
            l_ref[...] = alpha * l_ref[...] + jnp.sum(p, axis=0, keepdims=True)
            acc_ref[...] = alpha * acc_ref[...] + jnp.dot(vt, p.astype(BF16), preferred_element_type=F32)
            m_ref[...] = m_new

    def body(ki, carry):
        block(ki, False)
        return carry

    lax.fori_loop(0, qi, body, 0)
    block(qi, True)
    o = jnp.concatenate([acc_ref[...] / l_ref[...] for _, l_ref, acc_ref in groups], axis=1)
    o = o[:, :tq] - lam * o[:, tq:]
    ms = jnp.mean(o * o, axis=0, keepdims=True)
    o = o * (lax.rsqrt(ms + RMS_EPS) * post_scale)
    o_ref[...] = (o.T * g_ref[...]).astype(o_ref.dtype)


def _prompt_attn(q16, k32, v32, g_subln, scal, batch, seq, n_heads, post_scale, tq):
    nq = seq // tq
    gw = 2 * tq // ATTN_COL_GROUPS
    grid_spec = pltpu.PrefetchScalarGridSpec(
        num_scalar_prefetch=1,
        grid=(batch, n_heads, nq),
        in_specs=[
            pl.BlockSpec((tq, HEAD_DIM_V), lambda b, h, i, sc: (b * nq + i, h)),
            pl.BlockSpec((seq, HEAD_DIM_V), lambda b, h, i, sc: (b, h)),
            pl.BlockSpec((seq, HEAD_DIM_V), lambda b, h, i, sc: (b, h)),
            pl.BlockSpec((1, HEAD_DIM_V), lambda b, h, i, sc: (0, 0)),
        ],
        out_specs=pl.BlockSpec((tq, HEAD_DIM_V), lambda b, h, i, sc: (b * nq + i, h)),
        scratch_shapes=[
            pltpu.VMEM((2 * tq, HEAD_DIM_V), BF16),
            pltpu.VMEM((seq, HEAD_DIM_V), BF16),
            pltpu.VMEM((HEAD_DIM_V, seq), BF16),
        ] + [pltpu.VMEM((1, gw), F32), pltpu.VMEM((1, gw), F32), pltpu.VMEM((HEAD_DIM_V, gw), F32)] * ATTN_COL_GROUPS,
    )
    nbytes = 4 * seq * HEAD_DIM_V * 4 + 2 * seq * HEAD_DIM_V * 2 + 8 * 2 * tq * tq * 4
    return pl.pallas_call(
        functools.partial(_prompt_attn_kernel, post_scale, tq, n_heads, ATTN_COL_GROUPS),
        grid_spec=grid_spec,
        out_shape=jax.ShapeDtypeStruct((batch * seq, n_heads * HEAD_DIM_V), BF16),
        compiler_params=_params(("parallel", "parallel", "arbitrary"), nbytes),
        name="prompt_attn",
    )(scal, q16, k32, v32, g_subln)


def _decode_attn_kernel(post_scale, n_heads, t_new, past_len, grp, pt_ref, sc_ref, q_ref, *refs):
    k_refs, v_refs = refs[:grp], refs[grp:2 * grp]
    kn_ref, vn_ref, g_ref, o_ref, bias_ref, slope_ref, m_ref, l_ref, acc_ref = refs[2 * grp:]
    p = pl.program_id(1)
    rows = 2 * n_heads * t_new
    rows_per_head = 2 * t_new
    cols = PAGE_SIZE * n_heads

    def head_geometry(ncols):
        row = lax.broadcasted_iota(jnp.int32, (rows, ncols), 0)
        col = lax.broadcasted_iota(jnp.int32, (rows, ncols), 1)
        rhead = row // rows_per_head
        slope = jnp.zeros((rows, ncols), F32)
        for hh in range(n_heads):
            slope = jnp.where(rhead == hh, sc_ref[hh], slope)
        return row, col, rhead == col % n_heads, col // n_heads, slope

    @pl.when(p == 0)
    def _():
        m_ref[...] = jnp.full(m_ref.shape, NEG_INF, F32)
        l_ref[...] = jnp.zeros(l_ref.shape, F32)
        acc_ref[...] = jnp.zeros(acc_ref.shape, F32)
        _, _, same_head, tok, slope = head_geometry(cols)
        bias_ref[...] = jnp.where(same_head, slope * tok.astype(F32), NEG_INF)
        slope_ref[...] = slope[:, 0:1]

    q = q_ref[...]

    def update(blocks):
        scores = [lax.dot_general(q, k2, (((1,), (1,)), ((), ())), preferred_element_type=F32) + bias
                  for k2, _, bias, _ in blocks]
        m_prev = m_ref[...]
        m_new = m_prev
        for s, (_, _, _, shift) in zip(scores, blocks):
            m_new = jnp.maximum(m_new, jnp.max(s, axis=-1, keepdims=True) + shift)
        alpha = jnp.exp2(m_prev - m_new)
        l_new = alpha * l_ref[...]
        acc = alpha * acc_ref[...]
        for s, (_, v2, _, shift) in zip(scores, blocks):
            pr = jnp.exp2(s - (m_new - shift))
            l_new = l_new + jnp.sum(pr, axis=-1, keepdims=True)
            acc = acc + jnp.dot(pr.astype(BF16), v2, preferred_element_type=F32)
        l_ref[...] = l_new
        acc_ref[...] = acc
        m_ref[...] = m_new

    blocks = []
    for g in range(grp):
        k2 = k_refs[g][...].reshape(cols, HEAD_DIM_V).astype(BF16)
        v2 = v_refs[g][...].reshape(cols, HEAD_DIM_V).astype(BF16)
        page_start = ((p * grp + g) * PAGE_SIZE - past_len).astype(F32)
        blocks.append((k2, v2, bias_ref[...], slope_ref[...] * page_start))
    update(blocks)

    @pl.when(p == pl.num_programs(1) - 1)
    def _():
        n_new = t_new * n_heads
        ncols = pl.cdiv(n_new, LANES) * LANES
        pad = jnp.zeros((ncols - n_new, HEAD_DIM_V), F32)
        kn = jnp.concatenate([kn_ref[...].reshape(n_new, HEAD_DIM_V), pad], axis=0).astype(BF16)
        vn = jnp.concatenate([vn_ref[...].reshape(n_new, HEAD_DIM_V), pad], axis=0).astype(BF16)
        row, col, same_head, tok, slope = head_geometry(ncols)
        visible = jnp.where(same_head, tok, ncols) <= row % t_new
        update([(kn, vn, jnp.where(visible, slope * tok.astype(F32), NEG_INF), 0.0)])
        o = acc_ref[...] / l_ref[...]
        lam = sc_ref[n_heads]
        gain = g_ref[...]
        outs = []
        for hh in range(n_heads):
            r0 = hh * rows_per_head
            oh = o[r0:r0 + t_new] - lam * o[r0 + t_new:r0 + 2 * t_new]
            outs.append(_subln(oh, gain, post_scale))
        o_ref[...] = jnp.concatenate(outs, axis=1).astype(o_ref.dtype)


def _decode_attn(q2, cache_k, cache_v, layer, k_new, v_new, g_subln, page_table, scal, post_scale):
    batch, n_pages = page_table.shape
    _, t_new, n_heads, _ = k_new.shape
    rows = 2 * n_heads * t_new
    width = n_heads * HEAD_DIM_V
    past_len = n_pages * PAGE_SIZE
    grp = _pick(n_pages, DECODE_PAGES_PER_STEP)
    page_block = (None, None, PAGE_SIZE, n_heads, HEAD_DIM_V)

    def page_spec(g):
        return pl.BlockSpec(page_block, lambda b, p, pt, sc: (layer, pt[b, p * grp + g], 0, 0, 0))

    new_spec = pl.BlockSpec((None, t_new, n_heads, HEAD_DIM_V), lambda b, p, pt, sc: (b, 0, 0, 0))
    grid_spec = pltpu.PrefetchScalarGridSpec(
        num_scalar_prefetch=2,
        grid=(batch, n_pages // grp),
        in_specs=[pl.BlockSpec((None, rows, HEAD_DIM_V), lambda b, p, pt, sc: (b, 0, 0))]
        + [page_spec(g) for g in range(grp)] * 2
        + [new_spec, new_spec, pl.BlockSpec((1, HEAD_DIM_V), lambda b, p, pt, sc: (0, 0))],
        out_specs=pl.BlockSpec((None, t_new, width), lambda b, p, pt, sc: (b, 0, 0)),
        scratch_shapes=[
            pltpu.VMEM((rows, PAGE_SIZE * n_heads), F32),
            pltpu.VMEM((rows, 1), F32),
            pltpu.VMEM((rows, 1), F32),
            pltpu.VMEM((rows, 1), F32),
            pltpu.VMEM((rows, HEAD_DIM_V), F32),
        ],
    )
    page_bytes = PAGE_SIZE * width * 4
    nbytes = 4 * grp * page_bytes + 6 * rows * PAGE_SIZE * n_heads * 4
    return pl.pallas_call(
        functools.partial(_decode_attn_kernel, post_scale, n_heads, t_new, past_len, grp),
        grid_spec=grid_spec,
        out_shape=jax.ShapeDtypeStruct((batch, t_new, width), BF16),
        compiler_params=_params(("parallel", "arbitrary"), nbytes),
        name="decode_attn",
    )(page_table, scal, q2, *([cache_k] * grp), *([cache_v] * grp), k_new, v_new, g_subln)


def _conv_kernel(seq, chunk, u_ref, prev_ref, w_ref, bdw_ref, gcn_ref, bcn_ref, o_ref, new_ref, ext_ref):
    hist = CONV_KERNEL - 1
    lead = CONV_PAD - hist
    cw = u_ref.shape[-1]
    ext_ref[0:lead, :] = jnp.zeros((lead, cw), F32)
    ext_ref[lead:CONV_PAD, :] = prev_ref[0]
    ext_ref[CONV_PAD:CONV_PAD + seq, :] = u_ref[0]
    new_ref[0] = ext_ref[lead + seq:CONV_PAD + seq, :]
    bdw = bdw_ref[...]
    gcn = gcn_ref[...]
    bcn = bcn_ref[...]

    def body(c, _):
        r0 = c * chunk if isinstance(c, int) else pl.multiple_of(c * chunk, chunk)
        parts = [None] * CONV_PARTIALS
        for j in range(CONV_KERNEL):
            term = w_ref[j:j + 1, :] * ext_ref[pl.ds(r0 + (lead + j), chunk), :]
            slot = j % CONV_PARTIALS
            parts[slot] = term if parts[slot] is None else parts[slot] + term
        y = functools.reduce(lambda a, b: a + b, parts) + bdw
        outs = []
        for c0 in range(0, cw, LANES):
            grp = slice(c0, c0 + LANES)
            mu = jnp.mean(y[:, grp], axis=-1, keepdims=True)
            yc = y[:, grp] - mu
            var = jnp.mean(yc * yc, axis=-1, keepdims=True)
            yn = yc * lax.rsqrt(var + LN_EPS) * gcn[:, grp] + bcn[:, grp]
            outs.append(yn * _sigmoid(yn))
        o_ref[0, pl.ds(r0, chunk), :] = jnp.concatenate(outs, axis=1).astype(o_ref.dtype)
        return 0

    n_chunks = seq // chunk
    if n_chunks <= CONV_CHUNKS_PER_ITER:
        for c in range(n_chunks):
            body(c, 0)
    else:
        lax.fori_loop(0, n_chunks, body, 0, unroll=math.gcd(n_chunks, CONV_CHUNKS_PER_ITER))


def _conv_module(u, prev, w_dw, b_dw, g_cn, b_cn):
    batch, seq, width = u.shape
    assert width // CONV_GROUPS == LANES
    chunk = _pick(seq, CONV_CHUNK_ROWS)
    cw = min(width, LANES * (CONV_CHUNK_ROWS // chunk))
    assert width % cw == 0
    hist = CONV_KERNEL - 1
    vec = pl.BlockSpec((1, cw), lambda b, c: (0, c))
    return pl.pallas_call(
        functools.partial(_conv_kernel, seq, chunk),
        grid=(batch, width // cw),
        in_specs=[
            pl.BlockSpec((1, seq, cw), lambda b, c: (b, 0, c)),
            pl.BlockSpec((1, hist, cw), lambda b, c: (b, 0, c)),
            pl.BlockSpec((CONV_KERNEL, cw), lambda b, c: (0, c)),
            vec, vec, vec,
        ],
        out_specs=[
            pl.BlockSpec((1, seq, cw), lambda b, c: (b, 0, c)),
            pl.BlockSpec((1, hist, cw), lambda b, c: (b, 0, c)),
        ],
        out_shape=[jax.ShapeDtypeStruct((batch, seq, width), BF16), jax.ShapeDtypeStruct((batch, hist, width), F32)],
        scratch_shapes=[pltpu.VMEM((CONV_PAD + seq, cw), F32)],
        compiler_params=_params(("parallel", "parallel"), 8 * (seq + CONV_PAD) * cw * 4),
        name="conv_module",
    )(u, prev, w_dw, b_dw, g_cn, b_cn)


def _out_proj_kernel(ka, a_ref, b_ref, as_ref, bs_ref, w_ref, r_ref, rs_ref, o_ref, os_ref):
    wa = w_ref[0:ka, :].astype(BF16)
    wb = w_ref[ka:, :].astype(BF16)

    def proj(a, b):
        return jnp.dot(a, wa, preferred_element_type=F32) + jnp.dot(b, wb, preferred_element_type=F32)

    o_ref[...] = r_ref[...] + proj(a_ref[...], b_ref[...])

    @pl.when(pl.program_id(0) == 0)
    def _():
        os_ref[...] = rs_ref[...] + proj(as_ref[...], bs_ref[...])


def _out_proj(a, b, a_s, b_s, w, res, res_s, tm, tn):
    m, ka = a.shape
    ms = a_s.shape[0]
    kb = b.shape[1]
    d = w.shape[1]
    nj = d // tn
    nbytes = 2 * tm * (ka + kb) * 2 + 2 * (ka + kb) * tn * w.dtype.itemsize + (ka + kb) * tn * 2 + 5 * tm * tn * 4
    return pl.pallas_call(
        functools.partial(_out_proj_kernel, ka),
        grid=(m // tm, nj),
        in_specs=[_row_spec(tm, ka), _row_spec(tm, kb), _const_spec(ms, ka), _const_spec(ms, kb),
                  _col_spec(ka + kb, tn), _tile_spec(tm, tn), _side_tile_spec(ms, tn, nj)],
        out_specs=[_tile_spec(tm, tn), _side_tile_spec(ms, tn, nj)],
        out_shape=[jax.ShapeDtypeStruct((m, d), F32), jax.ShapeDtypeStruct((ms, d), F32)],
        compiler_params=_params(("arbitrary", "arbitrary"), nbytes),
        name="out_proj",
    )(a, b, a_s, b_s, w, res, res_s)


def _ple_kernel(final_norm, h_ref, p_ref, hs_ref, ps_ref, g_ref, wg_ref, wp_ref, gf_ref, o_ref, os_ref):
    def ple(h, p):
        n = _rms(h, g_ref[...]).astype(BF16)
        gate = _sigmoid(jnp.dot(n, wg_ref[...], preferred_element_type=F32))
        proj = jnp.dot(p.astype(BF16), wp_ref[...], preferred_element_type=F32)
        h = h + proj * gate
        return _rms(h, gf_ref[...]) if final_norm else h

    o_ref[...] = ple(h_ref[...], p_ref[...])

    @pl.when(pl.program_id(0) == 0)
    def _():
        os_ref[...] = ple(hs_ref[...], ps_ref[...])


def _ple(h, p, h_s, p_s, g, wg, wp, g_final, final_norm, tm):
    m, d = h.shape
    ms = h_s.shape[0]
    dp = p.shape[1]
    nbytes = 4 * tm * d * 4 + 2 * d * d * 2 + 2 * dp * d * 2 + 2 * tm * dp * 4 + 4 * tm * d * 4
    row = lambda cols: pl.BlockSpec((tm, cols), lambda i: (i, 0))
    const = lambda rows, cols: pl.BlockSpec((rows, cols), lambda i: (0, 0))
    return pl.pallas_call(
        functools.partial(_ple_kernel, final_norm),
        grid=(m // tm,),
        in_specs=[row(d), row(dp), const(ms, d), const(ms, dp), const(1, d), const(d, d), const(dp, d), const(1, d)],
        out_specs=[row(d), const(ms, d)],
        out_shape=[jax.ShapeDtypeStruct((m, d), F32), jax.ShapeDtypeStruct((ms, d), F32)],
        compiler_params=_params(("arbitrary",), nbytes),
        name="ple",
    )(h, p, h_s, p_s, g, wg, wp, g_final)


def _lambda_init(layer_idx):
    return 0.8 - 0.6 * math.exp(-0.3 * layer_idx)


def kernel(x_prompt, x_sample, cache_k, cache_v, state_conv, page_table, p_prompt, p_sample, g_ffn1, w_ffn1_gate, w_ffn1_up, w_ffn1_down, g_mix, w_in, lambda_q1, lambda_k1, lambda_q2, lambda_k2, g_subln, w_dw, b_dw, g_conv_norm, b_conv_norm, w_out, g_ffn2, w_ffn2_gate, w_ffn2_up, w_ffn2_down, g_ple, w_ple_gate, w_ple_proj, g_final):
    depth = w_in.shape[0]
    batch, seq, d_model = x_prompt.shape
    dec_batch, dec_seq, _ = x_sample.shape
    n_heads = cache_k.shape[3]
    assert cache_k.shape[4] == 2 * HEAD_DIM_QK and cache_v.shape[4] == HEAD_DIM_V and cache_k.shape[2] == PAGE_SIZE
    qk_cols = n_heads * 2 * HEAD_DIM_QK
    attn_width = n_heads * HEAD_DIM_V
    conv_width = d_model - attn_width
    n_p, n_s = batch * seq, dec_batch * dec_seq
    h_p = x_prompt.reshape(n_p, d_model)
    h_s = x_sample.reshape(n_s, d_model)
    tm = _pick(n_p, TOKEN_TILE)
    tq = _pick(seq, ATTN_Q_TILE)
    slopes = jnp.asarray([LOG2E * 2.0 ** (-8.0 * (h + 1) / n_heads) for h in range(n_heads)], F32)
    row2 = lambda a: a.reshape(1, -1)
    gf = row2(g_final)

    outs = {name: [] for name in ("kp", "vp", "cp", "ks", "vs", "cs")}
    for l in range(depth):
        last = l == depth - 1
        lam_init = _lambda_init(l)
        lam = (jnp.exp(jnp.sum(lambda_q1[l] * lambda_k1[l])) - jnp.exp(jnp.sum(lambda_q2[l] * lambda_k2[l])) + lam_init)
        scal = jnp.concatenate([slopes, lam.reshape(1).astype(F32)])
        post_scale = 1.0 - lam_init
        w1g, w1u, w1d = w_ffn1_gate[l], w_ffn1_up[l], w_ffn1_down[l].astype(BF16)
        w2g, w2u, w2d = w_ffn2_gate[l], w_ffn2_up[l], w_ffn2_down[l].astype(BF16)
        w_in16 = w_in[l].astype(BF16)
        w_out16 = w_out[l].astype(BF16)
        w_pg16 = w_ple_gate[l].astype(BF16)
        w_pp16 = w_ple_proj[l].astype(BF16)
        tf = _pick(w1g.shape[1], 512)
        tn_proj = math.gcd(math.gcd(qk_cols, attn_width), math.gcd(conv_width, 1024))
        g_sub = row2(g_subln[l])

        def ffn(hp, hs, g, wg, wu, wd):
            hid_p, hid_s = _ffn_up(hp, hs, row2(g), wg, wu, tm, tf)
            return _ffn_down(hid_p, hid_s, wd, hp, hs, tm, _pick(d_model, 512))

        h_p, h_s = ffn(h_p, h_s, g_ffn1[l], w1g, w1u, w1d)
        (q_p, q_s), (k_p, k_s), (v_p, v_s), (glu_p, glu_s) = _in_proj(
            h_p, h_s, row2(g_mix[l]), w_in16, qk_cols, attn_width, conv_width, tm, tn_proj)
        conv_args = (w_dw[l], row2(b_dw[l]), row2(g_conv_norm[l]), row2(b_conv_norm[l]))

        attn_p = _prompt_attn(q_p, k_p, v_p, g_sub, scal, batch, seq, n_heads, post_scale, tq)
        zero_hist = jnp.zeros((batch, CONV_KERNEL - 1, conv_width), F32)
        conv_p, hist_p = _conv_module(glu_p.reshape(batch, seq, conv_width), zero_hist, *conv_args)
        outs["kp"].append(k_p.reshape(batch, seq, n_heads, 2 * HEAD_DIM_QK))
        outs["vp"].append(v_p.reshape(batch, seq, n_heads, HEAD_DIM_V))
        outs["cp"].append(hist_p)

        k_new = k_s.reshape(dec_batch, dec_seq, n_heads, 2 * HEAD_DIM_QK)
        v_new = v_s.reshape(dec_batch, dec_seq, n_heads, HEAD_DIM_V)
        q5 = q_s.reshape(dec_batch, dec_seq, n_heads, 2, HEAD_DIM_QK)
        q2 = jnp.einsum("bthmd,mn->bhmtnd", q5, jnp.eye(2, dtype=BF16))
        q2 = q2.reshape(dec_batch, 2 * n_heads * dec_seq, 2 * HEAD_DIM_QK)
        attn_s = _decode_attn(q2, cache_k, cache_v, l, k_new, v_new, g_sub, page_table, scal, post_scale)
        conv_s, hist_s = _conv_module(glu_s.reshape(dec_batch, dec_seq, conv_width), state_conv[l], *conv_args)
        outs["ks"].append(k_new)
        outs["vs"].append(v_new)
        outs["cs"].append(hist_s)

        h_p, h_s = _out_proj(attn_p, conv_p.reshape(n_p, conv_width), attn_s.reshape(n_s, attn_width),
                             conv_s.reshape(n_s, conv_width), w_out16, h_p, h_s, min(tm, 512), d_model)
        h_p, h_s = ffn(h_p, h_s, g_ffn2[l], w2g, w2u, w2d)
        h_p, h_s = _ple(h_p, p_prompt[l].reshape(n_p, -1), h_s, p_sample[l].reshape(n_s, -1), row2(g_ple[l]),
                        w_pg16, w_pp16, gf, last, min(tm, 512))

    y_prompt = h_p.reshape(batch, seq, d_model)
    y_sample = h_s.reshape(dec_batch, dec_seq, d_model)
    return (y_prompt, y_sample, jnp.stack(outs["kp"]), jnp.stack(outs["vp"]), jnp.stack(outs["cp"]),
            jnp.stack(outs["ks"]), jnp.stack(outs["vs"]), jnp.stack(outs["cs"]))
```

```python
import functools
import math

import jax
import jax.numpy as jnp
from jax import lax
from jax.experimental import pallas as pl
from jax.experimental.pallas import tpu as pltpu

F32 = jnp.float32
BF16 = jnp.bfloat16

HEAD_DIM_QK = 64
HEAD_DIM_V = 2 * HEAD_DIM_QK
CONV_GROUPS = 8
CONV_KERNEL = 31
PAGE_SIZE = 128
RMS_EPS = 1e-6
LN_EPS = 1e-5
NEG_INF = -1e30
LOG2E = math.log2(math.e)

V7X_VMEM_BYTES = 64 * 1024 * 1024
LANES = 128
SUBLANES = 8
CONV_PAD = 32
CONV_PARTIALS = 4
CONV_CHUNK_ROWS = 64
CONV_CHUNKS_PER_ITER = 16
DECODE_PAGES_PER_STEP = 16
ATTN_Q_TILE = 2048
ATTN_COL_GROUPS = 8
TOKEN_TILE = 1024


def _vmem_limit(nbytes):
    return int(min(V7X_VMEM_BYTES - 4 * 1024 * 1024, nbytes + 16 * 1024 * 1024))


def _params(sem, nbytes):
    return pltpu.CompilerParams(dimension_semantics=sem, vmem_limit_bytes=_vmem_limit(nbytes))


def _rms(x, g):
    ms = jnp.mean(x * x, axis=-1, keepdims=True)
    return x * lax.rsqrt(ms + RMS_EPS) * g


def _sigmoid(x):
    return 1.0 / (1.0 + jnp.exp(-x))


def _pick(n, pref):
    t = min(n, pref)
    assert n % t == 0, (n, t)
    return t


def _row_spec(tm, cols):
    return pl.BlockSpec((tm, cols), lambda i, j: (i, 0))


def _col_spec(rows, tn, col0=0):
    off = col0 // tn
    return pl.BlockSpec((rows, tn), lambda i, j: (0, j + off))


def _tile_spec(tm, tn):
    return pl.BlockSpec((tm, tn), lambda i, j: (i, j))


def _const_spec(rows, cols):
    return pl.BlockSpec((rows, cols), lambda i, j: (0, 0))


def _side_tile_spec(rows, tn, n_col_tiles):
    return pl.BlockSpec((rows, tn), lambda i, j: (0, jnp.where(i == 0, j, n_col_tiles - 1)))


def _stage_norm(x_ref, xs_ref, g_ref, n_ref, ns_ref):
    i, j = pl.program_id(0), pl.program_id(1)

    @pl.when(j == 0)
    def _():
        n_ref[...] = _rms(x_ref[...], g_ref[...]).astype(BF16)

    @pl.when(jnp.logical_and(i == 0, j == 0))
    def _():
        ns_ref[...] = _rms(xs_ref[...], g_ref[...]).astype(BF16)


def _norm_scratch(tm, ms, d):
    return [pltpu.VMEM((tm, d), BF16), pltpu.VMEM((ms, d), BF16)]


def _ffn_up_kernel(x_ref, xs_ref, g_ref, wg_ref, wu_ref, o_ref, os_ref, n_ref, ns_ref):
    _stage_norm(x_ref, xs_ref, g_ref, n_ref, ns_ref)

    wg = wg_ref[...].astype(BF16)
    wu = wu_ref[...].astype(BF16)

    def act(n):
        a = jnp.dot(n, wg, preferred_element_type=F32)
        u = jnp.dot(n, wu, preferred_element_type=F32)
        return (a * _sigmoid(a) * u).astype(BF16)

    o_ref[...] = act(n_ref[...])

    @pl.when(pl.program_id(0) == 0)
    def _():
        os_ref[...] = act(ns_ref[...])


def _ffn_up(x, xs, g, wg, wu, tm, tf):
    m, d = x.shape
    ms = xs.shape[0]
    f = wg.shape[1]
    nj = f // tf
    wb = wg.dtype.itemsize
    nbytes = 2 * tm * d * 4 + tm * d * 2 + 2 * 2 * d * tf * wb + 2 * d * tf * 2 + 2 * tm * tf * 2 + 3 * tm * tf * 4
    return pl.pallas_call(
        _ffn_up_kernel,
        grid=(m // tm, nj),
        in_specs=[_row_spec(tm, d), _const_spec(ms, d), _const_spec(1, d), _col_spec(d, tf), _col_spec(d, tf)],
        out_specs=[_tile_spec(tm, tf), _side_tile_spec(ms, tf, nj)],
        out_shape=[jax.ShapeDtypeStruct((m, f), BF16), jax.ShapeDtypeStruct((ms, f), BF16)],
        scratch_shapes=_norm_scratch(tm, ms, d),
        compiler_params=_params(("arbitrary", "arbitrary"), nbytes),
        name="ffn_up",
    )(x, xs, g, wg, wu)


def _ffn_down_kernel(h_ref, hs_ref, w_ref, r_ref, rs_ref, o_ref, os_ref):
    w = w_ref[...].astype(BF16)
    o_ref[...] = r_ref[...] + 0.5 * jnp.dot(h_ref[...], w, preferred_element_type=F32)

    @pl.when(pl.program_id(0) == 0)
    def _():
        os_ref[...] = rs_ref[...] + 0.5 * jnp.dot(hs_ref[...], w, preferred_element_type=F32)


def _ffn_down(hid, hid_s, wd, res, res_s, tm, tn):
    m, f = hid.shape
    ms = hid_s.shape[0]
    d = wd.shape[1]
    nj = d // tn
    nbytes = 2 * tm * f * 2 + 2 * f * tn * wd.dtype.itemsize + f * tn * 2 + 4 * tm * tn * 4 + tm * tn * 4
    return pl.pallas_call(
        _ffn_down_kernel,
        grid=(m // tm, nj),
        in_specs=[_row_spec(tm, f), _const_spec(ms, f), _col_spec(f, tn), _tile_spec(tm, tn),
                  _side_tile_spec(ms, tn, nj)],
        out_specs=[_tile_spec(tm, tn), _side_tile_spec(ms, tn, nj)],
        out_shape=[jax.ShapeDtypeStruct((m, d), F32), jax.ShapeDtypeStruct((ms, d), F32)],
        compiler_params=_params(("arbitrary", "arbitrary"), nbytes),
        name="ffn_down",
    )(hid, hid_s, wd, res, res_s)


def _qkv_proj_kernel(q_scale, x_ref, xs_ref, g_ref, w_ref, q_ref, k_ref, v_ref, qs_ref, ks_ref, vs_ref, n_ref, ns_ref):
    _stage_norm(x_ref, xs_ref, g_ref, n_ref, ns_ref)
    i, j = pl.program_id(0), pl.program_id(1)
    w = w_ref[...].astype(BF16)

    def emit(n, outs):
        u = jnp.dot(n, w, preferred_element_type=F32)
        for jj, o_ref in enumerate(outs):
            @pl.when(j == jj)
            def _():
                o_ref[...] = (u * q_scale if jj == 0 else u).astype(o_ref.dtype)

    emit(n_ref[...], (q_ref, k_ref, v_ref))

    @pl.when(i == 0)
    def _():
        emit(ns_ref[...], (qs_ref, ks_ref, vs_ref))


def _glu_proj_kernel(x_ref, xs_ref, g_ref, wa_ref, wg_ref, o_ref, os_ref, n_ref, ns_ref):
    _stage_norm(x_ref, xs_ref, g_ref, n_ref, ns_ref)

    wa = wa_ref[...].astype(BF16)
    wg = wg_ref[...].astype(BF16)

    def glu(n):
        a = jnp.dot(n, wa, preferred_element_type=F32)
        gt = jnp.dot(n, wg, preferred_element_type=F32)
        return a * _sigmoid(gt)

    o_ref[...] = glu(n_ref[...])

    @pl.when(pl.program_id(0) == 0)
    def _():
        os_ref[...] = glu(ns_ref[...])


def _in_proj(x, xs, g, w_in, qk_cols, attn_width, conv_width, tm, tn):
    m, d = x.shape
    ms = xs.shape[0]
    q_scale = HEAD_DIM_QK ** -0.5 * LOG2E
    nbytes = 2 * tm * d * 4 + tm * d * 2 + 4 * d * tn * w_in.dtype.itemsize + 2 * d * tn * 2 + 6 * tm * tn * 4

    def call(kernel, col0s, cols, dtype, name):
        nj = cols // tn
        return pl.pallas_call(
            kernel,
            grid=(m // tm, nj),
            in_specs=[_row_spec(tm, d), _const_spec(ms, d), _const_spec(1, d)] + [_col_spec(d, tn, c) for c in col0s],
            out_specs=[_tile_spec(tm, tn), _side_tile_spec(ms, tn, nj)],
            out_shape=[jax.ShapeDtypeStruct((m, cols), dtype), jax.ShapeDtypeStruct((ms, cols), dtype)],
            scratch_shapes=_norm_scratch(tm, ms, d),
            compiler_params=_params(("arbitrary", "arbitrary"), nbytes),
            name=name,
        )(x, xs, g, *([w_in] * len(col0s)))

    assert qk_cols == attn_width
    width = qk_cols
    whole = pl.BlockSpec((tm, width), lambda i, j: (i, 0))
    whole_side = _const_spec(ms, width)
    dtypes = (BF16, F32, F32)
    qkv_bytes = 2 * tm * d * 4 + tm * d * 2 + 2 * d * width * w_in.dtype.itemsize + (2 * 10 + 4) * tm * width
    qkv = pl.pallas_call(
        functools.partial(_qkv_proj_kernel, q_scale),
        grid=(m // tm, 3),
        in_specs=[_row_spec(tm, d), _const_spec(ms, d), _const_spec(1, d), _col_spec(d, width)],
        out_specs=[whole] * 3 + [whole_side] * 3,
        out_shape=[jax.ShapeDtypeStruct((m, width), t) for t in dtypes]
        + [jax.ShapeDtypeStruct((ms, width), t) for t in dtypes],
        scratch_shapes=_norm_scratch(tm, ms, d),
        compiler_params=_params(("arbitrary", "arbitrary"), qkv_bytes),
        name="qkv_proj",
    )(x, xs, g, w_in)
    c0 = 2 * qk_cols + attn_width
    glu = call(_glu_proj_kernel, [c0, c0 + conv_width], conv_width, F32, "glu_proj")
    return (qkv[0], qkv[3]), (qkv[1], qkv[4]), (qkv[2], qkv[5]), glu


def _split_maps(q):
    lane = lax.broadcasted_iota(jnp.int32, q.shape, 1)
    zero = jnp.zeros_like(q)
    return jnp.concatenate([jnp.where(lane < HEAD_DIM_QK, q, zero), jnp.where(lane >= HEAD_DIM_QK, q, zero)], axis=0)


def _subln(o, g, post_scale):
    ms = jnp.mean(o * o, axis=-1, keepdims=True)
    return o * lax.rsqrt(ms + RMS_EPS) * g * post_scale


def _prompt_attn_kernel(post_scale, tq, n_heads, n_groups, single_tile, sc_ref, q_ref, k_ref, v_ref, g_ref, o_ref, qq_ref, k16_ref,
                        vt_ref, *stat_refs):
    h = pl.program_id(1)
    qi = pl.program_id(2)
    slope = sc_ref[h]
    lam = sc_ref[n_heads]
    gw = 2 * tq // n_groups
    assert tq % gw == 0
    groups = [stat_refs[3 * g:3 * g + 3] for g in range(n_groups)]

    @pl.when(qi == 0)
    def _():
        k16_ref[...] = k_ref[...].astype(BF16)
        vt_ref[...] = v_ref[...].T.astype(BF16)

    qq_ref[...] = _split_maps(q_ref[...])

    def scores(g, ki, masked):
        start = pl.multiple_of(ki * tq, tq)
        q0 = (g * gw) % tq
        nk = q0 + gw if masked else tq
        key = lax.broadcasted_iota(jnp.int32, (nk, LANES), 0)
        kbias = slope * ((ki - qi) * tq + key).astype(F32)
        kbias = jnp.concatenate([kbias] * (gw // LANES), axis=1)
        k = k16_ref[pl.ds(start, nk), :]
        vt = vt_ref[:, pl.ds(start, nk)]
        qg = qq_ref[g * gw:(g + 1) * gw, :]
        s = lax.dot_general(k, qg, (((1,), (1,)), ((), ())), preferred_element_type=F32) + kbias
        if masked:
            keyf = lax.broadcasted_iota(jnp.int32, (nk, gw), 0)
            query = q0 + lax.broadcasted_iota(jnp.int32, (nk, gw), 1)
            s = jnp.where(keyf <= query, s, NEG_INF)
        return s, vt

    if single_tile:
        outs = []
        for g in range(n_groups):
            s, vt = scores(g, qi, True)
            p = jnp.exp2(s - jnp.max(s, axis=0, keepdims=True))
            acc = jnp.dot(vt, p.astype(BF16), preferred_element_type=F32)
            outs.append(acc / jnp.sum(p, axis=0, keepdims=True))
        o = jnp.concatenate(outs, axis=1)
    else:
        for m_ref, l_ref, acc_ref in groups:
            m_ref[...] = jnp.full(m_ref.shape, NEG_INF, F32)
            l_ref[...] = jnp.zeros(l_ref.shape, F32)
            acc_ref[...] = jnp.zeros(acc_ref.shape, F32)

        def block(ki, masked):
            for g, (m_ref, l_ref, acc_ref) in enumerate(groups):
                s, vt = scores(g, ki, masked)
                m_prev = m_ref[...]
                m_new = jnp.maximum(m_prev, jnp.max(s, axis=0, keepdims=True))
                alpha = jnp.exp2(m_prev - m_new)
                p = jnp.exp2(s - m_new)
                l_ref[...] = alpha * l_ref[...] + jnp.sum(p, axis=0, keepdims=True)
                acc_ref[...] = alpha * acc_ref[...] + jnp.dot(vt, p.astype(BF16), preferred_element_type=F32)
                m_ref[...] = m_new

        def body(ki, carry):
            block(ki, False)
            return carry

        lax.fori_loop(0, qi, body, 0)
        block(qi, True)
        o = jnp.concatenate([acc_ref[...] / l_ref[...] for _, l_ref, acc_ref in groups], axis=1)
    o = o[:, :tq] - lam * o[:, tq:]
    ms = jnp.mean(o * o, axis=0, keepdims=True)
    o = o * (lax.rsqrt(ms + RMS_EPS) * post_scale)
    o_ref[...] = (o.T * g_ref[...]).astype(o_ref.dtype)


def _prompt_attn(q16, k32, v32, g_subln, scal, batch, seq, n_heads, post_scale, tq):
    nq = seq // tq
    gw = 2 * tq // ATTN_COL_GROUPS
    grid_spec = pltpu.PrefetchScalarGridSpec(
        num_scalar_prefetch=1,
        grid=(batch, n_heads, nq),
        in_specs=[
            pl.BlockSpec((tq, HEAD_DIM_V), lambda b, h, i, sc: (b * nq + i, h)),
            pl.BlockSpec((seq, HEAD_DIM_V), lambda b, h, i, sc: (b, h)),
            pl.BlockSpec((seq, HEAD_DIM_V), lambda b, h, i, sc: (b, h)),
            pl.BlockSpec((1, HEAD_DIM_V), lambda b, h, i, sc: (0, 0)),
        ],
        out_specs=pl.BlockSpec((tq, HEAD_DIM_V), lambda b, h, i, sc: (b * nq + i, h)),
        scratch_shapes=[
            pltpu.VMEM((2 * tq, HEAD_DIM_V), BF16),
            pltpu.VMEM((seq, HEAD_DIM_V), BF16),
            pltpu.VMEM((HEAD_DIM_V, seq), BF16),
        ] + [pltpu.VMEM((1, gw), F32), pltpu.VMEM((1, gw), F32), pltpu.VMEM((HEAD_DIM_V, gw), F32)] * ATTN_COL_GROUPS,
    )
    nbytes = 4 * seq * HEAD_DIM_V * 4 + 2 * seq * HEAD_DIM_V * 2 + 8 * 2 * tq * tq * 4
    return pl.pallas_call(
        functools.partial(_prompt_attn_kernel, post_scale, tq, n_heads, ATTN_COL_GROUPS, nq == 1),
        grid_spec=grid_spec,
        out_shape=jax.ShapeDtypeStruct((batch * seq, n_heads * HEAD_DIM_V), BF16),
        compiler_params=_params(("parallel", "parallel", "arbitrary"), nbytes),
        name="prompt_attn",
    )(scal, q16, k32, v32, g_subln)


def _decode_attn_kernel(post_scale, n_heads, t_new, past_len, grp, pt_ref, sc_ref, q_ref, *refs):
    k_refs, v_refs = refs[:grp], refs[grp:2 * grp]
    kn_ref, vn_ref, g_ref, o_ref, bias_ref, slope_ref, m_ref, l_ref, acc_ref = refs[2 * grp:]
    p = pl.program_id(1)
    rows = 2 * n_heads * t_new
    rows_per_head = 2 * t_new
    cols = PAGE_SIZE * n_heads

    def head_geometry(ncols):
        row = lax.broadcasted_iota(jnp.int32, (rows, ncols), 0)
        col = lax.broadcasted_iota(jnp.int32, (rows, ncols), 1)
        rhead = row // rows_per_head
        slope = jnp.zeros((rows, ncols), F32)
        for hh in range(n_heads):
            slope = jnp.where(rhead == hh, sc_ref[hh], slope)
        return row, col, rhead == col % n_heads, col // n_heads, slope

    @pl.when(p == 0)
    def _():
        m_ref[...] = jnp.full(m_ref.shape, NEG_INF, F32)
        l_ref[...] = jnp.zeros(l_ref.shape, F32)
        acc_ref[...] = jnp.zeros(acc_ref.shape, F32)
        _, _, same_head, tok, slope = head_geometry(cols)
        bias_ref[...] = jnp.where(same_head, slope * tok.astype(F32), NEG_INF)
        slope_ref[...] = slope[:, 0:1]

    q = q_ref[...]

    def update(blocks):
        scores = [lax.dot_general(q, k2, (((1,), (1,)), ((), ())), preferred_element_type=F32) + bias
                  for k2, _, bias, _ in blocks]
        m_prev = m_ref[...]
        m_new = m_prev
        for s, (_, _, _, shift) in zip(scores, blocks):
            m_new = jnp.maximum(m_new, jnp.max(s, axis=-1, keepdims=True) + shift)
        alpha = jnp.exp2(m_prev - m_new)
        l_new = alpha * l_ref[...]
        acc = alpha * acc_ref[...]
        for s, (_, v2, _, shift) in zip(scores, blocks):
            pr = jnp.exp2(s - (m_new - shift))
            l_new = l_new + jnp.sum(pr, axis=-1, keepdims=True)
            acc = acc + jnp.dot(pr.astype(BF16), v2, preferred_element_type=F32)
        l_ref[...] = l_new
        acc_ref[...] = acc
        m_ref[...] = m_new

    blocks = []
    for g in range(grp):
        k2 = k_refs[g][...].reshape(cols, HEAD_DIM_V).astype(BF16)
        v2 = v_refs[g][...].reshape(cols, HEAD_DIM_V).astype(BF16)
        page_start = ((p * grp + g) * PAGE_SIZE - past_len).astype(F32)
        blocks.append((k2, v2, bias_ref[...], slope_ref[...] * page_start))
    update(blocks)

    @pl.when(p == pl.num_programs(1) - 1)
    def _():
        n_new = t_new * n_heads
        ncols = pl.cdiv(n_new, LANES) * LANES
        pad = jnp.zeros((ncols - n_new, HEAD_DIM_V), F32)
        kn = jnp.concatenate([kn_ref[...].reshape(n_new, HEAD_DIM_V), pad], axis=0).astype(BF16)
        vn = jnp.concatenate([vn_ref[...].reshape(n_new, HEAD_DIM_V), pad], axis=0).astype(BF16)
        row, col, same_head, tok, slope = head_geometry(ncols)
        visible = jnp.where(same_head, tok, ncols) <= row % t_new
        update([(kn, vn, jnp.where(visible, slope * tok.astype(F32), NEG_INF), 0.0)])
        o = acc_ref[...] / l_ref[...]
        lam = sc_ref[n_heads]
        gain = g_ref[...]
        outs = []
        for hh in range(n_heads):
            r0 = hh * rows_per_head
            oh = o[r0:r0 + t_new] - lam * o[r0 + t_new:r0 + 2 * t_new]
            outs.append(_subln(oh, gain, post_scale))
        o_ref[...] = jnp.concatenate(outs, axis=1).astype(o_ref.dtype)


def _decode_attn(q2, cache_k, cache_v, layer, k_new, v_new, g_subln, page_table, scal, post_scale):
    batch, n_pages = page_table.shape
    _, t_new, n_heads, _ = k_new.shape
    rows = 2 * n_heads * t_new
    width = n_heads * HEAD_DIM_V
    past_len = n_pages * PAGE_SIZE
    grp = _pick(n_pages, DECODE_PAGES_PER_STEP)
    page_block = (None, None, PAGE_SIZE, n_heads, HEAD_DIM_V)

    def page_spec(g):
        return pl.BlockSpec(page_block, lambda b, p, pt, sc: (layer, pt[b, p * grp + g], 0, 0, 0))

    new_spec = pl.BlockSpec((None, t_new, n_heads, HEAD_DIM_V), lambda b, p, pt, sc: (b, 0, 0, 0))
    grid_spec = pltpu.PrefetchScalarGridSpec(
        num_scalar_prefetch=2,
        grid=(batch, n_pages // grp),
        in_specs=[pl.BlockSpec((None, rows, HEAD_DIM_V), lambda b, p, pt, sc: (b, 0, 0))]
        + [page_spec(g) for g in range(grp)] * 2
        + [new_spec, new_spec, pl.BlockSpec((1, HEAD_DIM_V), lambda b, p, pt, sc: (0, 0))],
        out_specs=pl.BlockSpec((None, t_new, width), lambda b, p, pt, sc: (b, 0, 0)),
        scratch_shapes=[
            pltpu.VMEM((rows, PAGE_SIZE * n_heads), F32),
            pltpu.VMEM((rows, 1), F32),
            pltpu.VMEM((rows, 1), F32),
            pltpu.VMEM((rows, 1), F32),
            pltpu.VMEM((rows, HEAD_DIM_V), F32),
        ],
    )
    page_bytes = PAGE_SIZE * width * 4
    nbytes = 4 * grp * page_bytes + 6 * rows * PAGE_SIZE * n_heads * 4
    return pl.pallas_call(
        functools.partial(_decode_attn_kernel, post_scale, n_heads, t_new, past_len, grp),
        grid_spec=grid_spec,
        out_shape=jax.ShapeDtypeStruct((batch, t_new, width), BF16),
        compiler_params=_params(("parallel", "arbitrary"), nbytes),
        name="decode_attn",
    )(page_table, scal, q2, *([cache_k] * grp), *([cache_v] * grp), k_new, v_new, g_subln)


def _conv_kernel(seq, chunk, u_ref, prev_ref, w_ref, bdw_ref, gcn_ref, bcn_ref, o_ref, new_ref, ext_ref):
    hist = CONV_KERNEL - 1
    lead = CONV_PAD - hist
    cw = u_ref.shape[-1]
    ext_ref[0:lead, :] = jnp.zeros((lead, cw), F32)
    ext_ref[lead:CONV_PAD, :] = prev_ref[0]
    ext_ref[CONV_PAD:CONV_PAD + seq, :] = u_ref[0]
    new_ref[0] = ext_ref[lead + seq:CONV_PAD + seq, :]
    bdw = bdw_ref[...]
    gcn = gcn_ref[...]
    bcn = bcn_ref[...]

    def body(c, _):
        r0 = c * chunk if isinstance(c, int) else pl.multiple_of(c * chunk, chunk)
        parts = [None] * CONV_PARTIALS
        for j in range(CONV_KERNEL):
            term = w_ref[j:j + 1, :] * ext_ref[pl.ds(r0 + (lead + j), chunk), :]
            slot = j % CONV_PARTIALS
            parts[slot] = term if parts[slot] is None else parts[slot] + term
        y = functools.reduce(lambda a, b: a + b, parts) + bdw
        outs = []
        for c0 in range(0, cw, LANES):
            grp = slice(c0, c0 + LANES)
            mu = jnp.mean(y[:, grp], axis=-1, keepdims=True)
            yc = y[:, grp] - mu
            var = jnp.mean(yc * yc, axis=-1, keepdims=True)
            yn = yc * lax.rsqrt(var + LN_EPS) * gcn[:, grp] + bcn[:, grp]
            outs.append(yn * _sigmoid(yn))
        o_ref[0, pl.ds(r0, chunk), :] = jnp.concatenate(outs, axis=1).astype(o_ref.dtype)
        return 0

    n_chunks = seq // chunk
    if n_chunks <= CONV_CHUNKS_PER_ITER:
        for c in range(n_chunks):
            body(c, 0)
    else:
        lax.fori_loop(0, n_chunks, body, 0, unroll=math.gcd(n_chunks, CONV_CHUNKS_PER_ITER))


def _conv_module(u, prev, w_dw, b_dw, g_cn, b_cn):
    batch, seq, width = u.shape
    assert width // CONV_GROUPS == LANES
    chunk = _pick(seq, CONV_CHUNK_ROWS)
    cw = min(width, LANES * (CONV_CHUNK_ROWS // chunk))
    assert width % cw == 0
    hist = CONV_KERNEL - 1
    vec = pl.BlockSpec((1, cw), lambda b, c: (0, c))
    return pl.pallas_call(
        functools.partial(_conv_kernel, seq, chunk),
        grid=(batch, width // cw),
        in_specs=[
            pl.BlockSpec((1, seq, cw), lambda b, c: (b, 0, c)),
            pl.BlockSpec((1, hist, cw), lambda b, c: (b, 0, c)),
            pl.BlockSpec((CONV_KERNEL, cw), lambda b, c: (0, c)),
            vec, vec, vec,
        ],
        out_specs=[
            pl.BlockSpec((1, seq, cw), lambda b, c: (b, 0, c)),
            pl.BlockSpec((1, hist, cw), lambda b, c: (b, 0, c)),
        ],
        out_shape=[jax.ShapeDtypeStruct((batch, seq, width), BF16), jax.ShapeDtypeStruct((batch, hist, width), F32)],
        scratch_shapes=[pltpu.VMEM((CONV_PAD + seq, cw), F32)],
        compiler_params=_params(("parallel", "parallel"), 8 * (seq + CONV_PAD) * cw * 4),
        name="conv_module",
    )(u, prev, w_dw, b_dw, g_cn, b_cn)


def _out_proj_kernel(ka, a_ref, b_ref, as_ref, bs_ref, w_ref, r_ref, rs_ref, o_ref, os_ref):
    wa = w_ref[0:ka, :].astype(BF16)
    wb = w_ref[ka:, :].astype(BF16)

    def proj(a, b):
        return jnp.dot(a, wa, preferred_element_type=F32) + jnp.dot(b, wb, preferred_element_type=F32)

    o_ref[...] = r_ref[...] + proj(a_ref[...], b_ref[...])

    @pl.when(pl.program_id(0) == 0)
    def _():
        os_ref[...] = rs_ref[...] + proj(as_ref[...], bs_ref[...])


def _out_proj(a, b, a_s, b_s, w, res, res_s, tm, tn):
    m, ka = a.shape
    ms = a_s.shape[0]
    kb = b.shape[1]
    d = w.shape[1]
    nj = d // tn
    nbytes = 2 * tm * (ka + kb) * 2 + 2 * (ka + kb) * tn * w.dtype.itemsize + (ka + kb) * tn * 2 + 5 * tm * tn * 4
    return pl.pallas_call(
        functools.partial(_out_proj_kernel, ka),
        grid=(m // tm, nj),
        in_specs=[_row_spec(tm, ka), _row_spec(tm, kb), _const_spec(ms, ka), _const_spec(ms, kb),
                  _col_spec(ka + kb, tn), _tile_spec(tm, tn), _side_tile_spec(ms, tn, nj)],
        out_specs=[_tile_spec(tm, tn), _side_tile_spec(ms, tn, nj)],
        out_shape=[jax.ShapeDtypeStruct((m, d), F32), jax.ShapeDtypeStruct((ms, d), F32)],
        compiler_params=_params(("arbitrary", "arbitrary"), nbytes),
        name="out_proj",
    )(a, b, a_s, b_s, w, res, res_s)


def _ple_kernel(final_norm, h_ref, p_ref, hs_ref, ps_ref, g_ref, wg_ref, wp_ref, gf_ref, o_ref, os_ref):
    def ple(h, p):
        n = _rms(h, g_ref[...]).astype(BF16)
        gate = _sigmoid(jnp.dot(n, wg_ref[...], preferred_element_type=F32))
        proj = jnp.dot(p.astype(BF16), wp_ref[...], preferred_element_type=F32)
        h = h + proj * gate
        return _rms(h, gf_ref[...]) if final_norm else h

    o_ref[...] = ple(h_ref[...], p_ref[...])

    @pl.when(pl.program_id(0) == 0)
    def _():
        os_ref[...] = ple(hs_ref[...], ps_ref[...])


def _ple(h, p, h_s, p_s, g, wg, wp, g_final, final_norm, tm):
    m, d = h.shape
    ms = h_s.shape[0]
    dp = p.shape[1]
    nbytes = 4 * tm * d * 4 + 2 * d * d * 2 + 2 * dp * d * 2 + 2 * tm * dp * 4 + 4 * tm * d * 4
    row = lambda cols: pl.BlockSpec((tm, cols), lambda i: (i, 0))
    const = lambda rows, cols: pl.BlockSpec((rows, cols), lambda i: (0, 0))
    return pl.pallas_call(
        functools.partial(_ple_kernel, final_norm),
        grid=(m // tm,),
        in_specs=[row(d), row(dp), const(ms, d), const(ms, dp), const(1, d), const(d, d), const(dp, d), const(1, d)],
        out_specs=[row(d), const(ms, d)],
        out_shape=[jax.ShapeDtypeStruct((m, d), F32), jax.ShapeDtypeStruct((ms, d), F32)],
        compiler_params=_params(("arbitrary",), nbytes),
        name="ple",
    )(h, p, h_s, p_s, g, wg, wp, g_final)


def _lambda_init(layer_idx):
    return 0.8 - 0.6 * math.exp(-0.3 * layer_idx)


def kernel(x_prompt, x_sample, cache_k, cache_v, state_conv, page_table, p_prompt, p_sample, g_ffn1, w_ffn1_gate, w_ffn1_up, w_ffn1_down, g_mix, w_in, lambda_q1, lambda_k1, lambda_q2, lambda_k2, g_subln, w_dw, b_dw, g_conv_norm, b_conv_norm, w_out, g_ffn2, w_ffn2_gate, w_ffn2_up, w_ffn2_down, g_ple, w_ple_gate, w_ple_proj, g_final):
    depth = w_in.shape[0]
    batch, seq, d_model = x_prompt.shape
    dec_batch, dec_seq, _ = x_sample.shape
    n_heads = cache_k.shape[3]
    assert cache_k.shape[4] == 2 * HEAD_DIM_QK and cache_v.shape[4] == HEAD_DIM_V and cache_k.shape[2] == PAGE_SIZE
    qk_cols = n_heads * 2 * HEAD_DIM_QK
    attn_width = n_heads * HEAD_DIM_V
    conv_width = d_model - attn_width
    n_p, n_s = batch * seq, dec_batch * dec_seq
    h_p = x_prompt.reshape(n_p, d_model)
    h_s = x_sample.reshape(n_s, d_model)
    tm = _pick(n_p, TOKEN_TILE)
    tq = _pick(seq, ATTN_Q_TILE)
    slopes = jnp.asarray([LOG2E * 2.0 ** (-8.0 * (h + 1) / n_heads) for h in range(n_heads)], F32)
    row2 = lambda a: a.reshape(1, -1)
    gf = row2(g_final)

    outs = {name: [] for name in ("kp", "vp", "cp", "ks", "vs", "cs")}
    for l in range(depth):
        last = l == depth - 1
        lam_init = _lambda_init(l)
        lam = (jnp.exp(jnp.sum(lambda_q1[l] * lambda_k1[l])) - jnp.exp(jnp.sum(lambda_q2[l] * lambda_k2[l])) + lam_init)
        scal = jnp.concatenate([slopes, lam.reshape(1).astype(F32)])
        post_scale = 1.0 - lam_init
        w1g, w1u, w1d = w_ffn1_gate[l], w_ffn1_up[l], w_ffn1_down[l].astype(BF16)
        w2g, w2u, w2d = w_ffn2_gate[l], w_ffn2_up[l], w_ffn2_down[l].astype(BF16)
        w_in16 = w_in[l].astype(BF16)
        w_out16 = w_out[l].astype(BF16)
        w_pg16 = w_ple_gate[l].astype(BF16)
        w_pp16 = w_ple_proj[l].astype(BF16)
        tf = _pick(w1g.shape[1], 512)
        tn_proj = math.gcd(math.gcd(qk_cols, attn_width), math.gcd(conv_width, 1024))
        g_sub = row2(g_subln[l])

        def ffn(hp, hs, g, wg, wu, wd):
            hid_p, hid_s = _ffn_up(hp, hs, row2(g), wg, wu, tm, tf)
            return _ffn_down(hid_p, hid_s, wd, hp, hs, tm, _pick(d_model, 512))

        h_p, h_s = ffn(h_p, h_s, g_ffn1[l], w1g, w1u, w1d)
        (q_p, q_s), (k_p, k_s), (v_p, v_s), (glu_p, glu_s) = _in_proj(
            h_p, h_s, row2(g_mix[l]), w_in16, qk_cols, attn_width, conv_width, tm, tn_proj)
        conv_args = (w_dw[l], row2(b_dw[l]), row2(g_conv_norm[l]), row2(b_conv_norm[l]))

        attn_p = _prompt_attn(q_p, k_p, v_p, g_sub, scal, batch, seq, n_heads, post_scale, tq)
        zero_hist = jnp.zeros((batch, CONV_KERNEL - 1, conv_width), F32)
        conv_p, hist_p = _conv_module(glu_p.reshape(batch, seq, conv_width), zero_hist, *conv_args)
        outs["kp"].append(k_p.reshape(batch, seq, n_heads, 2 * HEAD_DIM_QK))
        outs["vp"].append(v_p.reshape(batch, seq, n_heads, HEAD_DIM_V))
        outs["cp"].append(hist_p)

        k_new = k_s.reshape(dec_batch, dec_seq, n_heads, 2 * HEAD_DIM_QK)
        v_new = v_s.reshape(dec_batch, dec_seq, n_heads, HEAD_DIM_V)
        q5 = q_s.reshape(dec_batch, dec_seq, n_heads, 2, HEAD_DIM_QK)
        q2 = jnp.einsum("bthmd,mn->bhmtnd", q5, jnp.eye(2, dtype=BF16))
        q2 = q2.reshape(dec_batch, 2 * n_heads * dec_seq, 2 * HEAD_DIM_QK)
        attn_s = _decode_attn(q2, cache_k, cache_v, l, k_new, v_new, g_sub, page_table, scal, post_scale)
        conv_s, hist_s = _conv_module(glu_s.reshape(dec_batch, dec_seq, conv_width), state_conv[l], *conv_args)
        outs["ks"].append(k_new)
        outs["vs"].append(v_new)
        outs["cs"].append(hist_s)

        h_p, h_s = _out_proj(attn_p, conv_p.reshape(n_p, conv_width), attn_s.reshape(n_s, attn_width),
                             conv_s.reshape(n_s, conv_width), w_out16, h_p, h_s, min(tm, 512), d_model)
        h_p, h_s = ffn(h_p, h_s, g_ffn2[l], w2g, w2u, w2d)
        h_p, h_s = _ple(h_p, p_prompt[l].reshape(n_p, -1), h_s, p_sample[l].reshape(n_s, -1), row2(g_ple[l]),
                        w_pg16, w_pp16, gf, last, min(tm, 512))

    y_prompt = h_p.reshape(batch, seq, d_model)
    y_sample = h_s.reshape(dec_batch, dec_seq, d_model)
    return (y_prompt, y_sample, jnp.stack(outs["kp"]), jnp.stack(outs["vp"]), jnp.stack(outs["cp"]),
            jnp.stack(outs["ks"]), jnp.stack(outs["vs"]), jnp.stack(outs["cs"]))
```
